```python
import math
import jax, jax.numpy as jnp
from jax import lax
import numpy as np

D_MODEL = 1024
BATCH = 2
SEQ = 8192
DEPTH = 1

RWKV_HEAD_DIM = 64
RWKV_HEADS = D_MODEL // (2 * RWKV_HEAD_DIM)
RWKV_WIDTH = RWKV_HEADS * RWKV_HEAD_DIM
DECAY_LORA = 64
ICLR_LORA = 64
GATE_LORA = 128
GN_EPS = 64e-5
N_DIR = 2

DA_HEAD_DIM = 64
DA_HEADS = D_MODEL // (4 * DA_HEAD_DIM)
DA_V_DIM = 2 * DA_HEAD_DIM
DA_QK_WIDTH = DA_HEADS * 2 * DA_HEAD_DIM
DA_V_WIDTH = DA_HEADS * DA_V_DIM
SUBLN_EPS = 1e-5
Q_BLOCK = 128

ROPE_THETA = 500000.0
ROPE_DIM = DA_HEAD_DIM // 4

D_FF = 4 * D_MODEL
PLE_DIM = 256
N_BRANCH = 2
RMS_EPS = 1e-6

RWKV_COLS = 3 * RWKV_WIDTH + N_DIR * DECAY_LORA + N_DIR * ICLR_LORA + GATE_LORA
DA_COLS = 2 * DA_QK_WIDTH + DA_V_WIDTH
GATE_COLS = N_BRANCH * D_MODEL
IN_COLS = RWKV_COLS + DA_COLS + GATE_COLS

kernel_name = "hybrid_rwkv7_diffattn_gated_encoder"


def rmsnorm(x, g, eps=RMS_EPS):
    xf = x.astype(jnp.float32)
    y = xf * lax.rsqrt(jnp.mean(xf * xf, axis=-1, keepdims=True) + eps)
    return (y * g.astype(jnp.float32)).astype(x.dtype)


def centred_shift(u, mu_prev, mu_next):
    up = jnp.pad(u, ((0, 0), (1, 1), (0, 0)))
    return u + mu_prev * (up[:, :-2] - u) + mu_next * (up[:, 2:] - u)


def rwkv7_scan(r, w, k, v, kk, a, reverse):
    B, S, H, N = r.shape

    def step(state, inp):
        r_t, w_t, k_t, v_t, kk_t, a_t = inp
        sa = jnp.einsum('bhij,bhj->bhi', state, -kk_t)
        state = (state * w_t[:, :, None, :]
                 + sa[..., None] * (kk_t * a_t)[:, :, None, :]
                 + v_t[..., None] * k_t[:, :, None, :])
        y_t = jnp.einsum('bhij,bhj->bhi', state, r_t)
        return state, y_t

    xs = tuple(jnp.moveaxis(t, 1, 0) for t in (r, w, k, v, kk, a))
    s0 = jnp.zeros((B, H, N, N), jnp.float32)
    _, ys = lax.scan(step, s0, xs, reverse=reverse)
    return jnp.moveaxis(ys, 0, 1)


def rwkv7_branch(u, mu_prev, mu_next, w0, w2, a0, a2, g2, k_k, k_a, r_k, ln_w, ln_b, w_o):
    f32 = jnp.float32
    B, S, _ = u.shape
    H, N, W = RWKV_HEADS, RWKV_HEAD_DIM, RWKV_WIDTH
    u = centred_shift(u, mu_prev, mu_next).astype(f32)
    o1, o2, o3 = W, 2 * W, 3 * W
    o4 = o3 + N_DIR * DECAY_LORA
    o5 = o4 + N_DIR * ICLR_LORA
    r = u[..., :o1]
    k = u[..., o1:o2]
    v = u[..., o2:o3]
    wd = u[..., o3:o4].reshape(B, S, N_DIR, DECAY_LORA)
    ad = u[..., o4:o5].reshape(B, S, N_DIR, ICLR_LORA)
    gd = u[..., o5:]

    w_log = -jax.nn.softplus(-(w0.astype(f32) + jnp.einsum('bsdr,drc->bsdc', jnp.tanh(wd), w2.astype(f32)))) - 0.5
    decay = jnp.exp(-jnp.exp(w_log)).reshape(B, S, N_DIR, H, N)
    a = jax.nn.sigmoid(a0.astype(f32) + jnp.einsum('bsdr,drc->bsdc', ad, a2.astype(f32))).reshape(B, S, N_DIR, H, N)
    g = jax.nn.sigmoid(gd) @ g2.astype(f32)

    kk = (k * k_k.astype(f32)).reshape(B, S, H, N)
    kk = kk / jnp.maximum(jnp.sqrt(jnp.sum(kk * kk, axis=-1, keepdims=True)), 1e-12)
    r_h = r.reshape(B, S, H, N)
    v_h = v.reshape(B, S, H, N)
    k_dir = k.reshape(B, S, 1, H, N) * (1.0 + (a - 1.0) * k_a.astype(f32).reshape(H, N))

    y = (rwkv7_scan(r_h, decay[:, :, 0], k_dir[:, :, 0], v_h, kk, a[:, :, 0], False)
         + rwkv7_scan(r_h, decay[:, :, 1], k_dir[:, :, 1], v_h, kk, a[:, :, 1], True))

    mean = jnp.mean(y, axis=-1, keepdims=True)
    var = jnp.mean(jnp.square(y - mean), axis=-1, keepdims=True)
    y = ((y - mean) * lax.rsqrt(var + GN_EPS)).reshape(B, S, W)
    y = y * ln_w.astype(f32) + ln_b.astype(f32)
    bonus = jnp.sum(jnp.sum(r_h[:, :, None] * k_dir * r_k.astype(f32), axis=-1, keepdims=True), axis=2) * v_h
    y = (y + bonus.reshape(B, S, W)) * g
    return (y @ w_o.astype(f32)).astype(w_o.dtype)


def rope_tables(S):
    pos = jnp.arange(S, dtype=jnp.float32)
    inv_freq = ROPE_THETA ** (-jnp.arange(0, ROPE_DIM, 2, dtype=jnp.float32) / ROPE_DIM)
    ang = pos[:, None] * inv_freq[None, :]
    return jnp.cos(ang), jnp.sin(ang)


def partial_rope(x, cos, sin):
    half = ROPE_DIM // 2
    c = cos[:, None, None, :]
    s = sin[:, None, None, :]
    x1 = x[..., :half].astype(jnp.float32)
    x2 = x[..., half:ROPE_DIM].astype(jnp.float32)
    rot = jnp.concatenate([x1 * c - x2 * s, x2 * c + x1 * s], axis=-1)
    return jnp.concatenate([rot.astype(x.dtype), x[..., ROPE_DIM:]], axis=-1)


def diff_attn_branch(u, lq1, lk1, lq2, lk2, subln_w, w_o, lambda_init):
    f32 = jnp.float32
    B, S, _ = u.shape
    H, DH, DV = DA_HEADS, DA_HEAD_DIM, DA_V_DIM
    q = u[..., :DA_QK_WIDTH].reshape(B, S, H, 2, DH)
    k = u[..., DA_QK_WIDTH:2 * DA_QK_WIDTH].reshape(B, S, H, 2, DH)
    v = u[..., 2 * DA_QK_WIDTH:].reshape(B, S, H, DV)
    cos, sin = rope_tables(S)
    q = partial_rope(q, cos, sin) * (DH ** -0.5)
    k = partial_rope(k, cos, sin)
    lam = (jnp.exp(jnp.sum(lq1.astype(f32) * lk1.astype(f32)))
           - jnp.exp(jnp.sum(lq2.astype(f32) * lk2.astype(f32))) + lambda_init)

    kt = k.transpose(0, 2, 3, 1, 4)
    vt = v.transpose(0, 2, 1, 3)
    n_blk = S // Q_BLOCK
    qb = q.reshape(B, n_blk, Q_BLOCK, H, 2, DH).transpose(1, 0, 3, 4, 2, 5)

    def block(q_blk):
        s = jnp.einsum('bhcqd,bhckd->bhcqk', q_blk, kt).astype(f32)
        pr = jax.nn.softmax(s, axis=-1)
        pd = pr[:, :, 0] - lam * pr[:, :, 1]
        return jnp.einsum('bhqk,bhkv->bhqv', pd.astype(vt.dtype), vt)

    o = lax.map(block, qb)
    o = o.transpose(1, 0, 3, 2, 4).reshape(B, S, H, DV).astype(f32)
    o = o * lax.rsqrt(jnp.mean(o * o, axis=-1, keepdims=True) + SUBLN_EPS)
    o = o * subln_w.astype(f32) * (1.0 - lambda_init)
    return o.reshape(B, S, DA_V_WIDTH).astype(w_o.dtype) @ w_o


def setup_inputs(seed: int = 0) -> dict:
    key = jax.random.key(seed)
    ks = iter(jax.random.split(key, 40))
    L, D, W = DEPTH, D_MODEL, RWKV_WIDTH
    f32 = jnp.float32

    def nrm(shape, scale):
        return jax.random.normal(next(ks), shape, f32) * scale

    def gain(shape):
        return 1.0 + 0.05 * jax.random.normal(next(ks), shape, f32)

    def unif(shape, lo, hi):
        return jax.random.uniform(next(ks), shape, f32, lo, hi)

    return {
        "x": nrm((BATCH, SEQ, D), 1.0),
        "p": nrm((DEPTH, BATCH, SEQ, PLE_DIM), 1.0),
        "norm_mix": gain((L, D)),
        "w_in": nrm((L, D, IN_COLS), D ** -0.5),
        "shift_mu_prev": unif((L, RWKV_COLS), 0.0, 0.5),
        "shift_mu_next": unif((L, RWKV_COLS), 0.0, 0.5),
        "rwkv_w0": unif((L, N_DIR, W), -6.0, 1.0),
        "rwkv_w2": nrm((L, N_DIR, DECAY_LORA, W), 0.1 * DECAY_LORA ** -0.5),
        "rwkv_a0": nrm((L, N_DIR, W), 0.5),
        "rwkv_a2": nrm((L, N_DIR, ICLR_LORA, W), 0.1 * ICLR_LORA ** -0.5),
        "rwkv_g2": nrm((L, GATE_LORA, W), GATE_LORA ** -0.5),
        "rwkv_k_k": 0.85 + 0.05 * jax.random.normal(next(ks), (L, W), f32),
        "rwkv_k_a": gain((L, W)),
        "rwkv_r_k": nrm((L, RWKV_HEADS, RWKV_HEAD_DIM), 0.1),
        "rwkv_ln_w": gain((L, W)),
        "rwkv_ln_b": nrm((L, W), 0.01),
        "rwkv_w_o": nrm((L, W, D), W ** -0.5),
        "da_lq1": nrm((L, DA_HEAD_DIM), 0.1),
        "da_lk1": nrm((L, DA_HEAD_DIM), 0.1),
        "da_lq2": nrm((L, DA_HEAD_DIM), 0.1),
        "da_lk2": nrm((L, DA_HEAD_DIM), 0.1),
        "da_subln_w": gain((L, DA_V_DIM)),
        "da_w_o": nrm((L, DA_V_WIDTH, D), DA_V_WIDTH ** -0.5),
        "w_out": nrm((L, D, D), D ** -0.5),
        "norm_ffn": gain((L, D)),
        "w_ff1": nrm((L, D, D_FF), D ** -0.5),
        "w_ff2": nrm((L, D_FF, D), D_FF ** -0.5),
        "norm_ple": gain((L, D)),
        "w_ple_gate": nrm((L, D, D), D ** -0.5),
        "w_ple_proj": nrm((L, PLE_DIM, D), PLE_DIM ** -0.5),
        "norm_final": gain((D,)),
    }


def reference(x, p, norm_mix, w_in, shift_mu_prev, shift_mu_next, rwkv_w0, rwkv_w2, rwkv_a0,
              rwkv_a2, rwkv_g2, rwkv_k_k, rwkv_k_a, rwkv_r_k, rwkv_ln_w, rwkv_ln_b, rwkv_w_o,
              da_lq1, da_lk1, da_lq2, da_lk2, da_subln_w, da_w_o, w_out, norm_ffn, w_ff1, w_ff2,
              norm_ple, w_ple_gate, w_ple_proj, norm_final):
    B, S, D = x.shape
    for i in range(DEPTH):
        lambda_init = 0.8 - 0.6 * math.exp(-0.3 * i)
        h = rmsnorm(x, norm_mix[i])
        u = h @ w_in[i]
        u_rwkv = u[..., :RWKV_COLS]
        u_da = u[..., RWKV_COLS:RWKV_COLS + DA_COLS]
        gates = jax.nn.sigmoid(u[..., RWKV_COLS + DA_COLS:].astype(jnp.float32)).reshape(B, S, N_BRANCH, D)
        y_a = rwkv7_branch(u_rwkv, shift_mu_prev[i], shift_mu_next[i], rwkv_w0[i], rwkv_w2[i],
                           rwkv_a0[i], rwkv_a2[i], rwkv_g2[i], rwkv_k_k[i], rwkv_k_a[i], rwkv_r_k[i],
                           rwkv_ln_w[i], rwkv_ln_b[i], rwkv_w_o[i])
        y_b = diff_attn_branch(u_da, da_lq1[i], da_lk1[i], da_lq2[i], da_lk2[i], da_subln_w[i],
                               da_w_o[i], lambda_init)
        merged = (gates[:, :, 0] * y_a.astype(jnp.float32)
                  + gates[:, :, 1] * y_b.astype(jnp.float32)).astype(x.dtype)
        x = x + merged @ w_out[i]
        h = rmsnorm(x, norm_ffn[i])
        x = x + jnp.square(jax.nn.relu(h @ w_ff1[i])) @ w_ff2[i]
        h = rmsnorm(x, norm_ple[i])
        x = x + jax.nn.sigmoid(h @ w_ple_gate[i]) * (p[i] @ w_ple_proj[i])
    return rmsnorm(x, norm_final)
```

```python
import functools
import math

import jax
import jax.numpy as jnp
from jax import lax
from jax.experimental import pallas as pl
from jax.experimental.pallas import tpu as pltpu

F32 = jnp.float32
BF16 = jnp.bfloat16

HEAD = 64
LANES = 128
CHUNK = 64
RMS_EPS = 1e-6
GN_EPS = 64e-5
SUBLN_EPS = 1e-5
ROPE_THETA = 500000.0
ROPE_DIM = 16
Q_SCALE = HEAD ** -0.5
DECAY_SCALE = math.exp(-0.5)
VMEM_LIMIT = 56 * 1024 * 1024


def _rms(x, g, eps):
    return x * lax.rsqrt(jnp.mean(x * x, axis=-1, keepdims=True) + eps) * g


def _sigmoid(x):
    return 1.0 / (1.0 + jnp.exp(-x))


def _mm(a, b):
    return jnp.dot(a.astype(BF16), b.astype(BF16), preferred_element_type=F32)


def _mm_nt(a, b):
    return lax.dot_general(a.astype(BF16), b.astype(BF16), (((1,), (1,)), ((), ())),
                           preferred_element_type=F32)


def _split2(x):
    hi = x.astype(BF16)
    lo = (x - hi.astype(F32)).astype(BF16)
    return hi, lo


def _mm_exact_rhs(a, b_bf16):
    hi, lo = _split2(a)
    return (jnp.dot(hi, b_bf16, preferred_element_type=F32)
            + jnp.dot(lo, b_bf16, preferred_element_type=F32))


def _mm3(a, b):
    ah, al = _split2(a)
    bh, bl = _split2(b)
    return (jnp.dot(ah, bh, preferred_element_type=F32)
            + jnp.dot(ah, bl, preferred_element_type=F32)
            + jnp.dot(al, bh, preferred_element_type=F32))


def _inproj_kernel(x_ref, g_ref, wr_ref, wqk_ref, wv_ref, cos_ref, sin_ref,
                   ur_ref, q_ref, k_ref, v_ref):
    h = _rms(x_ref[...], g_ref[...], RMS_EPS).astype(BF16)
    ur_ref[...] = jnp.dot(h, wr_ref[...], preferred_element_type=F32)
    v_ref[...] = jnp.dot(h, wv_ref[...], preferred_element_type=F32).astype(BF16)
    cos_t = cos_ref[...]
    sin_t = sin_ref[...]
    lane = lax.broadcasted_iota(jnp.int32, cos_t.shape, 1)
    first_half = (lane & (HEAD - 1)) < (ROPE_DIM // 2)
    n_tiles = wqk_ref.shape[1] // LANES
    for c in range(n_tiles):
        xq = jnp.dot(h, wqk_ref[:, c * LANES:(c + 1) * LANES], preferred_element_type=F32)
        partner = jnp.where(first_half,
                            pltpu.roll(xq, LANES - ROPE_DIM // 2, 1),
                            pltpu.roll(xq, ROPE_DIM // 2, 1))
        ro = xq * cos_t + partner * sin_t
        if c < n_tiles // 2:
            q_ref[:, c * LANES:(c + 1) * LANES] = (ro * Q_SCALE).astype(BF16)
        else:
            cc = c - n_tiles // 2
            k_ref[:, cc * LANES:(cc + 1) * LANES] = ro.astype(BF16)


def _rope_tables(S):
    pos = jnp.arange(S, dtype=F32)
    inv_freq = ROPE_THETA ** (-jnp.arange(0, ROPE_DIM, 2, dtype=F32) / ROPE_DIM)
    ang = pos[:, None] * inv_freq[None, :]
    cos8, sin8 = jnp.cos(ang), jnp.sin(ang)
    ones = jnp.ones((S, HEAD - ROPE_DIM), F32)
    zeros = jnp.zeros((S, HEAD - ROPE_DIM), F32)
    cos64 = jnp.concatenate([cos8, cos8, ones], axis=1)
    sin64 = jnp.concatenate([-sin8, sin8, zeros], axis=1)
    return jnp.tile(cos64, (1, 2)), jnp.tile(sin64, (1, 2))


def _in_proj(x, g, w_r, w_qk, w_v, cos_t, sin_t, tm):
    B, S, D = x.shape
    nr, nqk, nv = w_r.shape[1], w_qk.shape[1], w_v.shape[1]
    const = lambda b, i: (0, 0)
    row = lambda b, i: (b, i, 0)
    return pl.pallas_call(
        _inproj_kernel,
        grid=(B, S // tm),
        in_specs=[
            pl.BlockSpec((None, tm, D), row),
            pl.BlockSpec((1, D), const),
            pl.BlockSpec((D, nr), const),
            pl.BlockSpec((D, nqk), const),
            pl.BlockSpec((D, nv), const),
            pl.BlockSpec((tm, LANES), lambda b, i: (i, 0)),
            pl.BlockSpec((tm, LANES), lambda b, i: (i, 0)),
        ],
        out_specs=[
            pl.BlockSpec((None, tm, nr), row),
            pl.BlockSpec((None, tm, nqk // 2), row),
            pl.BlockSpec((None, tm, nqk // 2), row),
            pl.BlockSpec((None, tm, nv), row),
        ],
        out_shape=[
            jax.ShapeDtypeStruct((B, S, nr), F32),
            jax.ShapeDtypeStruct((B, S, nqk // 2), BF16),
            jax.ShapeDtypeStruct((B, S, nqk // 2), BF16),
            jax.ShapeDtypeStruct((B, S, nv), BF16),
        ],
        compiler_params=pltpu.CompilerParams(
            dimension_semantics=("parallel", "parallel"), vmem_limit_bytes=VMEM_LIMIT),
        name="in_proj",
    )(x, g, w_r, w_qk, w_v, cos_t, sin_t)


def _prep_kernel(u_ref, up_ref, un_ref, mup_ref, mun_ref, w0_ref, w2_ref, a0_ref, a2_ref,
                 g2_ref, kk_w_ref, ka_ref, rk_ref, e_ref,
                 r_ref, v_ref, kk_ref, g_ref, bonus_ref, lw_ref, b_ref, kd_ref):
    i = pl.program_id(1)
    n = pl.num_programs(1)
    u = u_ref[...]
    tm = u.shape[0]
    W = r_ref.shape[-1]
    row = lax.broadcasted_iota(jnp.int32, (tm, 1), 0)
    prev_row = jnp.where(i > 0, up_ref[7:8, :], 0.0)
    next_row = jnp.where(i < n - 1, un_ref[0:1, :], 0.0)
    u_prev = jnp.where(row == 0, prev_row, pltpu.roll(u, 1, 0))
    u_next = jnp.where(row == tm - 1, next_row, pltpu.roll(u, tm - 1, 0))
    us = u + mup_ref[...] * (u_prev - u) + mun_ref[...] * (u_next - u)

    r = us[:, 0:W]
    k = us[:, W:2 * W]
    v = us[:, 2 * W:3 * W]
    wd = us[:, 3 * W:3 * W + LANES]
    ad = us[:, 3 * W + LANES:3 * W + 2 * LANES]
    gd = us[:, 3 * W + 2 * LANES:3 * W + 3 * LANES]

    z = w0_ref[...] + _mm(jnp.tanh(wd), w2_ref[...])
    lw = -(_sigmoid(z) * DECAY_SCALE)
    a = _sigmoid(a0_ref[...] + _mm(ad, a2_ref[...]))
    g_ref[...] = _mm(_sigmoid(gd), g2_ref[...])

    e = e_ref[...]
    kraw = k * kk_w_ref[...]
    ss = _mm_exact_rhs(kraw * kraw, e)
    kk = kraw / jnp.maximum(jnp.sqrt(ss), 1e-12)
    ka = ka_ref[...]
    kd_sum = jnp.zeros_like(k)
    for d in range(2):
        a_d = a[:, d * W:(d + 1) * W]
        kd = k * (1.0 + (a_d - 1.0) * ka)
        kd_sum = kd_sum + kd
        lw_ref[d] = lw[:, d * W:(d + 1) * W]
        b_ref[d] = kk * a_d
        kd_ref[d] = kd
    c = _mm_exact_rhs(r * rk_ref[...] * kd_sum, e)
    r_ref[...] = r
    v_ref[...] = v
    kk_ref[...] = kk
    bonus_ref[...] = c * v


def _rwkv_prep(u_r, mu_prev, mu_next, w0, w2bd, a0, a2bd, g2, k_k, k_a, r_k, e_seg, tm):
    B, S, C = u_r.shape
    W = k_k.shape[-1]
    nblk8 = S // 8
    const = lambda b, i: (0, 0)
    row = lambda b, i: (b, i, 0)
    drow = lambda b, i: (0, b, i, 0)
    tok = jax.ShapeDtypeStruct((B, S, W), F32)
    dtok = jax.ShapeDtypeStruct((2, B, S, W), F32)
    return pl.pallas_call(
        _prep_kernel,
        grid=(B, S // tm),
        in_specs=[
            pl.BlockSpec((None, tm, C), row),
            pl.BlockSpec((None, 8, C), lambda b, i: (b, jnp.maximum(i * (tm // 8) - 1, 0), 0)),
            pl.BlockSpec((None, 8, C), lambda b, i: (b, jnp.minimum((i + 1) * (tm // 8), nblk8 - 1), 0)),
            pl.BlockSpec((1, C), const),
            pl.BlockSpec((1, C), const),
            pl.BlockSpec((1, 2 * W), const),
            pl.BlockSpec((LANES, 2 * W), const),
            pl.BlockSpec((1, 2 * W), const),
            pl.BlockSpec((LANES, 2 * W), const),
            pl.BlockSpec((LANES, W), const),
            pl.BlockSpec((1, W), const),
            pl.BlockSpec((1, W), const),
            pl.BlockSpec((1, W), const),
            pl.BlockSpec((W, W), const),
        ],
        out_specs=[pl.BlockSpec((None, tm, W), row)] * 5
                  + [pl.BlockSpec((2, None, tm, W), drow)] * 3,
        out_shape=[tok] * 5 + [dtok] * 3,
        compiler_params=pltpu.CompilerParams(
            dimension_semantics=("parallel", "parallel"), vmem_limit_bytes=VMEM_LIMIT),
        name="rwkv_prep",
    )(u_r, u_r, u_r, mu_prev, mu_next, w0, w2bd, a0, a2bd, g2, k_k, k_a, r_k, e_seg)


def _scan_kernel(r_ref, v_ref, kk_ref, lw_ref, b_ref, kd_ref, y_ref, t_scr, *, nsub, n_pair):
    gidx = pl.program_id(0)
    step = pl.program_id(1)
    rev = (gidx // n_pair) % 2
    sgn = 1 - 2 * rev
    C = CHUNK
    P2 = 2 * C

    @pl.when(step == 0)
    def _():
        t_scr[...] = jnp.zeros_like(t_scr)

    row = lax.broadcasted_iota(jnp.int32, (P2, P2), 0)
    col = lax.broadcasted_iota(jnp.int32, (P2, P2), 1)
    delta = ((row & (C - 1)) - (col & (C - 1))) * sgn
    strict = delta > 0
    incl = delta >= 0
    eye = row == col
    rc = lax.broadcasted_iota(jnp.int32, (C, C), 0)
    cc = lax.broadcasted_iota(jnp.int32, (C, C), 1)
    lcum = jnp.where((rc - cc) * sgn >= 0, 1.0, 0.0).astype(BF16)
    lane = lax.broadcasted_iota(jnp.int32, (C, LANES), 1)
    head0 = lane < HEAD

    def stack(x):
        return jnp.concatenate([jnp.where(head0, x, 0.0), jnp.where(head0, 0.0, x)], axis=0)

    for s in range(nsub):
        ci = s + rev * (nsub - 1 - 2 * s)
        sl = pl.ds(pl.multiple_of(ci * C, C), C)
        lw = lw_ref[sl, :]
        r = r_ref[sl, :]
        v = v_ref[sl, :]
        kk = kk_ref[sl, :]
        b = b_ref[sl, :]
        kd = kd_ref[sl, :]

        l1 = lw.astype(BF16)
        rem = lw - l1.astype(F32)
        l2 = rem.astype(BF16)
        l3 = (rem - l2.astype(F32)).astype(BF16)
        cum = (jnp.dot(lcum, l1, preferred_element_type=F32)
               + jnp.dot(lcum, l2, preferred_element_type=F32)
               + jnp.dot(lcum, l3, preferred_element_type=F32))
        tot = jnp.sum(lw, axis=0, keepdims=True)
        e_pos = jnp.exp(cum)
        e_neg = jnp.exp(-cum)
        e_prev = jnp.exp(cum - lw)
        e_rem = jnp.exp(tot - cum)

        a_st = stack(-kk * e_prev)
        r_st = stack(r * e_pos)
        b_st = stack(b * e_neg)
        k_st = stack(kd * e_neg)
        bh_st = stack(b * e_rem)
        kh_st = stack(kd * e_rem)
        v_st = stack(v)

        lhs = jnp.concatenate([a_st, r_st], axis=0)
        rhs = jnp.concatenate([b_st, k_st], axis=0)
        sc = _mm3_nt(lhs, rhs)
        l_ab = jnp.where(strict, sc[:P2, :P2], 0.0)
        l_ak = jnp.where(strict, sc[:P2, P2:], 0.0)
        a_rb = jnp.where(incl, sc[P2:, :P2], 0.0)
        a_rk = jnp.where(incl, sc[P2:, P2:], 0.0)

        m_inv = jnp.where(eye, 1.0, 0.0) + l_ab
        pw = l_ab
        for _ in range(int(math.log2(C)) - 1):
            pw = _mm3(pw, pw)
            m_inv = m_inv + _mm3(pw, m_inv)

        x_loc = _mm3(l_ak, v_st)
        wu = _mm3(m_inv, jnp.concatenate([a_st, x_loc], axis=1))
        qy = _mm3(a_rb, wu)
        q_hat = r_st + qy[:, :LANES]
        y_loc = qy[:, LANES:] + _mm3(a_rk, v_st)
        w_til = wu[:, :LANES]
        u_til = wu[:, LANES:]
        g_mat = jnp.where(eye, jnp.exp(tot), 0.0) + _mm3_tn(bh_st, w_til)
        h_mat = _mm3_tn(bh_st, u_til) + _mm3_tn(kh_st, v_st)

        t0 = t_scr[...]
        y_st = _mm3(q_hat, t0) + y_loc
        t_scr[...] = _mm3(g_mat, t0) + h_mat
        y_ref[sl, :] = y_st[:C, :] + y_st[C:, :]


def _mm3_nt(a, b):
    ah, al = _split2(a)
    bh, bl = _split2(b)
    dn = (((1,), (1,)), ((), ()))
    return (lax.dot_general(ah, bh, dn, preferred_element_type=F32)
            + lax.dot_general(ah, bl, dn, preferred_element_type=F32)
            + lax.dot_general(al, bh, dn, preferred_element_type=F32))


def _mm3_tn(a, b):
    return _mm3(a.T, b)


def _rwkv_scan(r, v, kk, lw, bb, kd, nsub):
    B, S, W = r.shape
    n_pair = W // LANES
    T = nsub * CHUNK
    n_step = S // T

    def shared(g, c):
        d = (g // n_pair) % 2
        return (g // (2 * n_pair), c + d * (n_step - 1 - 2 * c), g % n_pair)

    def per_dir(g, c):
        d = (g // n_pair) % 2
        return (d, g // (2 * n_pair), c + d * (n_step - 1 - 2 * c), g % n_pair)

    return pl.pallas_call(
        functools.partial(_scan_kernel, nsub=nsub, n_pair=n_pair),
        grid=(B * 2 * n_pair, n_step),
        in_specs=[pl.BlockSpec((None, T, LANES), shared)] * 3
                 + [pl.BlockSpec((None, None, T, LANES), per_dir)] * 3,
        out_specs=pl.BlockSpec((None, None, T, LANES), per_dir),
        out_shape=jax.ShapeDtypeStruct((2, B, S, W), F32),
        scratch_shapes=[pltpu.VMEM((LANES, LANES), F32)],
        compiler_params=pltpu.CompilerParams(
            dimension_semantics=("parallel", "arbitrary"), vmem_limit_bytes=VMEM_LIMIT),
        name="rwkv_scan",
    )(r, v, kk, lw, bb, kd)


def _attn_kernel(q_ref, k_ref, v_ref, lq1_ref, lk1_ref, lq2_ref, lk2_ref, sw_ref, o_ref,
                 m_scr, l_scr, acc_scr, *, tk, lambda_init):
    tq = q_ref.shape[0]
    n_kv = k_ref.shape[0] // tk
    q = q_ref[...]
    lane = lax.broadcasted_iota(jnp.int32, q.shape, 1)
    qc = (jnp.where(lane < HEAD, q, jnp.zeros_like(q)), jnp.where(lane < HEAD, jnp.zeros_like(q), q))
    m_scr[...] = jnp.full(m_scr.shape, -jnp.inf, F32)
    l_scr[...] = jnp.zeros(l_scr.shape, F32)
    acc_scr[...] = jnp.zeros(acc_scr.shape, F32)

    def body(j, carry):
        ks = k_ref[pl.ds(pl.multiple_of(j * tk, tk), tk), :]
        vs = v_ref[pl.ds(pl.multiple_of(j * tk, tk), tk), :]
        for c in range(2):
            s = lax.dot_general(qc[c], ks, (((1,), (1,)), ((), ())), preferred_element_type=F32)
            m_old = m_scr[c]
            m_new = jnp.maximum(m_old, jnp.max(s, axis=-1, keepdims=True))
            alpha = jnp.exp(m_old - m_new)
            p = jnp.exp(s - m_new[:, 0:1])
            l_scr[c] = alpha * l_scr[c] + jnp.sum(p, axis=-1, keepdims=True)
            acc_scr[c] = alpha * acc_scr[c] + jnp.dot(p.astype(BF16), vs, preferred_element_type=F32)
            m_scr[c] = m_new
        return carry

    lax.fori_loop(0, n_kv, body, 0)

    lam = (jnp.exp(jnp.sum(lq1_ref[...] * lk1_ref[...], axis=-1, keepdims=True))
           - jnp.exp(jnp.sum(lq2_ref[...] * lk2_ref[...], axis=-1, keepdims=True)) + lambda_init)
    o = acc_scr[0] / l_scr[0] - lam * (acc_scr[1] / l_scr[1])
    o = o * lax.rsqrt(jnp.mean(o * o, axis=-1, keepdims=True) + SUBLN_EPS)
    o_ref[...] = (o * sw_ref[...] * (1.0 - lambda_init)).astype(o_ref.dtype)


def _diff_attn(q, k, v, lq1, lk1, lq2, lk2, subln_w, lambda_init, tq, tk):
    B, S, QW = q.shape
    H = QW // LANES
    const = lambda b, h, i: (0, 0)
    return pl.pallas_call(
        functools.partial(_attn_kernel, tk=tk, lambda_init=lambda_init),
        grid=(B, H, S // tq),
        in_specs=[
            pl.BlockSpec((None, tq, LANES), lambda b, h, i: (b, i, h)),
            pl.BlockSpec((None, S, LANES), lambda b, h, i: (b, 0, h)),
            pl.BlockSpec((None, S, LANES), lambda b, h, i: (b, 0, h)),
            pl.BlockSpec((1, HEAD), const),
            pl.BlockSpec((1, HEAD), const),
            pl.BlockSpec((1, HEAD), const),
            pl.BlockSpec((1, HEAD), const),
            pl.BlockSpec((1, LANES), const),
        ],
        out_specs=pl.BlockSpec((None, tq, LANES), lambda b, h, i: (b, i, h)),
        out_shape=jax.ShapeDtypeStruct((B, S, QW), BF16),
        scratch_shapes=[pltpu.VMEM((2, tq, LANES), F32),
                        pltpu.VMEM((2, tq, LANES), F32),
                        pltpu.VMEM((2, tq, LANES), F32)],
        compiler_params=pltpu.CompilerParams(
            dimension_semantics=("parallel", "parallel", "arbitrary"),
            vmem_limit_bytes=VMEM_LIMIT),
        name="diff_attn",
    )(q, k, v, lq1, lk1, lq2, lk2, subln_w)


def _merge_kernel(x_ref, y_ref, bonus_ref, g_ref, ob_ref, nm_ref, wg_ref, lnw_ref, lnb_ref,
                  e_ref, woa_ref, wob_ref, wout_ref, o_ref):
    x = x_ref[...]
    D = x.shape[-1]
    h = _rms(x, nm_ref[...], RMS_EPS).astype(BF16)
    gates = _sigmoid(jnp.dot(h, wg_ref[...], preferred_element_type=F32))
    e = e_ref[...]
    y = y_ref[0] + y_ref[1]
    mean = _mm_exact_rhs(y, e) * (1.0 / HEAD)
    yc = y - mean
    var = _mm_exact_rhs(yc * yc, e) * (1.0 / HEAD)
    yn = yc * lax.rsqrt(var + GN_EPS) * lnw_ref[...] + lnb_ref[...]
    ya = _mm((yn + bonus_ref[...]) * g_ref[...], woa_ref[...])
    yb = jnp.dot(ob_ref[...], wob_ref[...], preferred_element_type=F32)
    merged = gates[:, :D] * ya + gates[:, D:] * yb
    o_ref[...] = x + _mm(merged, wout_ref[...])


def _merge(x, y, bonus, g, ob, norm_mix, w_g, ln_w, ln_b, e_seg, w_oa, w_ob, w_out, tm):
    B, S, D = x.shape
    W = bonus.shape[-1]
    const = lambda b, i: (0, 0)
    row = lambda b, i: (b, i, 0)
    return pl.pallas_call(
        _merge_kernel,
        grid=(B, S // tm),
        in_specs=[
            pl.BlockSpec((None, tm, D), row),
            pl.BlockSpec((2, None, tm, W), lambda b, i: (0, b, i, 0)),
            pl.BlockSpec((None, tm, W), row),
            pl.BlockSpec((None, tm, W), row),
            pl.BlockSpec((None, tm, W), row),
            pl.BlockSpec((1, D), const),
            pl.BlockSpec((D, 2 * D), const),
            pl.BlockSpec((1, W), const),
            pl.BlockSpec((1, W), const),
            pl.BlockSpec((W, W), const),
            pl.BlockSpec((W, D), const),
            pl.BlockSpec((W, D), const),
            pl.BlockSpec((D, D), const),
        ],
        out_specs=pl.BlockSpec((None, tm, D), row),
        out_shape=jax.ShapeDtypeStruct((B, S, D), F32),
        compiler_params=pltpu.CompilerParams(
            dimension_semantics=("parallel", "parallel"), vmem_limit_bytes=VMEM_LIMIT),
        name="merge",
    )(x, y, bonus, g, ob, norm_mix, w_g, ln_w, ln_b, e_seg, w_oa, w_ob, w_out)


def _ffn_kernel(x_ref, p_ref, nf_ref, w1_ref, w2_ref, np_ref, wpg_ref, wpp_ref, nfin_ref, o_ref,
                *, n_chunk):
    x = x_ref[...]
    h = _rms(x, nf_ref[...], RMS_EPS).astype(BF16)
    ff = w1_ref.shape[1] // n_chunk
    acc = x
    for c in range(n_chunk):
        t = jnp.maximum(jnp.dot(h, w1_ref[:, c * ff:(c + 1) * ff], preferred_element_type=F32), 0.0)
        acc = acc + jnp.dot((t * t).astype(BF16), w2_ref[c * ff:(c + 1) * ff, :],
                            preferred_element_type=F32)
    x = acc
    h = _rms(x, np_ref[...], RMS_EPS).astype(BF16)
    gate = _sigmoid(jnp.dot(h, wpg_ref[...], preferred_element_type=F32))
    x = x + gate * _mm(p_ref[...], wpp_ref[...])
    o_ref[...] = x
    if nfin_ref is not None:
        o_ref[...] = _rms(x, nfin_ref[...], RMS_EPS)


def _ffn_kernel_plain(x_ref, p_ref, nf_ref, w1_ref, w2_ref, np_ref, wpg_ref, wpp_ref, o_ref, *, n_chunk):
    _ffn_kernel(x_ref, p_ref, nf_ref, w1_ref, w2_ref, np_ref, wpg_ref, wpp_ref, None, o_ref,
                n_chunk=n_chunk)


def _ffn(x, p, norm_ffn, w1, w2, norm_ple, w_pg, w_pp, norm_final, tm, n_chunk):
    B, S, D = x.shape
    PD = p.shape[-1]
    FF = w1.shape[1]
    const = lambda b, i: (0, 0)
    row = lambda b, i: (b, i, 0)
    in_specs = [
        pl.BlockSpec((None, tm, D), row),
        pl.BlockSpec((None, tm, PD), row),
        pl.BlockSpec((1, D), const),
        pl.BlockSpec((D, FF), const, pipeline_mode=pl.Buffered(1)),
        pl.BlockSpec((FF, D), const, pipeline_mode=pl.Buffered(1)),
        pl.BlockSpec((1, D), const),
        pl.BlockSpec((D, D), const, pipeline_mode=pl.Buffered(1)),
        pl.BlockSpec((PD, D), const, pipeline_mode=pl.Buffered(1)),
    ]
    args = [x, p, norm_ffn, w1, w2, norm_ple, w_pg, w_pp]
    if norm_final is not None:
        in_specs.append(pl.BlockSpec((1, D), const))
        args.append(norm_final)
        body = functools.partial(_ffn_kernel, n_chunk=n_chunk)
    else:
        body = functools.partial(_ffn_kernel_plain, n_chunk=n_chunk)
    return pl.pallas_call(
        body,
        grid=(B, S // tm),
        in_specs=in_specs,
        out_specs=pl.BlockSpec((None, tm, D), row),
        out_shape=jax.ShapeDtypeStruct((B, S, D), F32),
        compiler_params=pltpu.CompilerParams(
            dimension_semantics=("parallel", "parallel"), vmem_limit_bytes=VMEM_LIMIT),
        name="ffn",
    )(*args)


def _block_diag2(w):
    z = jnp.zeros_like(w[0])
    return jnp.concatenate([jnp.concatenate([w[0], z], axis=1),
                            jnp.concatenate([z, w[1]], axis=1)], axis=0)


def kernel(x, p, norm_mix, w_in, shift_mu_prev, shift_mu_next, rwkv_w0, rwkv_w2, rwkv_a0,
           rwkv_a2, rwkv_g2, rwkv_k_k, rwkv_k_a, rwkv_r_k, rwkv_ln_w, rwkv_ln_b, rwkv_w_o,
           da_lq1, da_lk1, da_lq2, da_lk2, da_subln_w, da_w_o, w_out, norm_ffn, w_ff1, w_ff2,
           norm_ple, w_ple_gate, w_ple_proj, norm_final):
    B, S, D = x.shape
    L = w_in.shape[0]
    W = rwkv_w0.shape[-1]
    n_r = shift_mu_prev.shape[-1]
    n_qk = 2 * da_w_o.shape[1]
    n_v = da_w_o.shape[1]
    cos_t, sin_t = _rope_tables(S)
    seg = jnp.arange(W) // HEAD
    e_seg = (seg[:, None] == seg[None, :]).astype(BF16)
    tm = min(256, S)
    for i in range(L):
        lambda_init = 0.8 - 0.6 * math.exp(-0.3 * i)
        w_i = w_in[i].astype(BF16)
        u_r, q, k, v = _in_proj(
            x, norm_mix[i][None], w_i[:, :n_r], w_i[:, n_r:n_r + n_qk],
            w_i[:, n_r + n_qk:n_r + n_qk + n_v], cos_t, sin_t, tm)
        r, vv, kk, g, bonus, lw, bb, kd = _rwkv_prep(
            u_r, shift_mu_prev[i][None], shift_mu_next[i][None],
            rwkv_w0[i].reshape(1, 2 * W), _block_diag2(rwkv_w2[i]).astype(BF16),
            rwkv_a0[i].reshape(1, 2 * W), _block_diag2(rwkv_a2[i]).astype(BF16),
            rwkv_g2[i].astype(BF16), rwkv_k_k[i][None], rwkv_k_a[i][None],
            rwkv_r_k[i].reshape(1, W), e_seg, tm)
        y = _rwkv_scan(r, vv, kk, lw, bb, kd, nsub=min(4, S // CHUNK))
        ob = _diff_attn(q, k, v, da_lq1[i][None], da_lk1[i][None], da_lq2[i][None],
                        da_lk2[i][None], da_subln_w[i][None], lambda_init,
                        tq=min(512, S), tk=min(512, S))
        x = _merge(x, y, bonus, g, ob, norm_mix[i][None],
                   w_i[:, n_r + n_qk + n_v:], rwkv_ln_w[i][None], rwkv_ln_b[i][None], e_seg,
                   rwkv_w_o[i].astype(BF16), da_w_o[i].astype(BF16), w_out[i].astype(BF16), tm)
        x = _ffn(x, p[i], norm_ffn[i][None], w_ff1[i].astype(BF16), w_ff2[i].astype(BF16),
                 norm_ple[i][None], w_ple_gate[i].astype(BF16), w_ple_proj[i].astype(BF16),
                 norm_final[None] if i == L - 1 else None, tm, n_chunk=4)
    return x
```

```python
import functools
import math

import jax
import jax.numpy as jnp
from jax import lax
from jax.experimental import pallas as pl
from jax.experimental.pallas import tpu as pltpu

F32 = jnp.float32
BF16 = jnp.bfloat16

HEAD = 64
LANES = 128
CHUNK = 64
RMS_EPS = 1e-6
GN_EPS = 64e-5
SUBLN_EPS = 1e-5
ROPE_THETA = 500000.0
ROPE_DIM = 16
Q_SCALE = HEAD ** -0.5
DECAY_SCALE = math.exp(-0.5)
VMEM_LIMIT = 56 * 1024 * 1024


def _rms(x, g, eps):
    return x * lax.rsqrt(jnp.mean(x * x, axis=-1, keepdims=True) + eps) * g


def _sigmoid(x):
    return 1.0 / (1.0 + jnp.exp(-x))


def _mm(a, b):
    return jnp.dot(a.astype(BF16), b.astype(BF16), preferred_element_type=F32)


def _mm_nt(a, b):
    return lax.dot_general(a.astype(BF16), b.astype(BF16), (((1,), (1,)), ((), ())),
                           preferred_element_type=F32)


def _split2(x):
    hi = x.astype(BF16)
    lo = (x - hi.astype(F32)).astype(BF16)
    return hi, lo


def _mm_exact_rhs(a, b_bf16):
    hi, lo = _split2(a)
    return (jnp.dot(hi, b_bf16, preferred_element_type=F32)
            + jnp.dot(lo, b_bf16, preferred_element_type=F32))


def _mm3(a, b):
    ah, al = _split2(a)
    bh, bl = _split2(b)
    return (jnp.dot(ah, bh, preferred_element_type=F32)
            + jnp.dot(ah, bl, preferred_element_type=F32)
            + jnp.dot(al, bh, preferred_element_type=F32))


def _inproj_kernel(x_ref, g_ref, wr_ref, wqk_ref, wv_ref, cos_ref, sin_ref,
                   ur_ref, q_ref, k_ref, v_ref):
    h = _rms(x_ref[...], g_ref[...], RMS_EPS).astype(BF16)
    ur_ref[...] = jnp.dot(h, wr_ref[...], preferred_element_type=F32)
    v_ref[...] = jnp.dot(h, wv_ref[...], preferred_element_type=F32).astype(BF16)
    cos_t = cos_ref[...]
    sin_t = sin_ref[...]
    lane = lax.broadcasted_iota(jnp.int32, cos_t.shape, 1)
    first_half = (lane & (HEAD - 1)) < (ROPE_DIM // 2)
    n_tiles = wqk_ref.shape[1] // LANES
    for c in range(n_tiles):
        xq = jnp.dot(h, wqk_ref[:, c * LANES:(c + 1) * LANES], preferred_element_type=F32)
        partner = jnp.where(first_half,
                            pltpu.roll(xq, LANES - ROPE_DIM // 2, 1),
                            pltpu.roll(xq, ROPE_DIM // 2, 1))
        ro = xq * cos_t + partner * sin_t
        if c < n_tiles // 2:
            q_ref[:, c * LANES:(c + 1) * LANES] = (ro * Q_SCALE).astype(BF16)
        else:
            cc = c - n_tiles // 2
            k_ref[:, cc * LANES:(cc + 1) * LANES] = ro.astype(BF16)


def _rope_tables(S):
    pos = jnp.arange(S, dtype=F32)
    inv_freq = ROPE_THETA ** (-jnp.arange(0, ROPE_DIM, 2, dtype=F32) / ROPE_DIM)
    ang = pos[:, None] * inv_freq[None, :]
    cos8, sin8 = jnp.cos(ang), jnp.sin(ang)
    ones = jnp.ones((S, HEAD - ROPE_DIM), F32)
    zeros = jnp.zeros((S, HEAD - ROPE_DIM), F32)
    cos64 = jnp.concatenate([cos8, cos8, ones], axis=1)
    sin64 = jnp.concatenate([-sin8, sin8, zeros], axis=1)
    return jnp.tile(cos64, (1, 2)), jnp.tile(sin64, (1, 2))


def _in_proj(x, g, w_r, w_qk, w_v, cos_t, sin_t, tm):
    B, S, D = x.shape
    nr, nqk, nv = w_r.shape[1], w_qk.shape[1], w_v.shape[1]
    const = lambda b, i: (0, 0)
    row = lambda b, i: (b, i, 0)
    return pl.pallas_call(
        _inproj_kernel,
        grid=(B, S // tm),
        in_specs=[
            pl.BlockSpec((None, tm, D), row),
            pl.BlockSpec((1, D), const),
            pl.BlockSpec((D, nr), const),
            pl.BlockSpec((D, nqk), const),
            pl.BlockSpec((D, nv), const),
            pl.BlockSpec((tm, LANES), lambda b, i: (i, 0)),
            pl.BlockSpec((tm, LANES), lambda b, i: (i, 0)),
        ],
        out_specs=[
            pl.BlockSpec((None, tm, nr), row),
            pl.BlockSpec((None, tm, nqk // 2), row),
            pl.BlockSpec((None, tm, nqk // 2), row),
            pl.BlockSpec((None, tm, nv), row),
        ],
        out_shape=[
            jax.ShapeDtypeStruct((B, S, nr), F32),
            jax.ShapeDtypeStruct((B, S, nqk // 2), BF16),
            jax.ShapeDtypeStruct((B, S, nqk // 2), BF16),
            jax.ShapeDtypeStruct((B, S, nv), BF16),
        ],
        compiler_params=pltpu.CompilerParams(
            dimension_semantics=("parallel", "parallel"), vmem_limit_bytes=VMEM_LIMIT),
        name="in_proj",
    )(x, g, w_r, w_qk, w_v, cos_t, sin_t)


def _prep_kernel(u_ref, up_ref, un_ref, mup_ref, mun_ref, w0_ref, w2_ref, a0_ref, a2_ref,
                 g2_ref, kk_w_ref, ka_ref, rk_ref, e_ref,
                 r_ref, v_ref, kk_ref, g_ref, bonus_ref, lw_ref, b_ref, kd_ref):
    i = pl.program_id(1)
    n = pl.num_programs(1)
    u = u_ref[...]
    tm = u.shape[0]
    W = r_ref.shape[-1]
    row = lax.broadcasted_iota(jnp.int32, (tm, 1), 0)
    prev_row = jnp.where(i > 0, up_ref[7:8, :], 0.0)
    next_row = jnp.where(i < n - 1, un_ref[0:1, :], 0.0)
    u_prev = jnp.where(row == 0, prev_row, pltpu.roll(u, 1, 0))
    u_next = jnp.where(row == tm - 1, next_row, pltpu.roll(u, tm - 1, 0))
    us = u + mup_ref[...] * (u_prev - u) + mun_ref[...] * (u_next - u)

    r = us[:, 0:W]
    k = us[:, W:2 * W]
    v = us[:, 2 * W:3 * W]
    wd = us[:, 3 * W:3 * W + LANES]
    ad = us[:, 3 * W + LANES:3 * W + 2 * LANES]
    gd = us[:, 3 * W + 2 * LANES:3 * W + 3 * LANES]

    z = w0_ref[...] + _mm(jnp.tanh(wd), w2_ref[...])
    lw = -(_sigmoid(z) * DECAY_SCALE)
    a = _sigmoid(a0_ref[...] + _mm(ad, a2_ref[...]))
    g_ref[...] = _mm(_sigmoid(gd), g2_ref[...])

    e = e_ref[...]
    kraw = k * kk_w_ref[...]
    ss = _mm_exact_rhs(kraw * kraw, e)
    kk = kraw / jnp.maximum(jnp.sqrt(ss), 1e-12)
    ka = ka_ref[...]
    kd_sum = jnp.zeros_like(k)
    for d in range(2):
        a_d = a[:, d * W:(d + 1) * W]
        kd = k * (1.0 + (a_d - 1.0) * ka)
        kd_sum = kd_sum + kd
        lw_ref[d] = lw[:, d * W:(d + 1) * W]
        b_ref[d] = kk * a_d
        kd_ref[d] = kd
    c = _mm_exact_rhs(r * rk_ref[...] * kd_sum, e)
    r_ref[...] = r
    v_ref[...] = v
    kk_ref[...] = kk
    bonus_ref[...] = c * v


def _rwkv_prep(u_r, mu_prev, mu_next, w0, w2bd, a0, a2bd, g2, k_k, k_a, r_k, e_seg, tm):
    B, S, C = u_r.shape
    W = k_k.shape[-1]
    nblk8 = S // 8
    const = lambda b, i: (0, 0)
    row = lambda b, i: (b, i, 0)
    drow = lambda b, i: (0, b, i, 0)
    tok = jax.ShapeDtypeStruct((B, S, W), F32)
    dtok = jax.ShapeDtypeStruct((2, B, S, W), F32)
    return pl.pallas_call(
        _prep_kernel,
        grid=(B, S // tm),
        in_specs=[
            pl.BlockSpec((None, tm, C), row),
            pl.BlockSpec((None, 8, C), lambda b, i: (b, jnp.maximum(i * (tm // 8) - 1, 0), 0)),
            pl.BlockSpec((None, 8, C), lambda b, i: (b, jnp.minimum((i + 1) * (tm // 8), nblk8 - 1), 0)),
            pl.BlockSpec((1, C), const),
            pl.BlockSpec((1, C), const),
            pl.BlockSpec((1, 2 * W), const),
            pl.BlockSpec((LANES, 2 * W), const),
            pl.BlockSpec((1, 2 * W), const),
            pl.BlockSpec((LANES, 2 * W), const),
            pl.BlockSpec((LANES, W), const),
            pl.BlockSpec((1, W), const),
            pl.BlockSpec((1, W), const),
            pl.BlockSpec((1, W), const),
            pl.BlockSpec((W, W), const),
        ],
        out_specs=[pl.BlockSpec((None, tm, W), row)] * 5
                  + [pl.BlockSpec((2, None, tm, W), drow)] * 3,
        out_shape=[tok] * 5 + [dtok] * 3,
        compiler_params=pltpu.CompilerParams(
            dimension_semantics=("parallel", "parallel"), vmem_limit_bytes=VMEM_LIMIT),
        name="rwkv_prep",
    )(u_r, u_r, u_r, mu_prev, mu_next, w0, w2bd, a0, a2bd, g2, k_k, k_a, r_k, e_seg)


def _scan_kernel(r_ref, v_ref, kk_ref, lw_ref, b_ref, kd_ref, y_ref,
                 t_scr, g_scr, h_scr, q_scr, yl_scr, *, nsub, n_pair):
    gidx = pl.program_id(0)
    step = pl.program_id(1)
    rev = (gidx // n_pair) % 2
    sgn = 1 - 2 * rev
    C = CHUNK
    P2 = 2 * C

    @pl.when(step == 0)
    def _():
        t_scr[...] = jnp.zeros_like(t_scr)
        g_scr[...] = jnp.zeros_like(g_scr)
        h_scr[...] = jnp.zeros_like(h_scr)
        q_scr[...] = jnp.zeros_like(q_scr)
        yl_scr[...] = jnp.zeros_like(yl_scr)

    chunk_slices = []
    for s in range(nsub):
        ci = s + rev * (nsub - 1 - 2 * s)
        chunk_slices.append(pl.ds(pl.multiple_of(ci * C, C), C))

    chain = {"t": t_scr[...], "next": 0}

    def chain_steps(n):
        for _ in range(n):
            s = chain["next"]
            if s == nsub:
                return
            t_cur = chain["t"]
            y_st = _mm(q_scr[s], t_cur) + yl_scr[s]
            chain["t"] = _mm3(g_scr[s], t_cur) + h_scr[s]
            y_ref[chunk_slices[s], :] = y_st[:C, :] + y_st[C:, :]
            chain["next"] = s + 1

    row = lax.broadcasted_iota(jnp.int32, (P2, P2), 0)
    col = lax.broadcasted_iota(jnp.int32, (P2, P2), 1)
    rt = row & (C - 1)
    ct = col & (C - 1)
    delta = (rt - ct) * sgn
    strict = delta > 0
    incl = delta >= 0
    eye = row == col
    ident = jnp.where(eye, 1.0, 0.0)
    blk8 = (rt >> 3) == (ct >> 3)
    blk16 = (rt >> 4) == (ct >> 4)
    blk32 = (rt >> 5) == (ct >> 5)
    in8 = strict & blk8
    in16 = strict & blk16 & ~blk8
    in32 = strict & blk32 & ~blk16
    in64 = strict & ~blk32
    rc = lax.broadcasted_iota(jnp.int32, (C, C), 0)
    cc = lax.broadcasted_iota(jnp.int32, (C, C), 1)
    lcum = jnp.where((rc - cc) * sgn >= 0, 1.0, 0.0).astype(BF16)
    lane = lax.broadcasted_iota(jnp.int32, (C, LANES), 1)
    head0 = lane < HEAD

    def stack(x):
        return jnp.concatenate([jnp.where(head0, x, 0.0), jnp.where(head0, 0.0, x)], axis=0)

    subs = range(nsub)
    sls = chunk_slices

    lws = [lw_ref[sl, :] for sl in sls]
    cums = []
    for lw in lws:
        l1 = lw.astype(BF16)
        rem = lw - l1.astype(F32)
        l2 = rem.astype(BF16)
        l3 = (rem - l2.astype(F32)).astype(BF16)
        cums.append(jnp.dot(lcum, l1, preferred_element_type=F32)
                    + jnp.dot(lcum, l2, preferred_element_type=F32)
                    + jnp.dot(lcum, l3, preferred_element_type=F32))
    tots = [jnp.sum(lw, axis=0, keepdims=True) for lw in lws]
    a_st, r_st, b_st, k_st, bh_t, kh_t, v_st = [], [], [], [], [], [], []
    for s in subs:
        sl, lw, cum, tot = sls[s], lws[s], cums[s], tots[s]
        e_neg = jnp.exp(-cum)
        e_rem = jnp.exp(tot - cum)
        b = b_ref[sl, :]
        kd = kd_ref[sl, :]
        a_st.append(stack(kk_ref[sl, :] * -jnp.exp(cum - lw)).astype(BF16))
        r_st.append(stack(r_ref[sl, :] * jnp.exp(cum)))
        b_st.append(stack(b * e_neg).astype(BF16))
        k_st.append(stack(kd * e_neg).astype(BF16))
        bh_t.append(stack(b * e_rem).T.astype(BF16))
        kh_t.append(stack(kd * e_rem).T.astype(BF16))
        v_st.append(stack(v_ref[sl, :]).astype(BF16))

    nt = (((1,), (1,)), ((), ()))
    sc = [lax.dot_general(jnp.concatenate([a_st[s], r_st[s].astype(BF16)], axis=0),
                          jnp.concatenate([b_st[s], k_st[s]], axis=0), nt,
                          preferred_element_type=F32) for s in subs]
    per_stage = -(-nsub // 8)
    chain_steps(per_stage)
    l_ab = [x[:P2, :P2] for x in sc]
    l_ak = [jnp.where(strict, x[:P2, P2:], 0.0).astype(BF16) for x in sc]
    a_rb = [jnp.where(incl, x[P2:, :P2], 0.0).astype(BF16) for x in sc]
    a_rk = [jnp.where(incl, x[P2:, P2:], 0.0).astype(BF16) for x in sc]
    x_loc = [jnp.dot(l_ak[s], v_st[s], preferred_element_type=F32) for s in subs]
    y_rkv = [jnp.dot(a_rk[s], v_st[s], preferred_element_type=F32) for s in subs]
    chain_steps(per_stage)

    d8 = [jnp.where(in8, x, 0.0) for x in l_ab]
    m1 = [ident + x for x in d8]
    p1 = [_mm(x, x) for x in d8]
    chain_steps(per_stage)
    pm = [_mm(p1[s], jnp.concatenate([p1[s], m1[s]], axis=1)) for s in subs]
    m2 = [m1[s] + pm[s][:, P2:] for s in subs]
    m8 = [m2[s] + _mm(pm[s][:, :P2], m2[s]) for s in subs]
    chain_steps(per_stage)
    o16 = [jnp.where(in16, x, 0.0) for x in l_ab]
    om = [_mm(o16[s], m8[s]) for s in subs]
    m16 = [m8[s] + _mm(m8[s], om[s]) for s in subs]
    chain_steps(per_stage)
    o32 = [jnp.where(in32, x, 0.0) for x in l_ab]
    om = [_mm(o32[s], m16[s]) for s in subs]
    m32 = [(m16[s] + _mm(m16[s], om[s])).astype(BF16) for s in subs]
    chain_steps(per_stage)
    o64 = [jnp.where(in64, x, 0.0).astype(BF16) for x in l_ab]
    y1 = [jnp.dot(m32[s], jnp.concatenate([a_st[s], x_loc[s].astype(BF16)], axis=1),
                  preferred_element_type=F32) for s in subs]
    oy = [_mm(o64[s], y1[s]).astype(BF16) for s in subs]
    chain_steps(per_stage)
    wu16 = [(y1[s] + jnp.dot(m32[s], oy[s], preferred_element_type=F32)).astype(BF16)
            for s in subs]
    qy = [jnp.dot(a_rb[s], wu16[s], preferred_element_type=F32) for s in subs]
    chain_steps(per_stage)
    gh = [jnp.dot(bh_t[s], wu16[s], preferred_element_type=F32) for s in subs]
    chain_steps(nsub)
    t_scr[...] = chain["t"]
    for s in subs:
        q_scr[s] = r_st[s] + qy[s][:, :LANES]
        yl_scr[s] = qy[s][:, LANES:] + y_rkv[s]
        g_scr[s] = jnp.where(eye, jnp.exp(tots[s]), 0.0) + gh[s][:, :LANES]
        h_scr[s] = gh[s][:, LANES:] + jnp.dot(kh_t[s], v_st[s], preferred_element_type=F32)


def _rwkv_scan(r, v, kk, lw, bb, kd, nsub):
    B, S, W = r.shape
    n_pair = W // LANES
    T = nsub * CHUNK
    n_step = S // T

    def block(d, c):
        return c + d * (n_step - 1 - 2 * c)

    def shared(g, c):
        d = (g // n_pair) % 2
        return (g // (2 * n_pair), block(d, jnp.minimum(c, n_step - 1)), g % n_pair)

    def per_dir(g, c):
        d = (g // n_pair) % 2
        return (d, g // (2 * n_pair), block(d, jnp.minimum(c, n_step - 1)), g % n_pair)

    def out_map(g, c):
        d = (g // n_pair) % 2
        return (d, g // (2 * n_pair), block(d, jnp.maximum(c - 1, 0)), g % n_pair)

    mat = pltpu.VMEM((nsub, LANES, LANES), F32)
    return pl.pallas_call(
        functools.partial(_scan_kernel, nsub=nsub, n_pair=n_pair),
        grid=(B * 2 * n_pair, n_step + 1),
        in_specs=[pl.BlockSpec((None, T, LANES), shared)] * 3
                 + [pl.BlockSpec((None, None, T, LANES), per_dir)] * 3,
        out_specs=pl.BlockSpec((None, None, T, LANES), out_map),
        out_shape=jax.ShapeDtypeStruct((2, B, S, W), F32),
        scratch_shapes=[pltpu.VMEM((LANES, LANES), F32), mat, mat, mat, mat],
        compiler_params=pltpu.CompilerParams(
            dimension_semantics=("parallel", "arbitrary"), vmem_limit_bytes=VMEM_LIMIT),
        name="rwkv_scan",
    )(r, v, kk, lw, bb, kd)


def _attn_kernel(q_ref, k_ref, v_ref, lq1_ref, lk1_ref, lq2_ref, lk2_ref, sw_ref, o_ref,
                 m_scr, l_scr, acc_scr, *, tk, lambda_init):
    tq = q_ref.shape[0]
    n_kv = k_ref.shape[0] // tk
    q = q_ref[...]
    lane = lax.broadcasted_iota(jnp.int32, q.shape, 1)
    qc = (jnp.where(lane < HEAD, q, jnp.zeros_like(q)), jnp.where(lane < HEAD, jnp.zeros_like(q), q))
    m_scr[...] = jnp.full(m_scr.shape, -jnp.inf, F32)
    l_scr[...] = jnp.zeros(l_scr.shape, F32)
    acc_scr[...] = jnp.zeros(acc_scr.shape, F32)

    def body(j, carry):
        ks = k_ref[pl.ds(pl.multiple_of(j * tk, tk), tk), :]
        vs = v_ref[pl.ds(pl.multiple_of(j * tk, tk), tk), :]
        for c in range(2):
            s = lax.dot_general(qc[c], ks, (((1,), (1,)), ((), ())), preferred_element_type=F32)
            m_old = m_scr[c]
            m_new = jnp.maximum(m_old, jnp.max(s, axis=-1, keepdims=True))
            alpha = jnp.exp(m_old - m_new)
            p = jnp.exp(s - m_new[:, 0:1])
            l_scr[c] = alpha * l_scr[c] + jnp.sum(p, axis=-1, keepdims=True)
            acc_scr[c] = alpha * acc_scr[c] + jnp.dot(p.astype(BF16), vs, preferred_element_type=F32)
            m_scr[c] = m_new
        return carry

    lax.fori_loop(0, n_kv, body, 0)

    lam = (jnp.exp(jnp.sum(lq1_ref[...] * lk1_ref[...], axis=-1, keepdims=True))
           - jnp.exp(jnp.sum(lq2_ref[...] * lk2_ref[...], axis=-1, keepdims=True)) + lambda_init)
    o = acc_scr[0] / l_scr[0] - lam * (acc_scr[1] / l_scr[1])
    o = o * lax.rsqrt(jnp.mean(o * o, axis=-1, keepdims=True) + SUBLN_EPS)
    o_ref[...] = (o * sw_ref[...] * (1.0 - lambda_init)).astype(o_ref.dtype)


def _diff_attn(q, k, v, lq1, lk1, lq2, lk2, subln_w, lambda_init, tq, tk):
    B, S, QW = q.shape
    H = QW // LANES
    const = lambda b, h, i: (0, 0)
    return pl.pallas_call(
        functools.partial(_attn_kernel, tk=tk, lambda_init=lambda_init),
        grid=(B, H, S // tq),
        in_specs=[
            pl.BlockSpec((None, tq, LANES), lambda b, h, i: (b, i, h)),
            pl.BlockSpec((None, S, LANES), lambda b, h, i: (b, 0, h)),
            pl.BlockSpec((None, S, LANES), lambda b, h, i: (b, 0, h)),
            pl.BlockSpec((1, HEAD), const),
            pl.BlockSpec((1, HEAD), const),
            pl.BlockSpec((1, HEAD), const),
            pl.BlockSpec((1, HEAD), const),
            pl.BlockSpec((1, LANES), const),
        ],
        out_specs=pl.BlockSpec((None, tq, LANES), lambda b, h, i: (b, i, h)),
        out_shape=jax.ShapeDtypeStruct((B, S, QW), BF16),
        scratch_shapes=[pltpu.VMEM((2, tq, LANES), F32),
                        pltpu.VMEM((2, tq, LANES), F32),
                        pltpu.VMEM((2, tq, LANES), F32)],
        compiler_params=pltpu.CompilerParams(
            dimension_semantics=("parallel", "parallel", "arbitrary"),
            vmem_limit_bytes=VMEM_LIMIT),
        name="diff_attn",
    )(q, k, v, lq1, lk1, lq2, lk2, subln_w)


def _merge_kernel(x_ref, y_ref, bonus_ref, g_ref, ob_ref, nm_ref, wg_ref, lnw_ref, lnb_ref,
                  e_ref, woa_ref, wob_ref, wout_ref, o_ref):
    x = x_ref[...]
    D = x.shape[-1]
    h = _rms(x, nm_ref[...], RMS_EPS).astype(BF16)
    gates = _sigmoid(jnp.dot(h, wg_ref[...], preferred_element_type=F32))
    e = e_ref[...]
    y = y_ref[0] + y_ref[1]
    mean = _mm_exact_rhs(y, e) * (1.0 / HEAD)
    yc = y - mean
    var = _mm_exact_rhs(yc * yc, e) * (1.0 / HEAD)
    yn = yc * lax.rsqrt(var + GN_EPS) * lnw_ref[...] + lnb_ref[...]
    ya = _mm((yn + bonus_ref[...]) * g_ref[...], woa_ref[...])
    yb = jnp.dot(ob_ref[...], wob_ref[...], preferred_element_type=F32)
    merged = gates[:, :D] * ya + gates[:, D:] * yb
    o_ref[...] = x + _mm(merged, wout_ref[...])


def _merge(x, y, bonus, g, ob, norm_mix, w_g, ln_w, ln_b, e_seg, w_oa, w_ob, w_out, tm):
    B, S, D = x.shape
    W = bonus.shape[-1]
    const = lambda b, i: (0, 0)
    row = lambda b, i: (b, i, 0)
    return pl.pallas_call(
        _merge_kernel,
        grid=(B, S // tm),
        in_specs=[
            pl.BlockSpec((None, tm, D), row),
            pl.BlockSpec((2, None, tm, W), lambda b, i: (0, b, i, 0)),
            pl.BlockSpec((None, tm, W), row),
            pl.BlockSpec((None, tm, W), row),
            pl.BlockSpec((None, tm, W), row),
            pl.BlockSpec((1, D), const),
            pl.BlockSpec((D, 2 * D), const),
            pl.BlockSpec((1, W), const),
            pl.BlockSpec((1, W), const),
            pl.BlockSpec((W, W), const),
            pl.BlockSpec((W, D), const),
            pl.BlockSpec((W, D), const),
            pl.BlockSpec((D, D), const),
        ],
        out_specs=pl.BlockSpec((None, tm, D), row),
        out_shape=jax.ShapeDtypeStruct((B, S, D), F32),
        compiler_params=pltpu.CompilerParams(
            dimension_semantics=("parallel", "parallel"), vmem_limit_bytes=VMEM_LIMIT),
        name="merge",
    )(x, y, bonus, g, ob, norm_mix, w_g, ln_w, ln_b, e_seg, w_oa, w_ob, w_out)


def _ffn_kernel(x_ref, p_ref, nf_ref, w1_ref, w2_ref, np_ref, wpg_ref, wpp_ref, nfin_ref, o_ref,
                *, n_chunk):
    x = x_ref[...]
    h = _rms(x, nf_ref[...], RMS_EPS).astype(BF16)
    ff = w1_ref.shape[1] // n_chunk
    acc = x
    for c in range(n_chunk):
        t = jnp.maximum(jnp.dot(h, w1_ref[:, c * ff:(c + 1) * ff], preferred_element_type=F32), 0.0)
        acc = acc + jnp.dot((t * t).astype(BF16), w2_ref[c * ff:(c + 1) * ff, :],
                            preferred_element_type=F32)
    x = acc
    h = _rms(x, np_ref[...], RMS_EPS).astype(BF16)
    gate = _sigmoid(jnp.dot(h, wpg_ref[...], preferred_element_type=F32))
    x = x + gate * _mm(p_ref[...], wpp_ref[...])
    o_ref[...] = x
    if nfin_ref is not None:
        o_ref[...] = _rms(x, nfin_ref[...], RMS_EPS)


def _ffn_kernel_plain(x_ref, p_ref, nf_ref, w1_ref, w2_ref, np_ref, wpg_ref, wpp_ref, o_ref, *, n_chunk):
    _ffn_kernel(x_ref, p_ref, nf_ref, w1_ref, w2_ref, np_ref, wpg_ref, wpp_ref, None, o_ref,
                n_chunk=n_chunk)


def _ffn(x, p, norm_ffn, w1, w2, norm_ple, w_pg, w_pp, norm_final, tm, n_chunk):
    B, S, D = x.shape
    PD = p.shape[-1]
    FF = w1.shape[1]
    const = lambda b, i: (0, 0)
    row = lambda b, i: (b, i, 0)
    in_specs = [
        pl.BlockSpec((None, tm, D), row),
        pl.BlockSpec((None, tm, PD), row),
        pl.BlockSpec((1, D), const),
        pl.BlockSpec((D, FF), const, pipeline_mode=pl.Buffered(1)),
        pl.BlockSpec((FF, D), const, pipeline_mode=pl.Buffered(1)),
        pl.BlockSpec((1, D), const),
        pl.BlockSpec((D, D), const, pipeline_mode=pl.Buffered(1)),
        pl.BlockSpec((PD, D), const, pipeline_mode=pl.Buffered(1)),
    ]
    args = [x, p, norm_ffn, w1, w2, norm_ple, w_pg, w_pp]
    if norm_final is not None:
        in_specs.append(pl.BlockSpec((1, D), const))
        args.append(norm_final)
        body = functools.partial(_ffn_kernel, n_chunk=n_chunk)
    else:
        body = functools.partial(_ffn_kernel_plain, n_chunk=n_chunk)
    return pl.pallas_call(
        body,
        grid=(B, S // tm),
        in_specs=in_specs,
        out_specs=pl.BlockSpec((None, tm, D), row),
        out_shape=jax.ShapeDtypeStruct((B, S, D), F32),
        compiler_params=pltpu.CompilerParams(
            dimension_semantics=("parallel", "parallel"), vmem_limit_bytes=VMEM_LIMIT),
        name="ffn",
    )(*args)


def _block_diag2(w):
    z = jnp.zeros_like(w[0])
    return jnp.concatenate([jnp.concatenate([w[0], z], axis=1),
                            jnp.concatenate([z, w[1]], axis=1)], axis=0)


def kernel(x, p, norm_mix, w_in, shift_mu_prev, shift_mu_next, rwkv_w0, rwkv_w2, rwkv_a0,
           rwkv_a2, rwkv_g2, rwkv_k_k, rwkv_k_a, rwkv_r_k, rwkv_ln_w, rwkv_ln_b, rwkv_w_o,
           da_lq1, da_lk1, da_lq2, da_lk2, da_subln_w, da_w_o, w_out, norm_ffn, w_ff1, w_ff2,
           norm_ple, w_ple_gate, w_ple_proj, norm_final):
    B, S, D = x.shape
    L = w_in.shape[0]
    W = rwkv_w0.shape[-1]
    n_r = shift_mu_prev.shape[-1]
    n_qk = 2 * da_w_o.shape[1]
    n_v = da_w_o.shape[1]
    cos_t, sin_t = _rope_tables(S)
    seg = jnp.arange(W) // HEAD
    e_seg = (seg[:, None] == seg[None, :]).astype(BF16)
    tm = min(256, S)
    for i in range(L):
        lambda_init = 0.8 - 0.6 * math.exp(-0.3 * i)
        w_i = w_in[i].astype(BF16)
        u_r, q, k, v = _in_proj(
            x, norm_mix[i][None], w_i[:, :n_r], w_i[:, n_r:n_r + n_qk],
            w_i[:, n_r + n_qk:n_r + n_qk + n_v], cos_t, sin_t, tm)
        r, vv, kk, g, bonus, lw, bb, kd = _rwkv_prep(
            u_r, shift_mu_prev[i][None], shift_mu_next[i][None],
            rwkv_w0[i].reshape(1, 2 * W), _block_diag2(rwkv_w2[i]).astype(BF16),
            rwkv_a0[i].reshape(1, 2 * W), _block_diag2(rwkv_a2[i]).astype(BF16),
            rwkv_g2[i].astype(BF16), rwkv_k_k[i][None], rwkv_k_a[i][None],
            rwkv_r_k[i].reshape(1, W), e_seg, tm)
        y = _rwkv_scan(r, vv, kk, lw, bb, kd, nsub=min(8, S // CHUNK))
        ob = _diff_attn(q, k, v, da_lq1[i][None], da_lk1[i][None], da_lq2[i][None],
                        da_lk2[i][None], da_subln_w[i][None], lambda_init,
                        tq=min(512, S), tk=min(512, S))
        x = _merge(x, y, bonus, g, ob, norm_mix[i][None],
                   w_i[:, n_r + n_qk + n_v:], rwkv_ln_w[i][None], rwkv_ln_b[i][None], e_seg,
                   rwkv_w_o[i].astype(BF16), da_w_o[i].astype(BF16), w_out[i].astype(BF16), tm)
        x = _ffn(x, p[i], norm_ffn[i][None], w_ff1[i].astype(BF16), w_ff2[i].astype(BF16),
                 norm_ple[i][None], w_ple_gate[i].astype(BF16), w_ple_proj[i].astype(BF16),
                 norm_final[None] if i == L - 1 else None, tm, n_chunk=4)
    return x
```

```python
import functools
import math

import jax
import jax.numpy as jnp
from jax import lax
from jax.experimental import pallas as pl
from jax.experimental.pallas import tpu as pltpu

F32 = jnp.float32
BF16 = jnp.bfloat16

HEAD = 64
LANES = 128
CHUNK = 64
RMS_EPS = 1e-6
GN_EPS = 64e-5
SUBLN_EPS = 1e-5
ROPE_THETA = 500000.0
ROPE_DIM = 16
Q_SCALE = HEAD ** -0.5 * math.log2(math.e)
DECAY_SCALE = math.exp(-0.5)
VMEM_LIMIT = 56 * 1024 * 1024


def _rms(x, g, eps):
    return x * lax.rsqrt(jnp.mean(x * x, axis=-1, keepdims=True) + eps) * g


def _sigmoid(x):
    return 1.0 / (1.0 + jnp.exp(-x))


def _mm(a, b):
    return jnp.dot(a.astype(BF16), b.astype(BF16), preferred_element_type=F32)


def _mm_nt(a, b):
    return lax.dot_general(a.astype(BF16), b.astype(BF16), (((1,), (1,)), ((), ())),
                           preferred_element_type=F32)


def _split2(x):
    hi = x.astype(BF16)
    lo = (x - hi.astype(F32)).astype(BF16)
    return hi, lo


def _mm_exact_rhs(a, b_bf16):
    hi, lo = _split2(a)
    return (jnp.dot(hi, b_bf16, preferred_element_type=F32)
            + jnp.dot(lo, b_bf16, preferred_element_type=F32))


def _mm3(a, b):
    ah, al = _split2(a)
    bh, bl = _split2(b)
    return (jnp.dot(ah, bh, preferred_element_type=F32)
            + jnp.dot(ah, bl, preferred_element_type=F32)
            + jnp.dot(al, bh, preferred_element_type=F32))


def _inproj_kernel(x_ref, g_ref, wr_ref, wqk_ref, wv_ref, cos_ref, sin_ref,
                   ur_ref, q_ref, k_ref, v_ref):
    h = _rms(x_ref[...], g_ref[...], RMS_EPS).astype(BF16)
    ur_ref[...] = jnp.dot(h, wr_ref[...], preferred_element_type=F32)
    v_ref[...] = lax.dot_general(wv_ref[...], h, (((1,), (1,)), ((), ())),
                                 preferred_element_type=F32).astype(BF16)
    cos_t = cos_ref[...]
    sin_t = sin_ref[...]
    lane = lax.broadcasted_iota(jnp.int32, cos_t.shape, 1)
    first_half = (lane & (HEAD - 1)) < (ROPE_DIM // 2)
    n_tiles = wqk_ref.shape[1] // LANES
    for c in range(n_tiles):
        xq = jnp.dot(h, wqk_ref[:, c * LANES:(c + 1) * LANES], preferred_element_type=F32)
        partner = jnp.where(first_half,
                            pltpu.roll(xq, LANES - ROPE_DIM // 2, 1),
                            pltpu.roll(xq, ROPE_DIM // 2, 1))
        ro = xq * cos_t + partner * sin_t
        if c < n_tiles // 2:
            q_ref[:, c * LANES:(c + 1) * LANES] = (ro * Q_SCALE).astype(BF16)
        else:
            cc = c - n_tiles // 2
            k_ref[:, cc * LANES:(cc + 1) * LANES] = ro.astype(BF16)


def _rope_tables(S):
    pos = jnp.arange(S, dtype=F32)
    inv_freq = ROPE_THETA ** (-jnp.arange(0, ROPE_DIM, 2, dtype=F32) / ROPE_DIM)
    ang = pos[:, None] * inv_freq[None, :]
    cos8, sin8 = jnp.cos(ang), jnp.sin(ang)
    ones = jnp.ones((S, HEAD - ROPE_DIM), F32)
    zeros = jnp.zeros((S, HEAD - ROPE_DIM), F32)
    cos64 = jnp.concatenate([cos8, cos8, ones], axis=1)
    sin64 = jnp.concatenate([-sin8, sin8, zeros], axis=1)
    return jnp.tile(cos64, (1, 2)), jnp.tile(sin64, (1, 2))


def _in_proj(x, g, w_r, w_qk, w_vt, cos_t, sin_t, tm):
    B, S, D = x.shape
    nr, nqk, nv = w_r.shape[1], w_qk.shape[1], w_vt.shape[0]
    const = lambda b, i: (0, 0)
    row = lambda b, i: (b, i, 0)
    return pl.pallas_call(
        _inproj_kernel,
        grid=(B, S // tm),
        in_specs=[
            pl.BlockSpec((None, tm, D), row),
            pl.BlockSpec((1, D), const),
            pl.BlockSpec((D, nr), const),
            pl.BlockSpec((D, nqk), const),
            pl.BlockSpec((nv, D), const),
            pl.BlockSpec((tm, LANES), lambda b, i: (i, 0)),
            pl.BlockSpec((tm, LANES), lambda b, i: (i, 0)),
        ],
        out_specs=[
            pl.BlockSpec((None, tm, nr), row),
            pl.BlockSpec((None, tm, nqk // 2), row),
            pl.BlockSpec((None, tm, nqk // 2), row),
            pl.BlockSpec((None, nv, tm), lambda b, i: (b, 0, i)),
        ],
        out_shape=[
            jax.ShapeDtypeStruct((B, S, nr), F32),
            jax.ShapeDtypeStruct((B, S, nqk // 2), BF16),
            jax.ShapeDtypeStruct((B, S, nqk // 2), BF16),
            jax.ShapeDtypeStruct((B, nv, S), BF16),
        ],
        compiler_params=pltpu.CompilerParams(
            dimension_semantics=("parallel", "parallel"), vmem_limit_bytes=VMEM_LIMIT),
        name="in_proj",
    )(x, g, w_r, w_qk, w_vt, cos_t, sin_t)


def _prep_kernel(u_ref, up_ref, un_ref, mup_ref, mun_ref, w0_ref, w2_ref, a0_ref, a2_ref,
                 g2_ref, kk_w_ref, ka_ref, rk_ref, e_ref,
                 r_ref, v_ref, kk_ref, g_ref, bonus_ref, lw_ref, b_ref, kd_ref):
    i = pl.program_id(1)
    n = pl.num_programs(1)
    u = u_ref[...]
    tm = u.shape[0]
    W = r_ref.shape[-1]
    row = lax.broadcasted_iota(jnp.int32, (tm, 1), 0)
    prev_row = jnp.where(i > 0, up_ref[7:8, :], 0.0)
    next_row = jnp.where(i < n - 1, un_ref[0:1, :], 0.0)
    u_prev = jnp.where(row == 0, prev_row, pltpu.roll(u, 1, 0))
    u_next = jnp.where(row == tm - 1, next_row, pltpu.roll(u, tm - 1, 0))
    us = u + mup_ref[...] * (u_prev - u) + mun_ref[...] * (u_next - u)

    r = us[:, 0:W]
    k = us[:, W:2 * W]
    v = us[:, 2 * W:3 * W]
    wd = us[:, 3 * W:3 * W + LANES]
    ad = us[:, 3 * W + LANES:3 * W + 2 * LANES]
    gd = us[:, 3 * W + 2 * LANES:3 * W + 3 * LANES]

    z = w0_ref[...] + _mm(jnp.tanh(wd), w2_ref[...])
    lw = -(_sigmoid(z) * DECAY_SCALE)
    a = _sigmoid(a0_ref[...] + _mm(ad, a2_ref[...]))
    g_ref[...] = _mm(_sigmoid(gd), g2_ref[...])

    e = e_ref[...]
    kraw = k * kk_w_ref[...]
    ss = _mm_exact_rhs(kraw * kraw, e)
    kk = kraw / jnp.maximum(jnp.sqrt(ss), 1e-12)
    ka = ka_ref[...]
    kd_sum = jnp.zeros_like(k)
    for d in range(2):
        a_d = a[:, d * W:(d + 1) * W]
        kd = k * (1.0 + (a_d - 1.0) * ka)
        kd_sum = kd_sum + kd
        lw_ref[d] = lw[:, d * W:(d + 1) * W]
        b_ref[d] = kk * a_d
        kd_ref[d] = kd
    c = _mm_exact_rhs(r * rk_ref[...] * kd_sum, e)
    r_ref[...] = r
    v_ref[...] = v
    kk_ref[...] = kk
    bonus_ref[...] = c * v


def _rwkv_prep(u_r, mu_prev, mu_next, w0, w2bd, a0, a2bd, g2, k_k, k_a, r_k, e_seg, tm):
    B, S, C = u_r.shape
    W = k_k.shape[-1]
    nblk8 = S // 8
    const = lambda b, i: (0, 0)
    row = lambda b, i: (b, i, 0)
    drow = lambda b, i: (0, b, i, 0)
    tok = jax.ShapeDtypeStruct((B, S, W), F32)
    dtok = jax.ShapeDtypeStruct((2, B, S, W), F32)
    return pl.pallas_call(
        _prep_kernel,
        grid=(B, S // tm),
        in_specs=[
            pl.BlockSpec((None, tm, C), row),
            pl.BlockSpec((None, 8, C), lambda b, i: (b, jnp.maximum(i * (tm // 8) - 1, 0), 0)),
            pl.BlockSpec((None, 8, C), lambda b, i: (b, jnp.minimum((i + 1) * (tm // 8), nblk8 - 1), 0)),
            pl.BlockSpec((1, C), const),
            pl.BlockSpec((1, C), const),
            pl.BlockSpec((1, 2 * W), const),
            pl.BlockSpec((LANES, 2 * W), const),
            pl.BlockSpec((1, 2 * W), const),
            pl.BlockSpec((LANES, 2 * W), const),
            pl.BlockSpec((LANES, W), const),
            pl.BlockSpec((1, W), const),
            pl.BlockSpec((1, W), const),
            pl.BlockSpec((1, W), const),
            pl.BlockSpec((W, W), const),
        ],
        out_specs=[pl.BlockSpec((None, tm, W), row)] * 5
                  + [pl.BlockSpec((2, None, tm, W), drow)] * 3,
        out_shape=[tok] * 5 + [dtok] * 3,
        compiler_params=pltpu.CompilerParams(
            dimension_semantics=("parallel", "parallel"), vmem_limit_bytes=VMEM_LIMIT),
        name="rwkv_prep",
    )(u_r, u_r, u_r, mu_prev, mu_next, w0, w2bd, a0, a2bd, g2, k_k, k_a, r_k, e_seg)


def _scan_kernel(r_ref, v_ref, kk_ref, lw_ref, b_ref, kd_ref, y_ref,
                 t_scr, g_scr, h_scr, q_scr, yl_scr, *, nsub, n_pair):
    gidx = pl.program_id(0)
    step = pl.program_id(1)
    rev = (gidx // n_pair) % 2
    sgn = 1 - 2 * rev
    C = CHUNK
    P2 = 2 * C

    @pl.when(step == 0)
    def _():
        t_scr[...] = jnp.zeros_like(t_scr)
        g_scr[...] = jnp.zeros_like(g_scr)
        h_scr[...] = jnp.zeros_like(h_scr)
        q_scr[...] = jnp.zeros_like(q_scr)
        yl_scr[...] = jnp.zeros_like(yl_scr)

    chunk_slices = []
    for s in range(nsub):
        ci = s + rev * (nsub - 1 - 2 * s)
        chunk_slices.append(pl.ds(pl.multiple_of(ci * C, C), C))

    chain = {"t": t_scr[...], "next": 0}

    def chain_steps(n):
        for _ in range(n):
            s = chain["next"]
            if s == nsub:
                return
            t_cur = chain["t"]
            y_st = _mm(q_scr[s], t_cur) + yl_scr[s]
            chain["t"] = _mm3(g_scr[s], t_cur) + h_scr[s]
            y_ref[chunk_slices[s], :] = y_st[:C, :] + y_st[C:, :]
            chain["next"] = s + 1

    row = lax.broadcasted_iota(jnp.int32, (P2, P2), 0)
    col = lax.broadcasted_iota(jnp.int32, (P2, P2), 1)
    rt = row & (C - 1)
    ct = col & (C - 1)
    delta = (rt - ct) * sgn
    strict = delta > 0
    incl = delta >= 0
    eye = row == col
    ident = jnp.where(eye, 1.0, 0.0)
    blk8 = (rt >> 3) == (ct >> 3)
    blk16 = (rt >> 4) == (ct >> 4)
    blk32 = (rt >> 5) == (ct >> 5)
    in8 = strict & blk8
    in16 = strict & blk16 & ~blk8
    in32 = strict & blk32 & ~blk16
    in64 = strict & ~blk32
    rc = lax.broadcasted_iota(jnp.int32, (C, C), 0)
    cc = lax.broadcasted_iota(jnp.int32, (C, C), 1)
    lcum = jnp.where((rc - cc) * sgn >= 0, 1.0, 0.0).astype(BF16)
    lane = lax.broadcasted_iota(jnp.int32, (C, LANES), 1)
    head0 = lane < HEAD

    def stack(x):
        return jnp.concatenate([jnp.where(head0, x, 0.0), jnp.where(head0, 0.0, x)], axis=0)

    subs = range(nsub)
    sls = chunk_slices

    lws = [lw_ref[sl, :] for sl in sls]
    cums = []
    for lw in lws:
        l1 = lw.astype(BF16)
        rem = lw - l1.astype(F32)
        l2 = rem.astype(BF16)
        l3 = (rem - l2.astype(F32)).astype(BF16)
        cums.append(jnp.dot(lcum, l1, preferred_element_type=F32)
                    + jnp.dot(lcum, l2, preferred_element_type=F32)
                    + jnp.dot(lcum, l3, preferred_element_type=F32))
    tots = [jnp.sum(lw, axis=0, keepdims=True) for lw in lws]
    a_st, r_st, b_st, k_st, bh_t, kh_t, v_st = [], [], [], [], [], [], []
    for s in subs:
        sl, lw, cum, tot = sls[s], lws[s], cums[s], tots[s]
        e_neg = jnp.exp(-cum)
        e_rem = jnp.exp(tot - cum)
        b = b_ref[sl, :]
        kd = kd_ref[sl, :]
        a_st.append(stack(kk_ref[sl, :] * -jnp.exp(cum - lw)).astype(BF16))
        r_st.append(stack(r_ref[sl, :] * jnp.exp(cum)))
        b_st.append(stack(b * e_neg).astype(BF16))
        k_st.append(stack(kd * e_neg).astype(BF16))
        bh_t.append(stack(b * e_rem).T.astype(BF16))
        kh_t.append(stack(kd * e_rem).T.astype(BF16))
        v_st.append(stack(v_ref[sl, :]).astype(BF16))

    nt = (((1,), (1,)), ((), ()))
    sc = [lax.dot_general(jnp.concatenate([a_st[s], r_st[s].astype(BF16)], axis=0),
                          jnp.concatenate([b_st[s], k_st[s]], axis=0), nt,
                          preferred_element_type=F32) for s in subs]
    per_stage = -(-nsub // 8)
    chain_steps(per_stage)
    l_ab = [x[:P2, :P2] for x in sc]
    l_ak = [jnp.where(strict, x[:P2, P2:], 0.0).astype(BF16) for x in sc]
    a_rb = [jnp.where(incl, x[P2:, :P2], 0.0).astype(BF16) for x in sc]
    a_rk = [jnp.where(incl, x[P2:, P2:], 0.0).astype(BF16) for x in sc]
    x_loc = [jnp.dot(l_ak[s], v_st[s], preferred_element_type=F32) for s in subs]
    y_rkv = [jnp.dot(a_rk[s], v_st[s], preferred_element_type=F32) for s in subs]
    chain_steps(per_stage)

    d8 = [jnp.where(in8, x, 0.0) for x in l_ab]
    m1 = [ident + x for x in d8]
    p1 = [_mm(x, x) for x in d8]
    chain_steps(per_stage)
    pm = [_mm(p1[s], jnp.concatenate([p1[s], m1[s]], axis=1)) for s in subs]
    m2 = [m1[s] + pm[s][:, P2:] for s in subs]
    m8 = [m2[s] + _mm(pm[s][:, :P2], m2[s]) for s in subs]
    chain_steps(per_stage)
    o16 = [jnp.where(in16, x, 0.0) for x in l_ab]
    om = [_mm(o16[s], m8[s]) for s in subs]
    m16 = [m8[s] + _mm(m8[s], om[s]) for s in subs]
    chain_steps(per_stage)
    o32 = [jnp.where(in32, x, 0.0) for x in l_ab]
    om = [_mm(o32[s], m16[s]) for s in subs]
    m32 = [(m16[s] + _mm(m16[s], om[s])).astype(BF16) for s in subs]
    chain_steps(per_stage)
    o64 = [jnp.where(in64, x, 0.0).astype(BF16) for x in l_ab]
    y1 = [jnp.dot(m32[s], jnp.concatenate([a_st[s], x_loc[s].astype(BF16)], axis=1),
                  preferred_element_type=F32) for s in subs]
    oy = [_mm(o64[s], y1[s]).astype(BF16) for s in subs]
    chain_steps(per_stage)
    wu16 = [(y1[s] + jnp.dot(m32[s], oy[s], preferred_element_type=F32)).astype(BF16)
            for s in subs]
    qy = [jnp.dot(a_rb[s], wu16[s], preferred_element_type=F32) for s in subs]
    chain_steps(per_stage)
    gh = [jnp.dot(bh_t[s], wu16[s], preferred_element_type=F32) for s in subs]
    chain_steps(nsub)
    t_scr[...] = chain["t"]
    for s in subs:
        q_scr[s] = r_st[s] + qy[s][:, :LANES]
        yl_scr[s] = qy[s][:, LANES:] + y_rkv[s]
        g_scr[s] = jnp.where(eye, jnp.exp(tots[s]), 0.0) + gh[s][:, :LANES]
        h_scr[s] = gh[s][:, LANES:] + jnp.dot(kh_t[s], v_st[s], preferred_element_type=F32)


def _rwkv_scan(r, v, kk, lw, bb, kd, nsub):
    B, S, W = r.shape
    n_pair = W // LANES
    T = nsub * CHUNK
    n_step = S // T

    def block(d, c):
        return c + d * (n_step - 1 - 2 * c)

    def shared(g, c):
        d = (g // n_pair) % 2
        return (g // (2 * n_pair), block(d, jnp.minimum(c, n_step - 1)), g % n_pair)

    def per_dir(g, c):
        d = (g // n_pair) % 2
        return (d, g // (2 * n_pair), block(d, jnp.minimum(c, n_step - 1)), g % n_pair)

    def out_map(g, c):
        d = (g // n_pair) % 2
        return (d, g // (2 * n_pair), block(d, jnp.maximum(c - 1, 0)), g % n_pair)

    mat = pltpu.VMEM((nsub, LANES, LANES), F32)
    return pl.pallas_call(
        functools.partial(_scan_kernel, nsub=nsub, n_pair=n_pair),
        grid=(B * 2 * n_pair, n_step + 1),
        in_specs=[pl.BlockSpec((None, T, LANES), shared)] * 3
                 + [pl.BlockSpec((None, None, T, LANES), per_dir)] * 3,
        out_specs=pl.BlockSpec((None, None, T, LANES), out_map),
        out_shape=jax.ShapeDtypeStruct((2, B, S, W), F32),
        scratch_shapes=[pltpu.VMEM((LANES, LANES), F32), mat, mat, mat, mat],
        compiler_params=pltpu.CompilerParams(
            dimension_semantics=("parallel", "arbitrary"), vmem_limit_bytes=VMEM_LIMIT),
        name="rwkv_scan",
    )(r, v, kk, lw, bb, kd)


def _attn_kernel(q_ref, k_ref, vt_ref, lq1_ref, lk1_ref, lq2_ref, lk2_ref, sw_ref, o_ref,
                 m_scr, l_scr, alpha_scr, acc_scr, s_even, s_odd, p_even, p_odd,
                 *, tk, lambda_init):
    n_kv = k_ref.shape[0] // tk
    q = q_ref[...]
    lane = lax.broadcasted_iota(jnp.int32, q.shape, 1)
    zero = jnp.zeros_like(q)
    qc = (jnp.where(lane < HEAD, q, zero), jnp.where(lane < HEAD, zero, q))
    m_scr[...] = jnp.full(m_scr.shape, -jnp.inf, F32)
    l_scr[...] = jnp.zeros(l_scr.shape, F32)
    acc_scr[...] = jnp.zeros(acc_scr.shape, F32)
    alpha_scr[...] = jnp.ones(alpha_scr.shape, F32)
    p_odd[...] = jnp.zeros(p_odd.shape, BF16)

    def scores(j, c):
        ks = k_ref[pl.ds(pl.multiple_of(j * tk, tk), tk), :]
        return lax.dot_general(ks, qc[c], (((1,), (1,)), ((), ())),
                               preferred_element_type=F32)

    def weighted_values(j, p_ref, c):
        vts = vt_ref[:, pl.ds(pl.multiple_of(j * tk, tk), tk)]
        acc_scr[c] = alpha_scr[c] * acc_scr[c] + jnp.dot(vts, p_ref[c],
                                                         preferred_element_type=F32)

    def tile_step(j, s_cur, s_nxt, p_cur, p_prev):
        j_prev = jnp.maximum(j - 1, 0)
        j_next = jnp.minimum(j + 1, n_kv - 1)
        for c in range(2):
            weighted_values(j_prev, p_prev, c)
            s_nxt[c] = scores(j_next, c)
            s = s_cur[c]
            m_old = m_scr[c]
            m_new = jnp.maximum(m_old, jnp.max(s, axis=0, keepdims=True))
            alpha = jnp.exp2(m_old - m_new)
            p = jnp.exp2(s - m_new)
            l_scr[c] = alpha * l_scr[c] + jnp.sum(p, axis=0, keepdims=True)
            p_cur[c] = p.astype(BF16)
            alpha_scr[c] = alpha
            m_scr[c] = m_new

    for c in range(2):
        s_even[c] = scores(0, c)

    def body(i, carry):
        tile_step(2 * i, s_even, s_odd, p_even, p_odd)
        tile_step(2 * i + 1, s_odd, s_even, p_odd, p_even)
        return carry

    lax.fori_loop(0, n_kv // 2, body, 0)
    for c in range(2):
        weighted_values(n_kv - 1, p_odd, c)

    lam = (jnp.exp(jnp.sum(lq1_ref[...] * lk1_ref[...], axis=-1, keepdims=True))
           - jnp.exp(jnp.sum(lq2_ref[...] * lk2_ref[...], axis=-1, keepdims=True)) + lambda_init)
    o = acc_scr[0] / l_scr[0] - lam * (acc_scr[1] / l_scr[1])
    o = o * lax.rsqrt(jnp.mean(o * o, axis=0, keepdims=True) + SUBLN_EPS)
    o = o * (sw_ref[...] * (1.0 - lambda_init))
    o_ref[...] = o.T.astype(o_ref.dtype)


def _diff_attn(q, k, vt, lq1, lk1, lq2, lk2, subln_w, lambda_init, tq, tk):
    B, S, QW = q.shape
    H = QW // LANES
    DV = vt.shape[1] // H
    const = lambda b, h, i: (0, 0)
    return pl.pallas_call(
        functools.partial(_attn_kernel, tk=tk, lambda_init=lambda_init),
        grid=(B, H, S // tq),
        in_specs=[
            pl.BlockSpec((None, tq, LANES), lambda b, h, i: (b, i, h)),
            pl.BlockSpec((None, S, LANES), lambda b, h, i: (b, 0, h)),
            pl.BlockSpec((None, DV, S), lambda b, h, i: (b, h, 0)),
            pl.BlockSpec((1, HEAD), const),
            pl.BlockSpec((1, HEAD), const),
            pl.BlockSpec((1, HEAD), const),
            pl.BlockSpec((1, HEAD), const),
            pl.BlockSpec((DV, 1), const),
        ],
        out_specs=pl.BlockSpec((None, tq, DV), lambda b, h, i: (b, i, h)),
        out_shape=jax.ShapeDtypeStruct((B, S, H * DV), BF16),
        scratch_shapes=[pltpu.VMEM((2, 1, tq), F32),
                        pltpu.VMEM((2, 1, tq), F32),
                        pltpu.VMEM((2, 1, tq), F32),
                        pltpu.VMEM((2, DV, tq), F32),
                        pltpu.VMEM((2, tk, tq), F32),
                        pltpu.VMEM((2, tk, tq), F32),
                        pltpu.VMEM((2, tk, tq), BF16),
                        pltpu.VMEM((2, tk, tq), BF16)],
        compiler_params=pltpu.CompilerParams(
            dimension_semantics=("parallel", "parallel", "arbitrary"),
            vmem_limit_bytes=VMEM_LIMIT),
        name="diff_attn",
    )(q, k, vt, lq1, lk1, lq2, lk2, subln_w)


def _merge_kernel(x_ref, y_ref, bonus_ref, g_ref, ob_ref, nm_ref, wg_ref, lnw_ref, lnb_ref,
                  e_ref, woa_ref, wob_ref, wout_ref, o_ref):
    x = x_ref[...]
    D = x.shape[-1]
    h = _rms(x, nm_ref[...], RMS_EPS).astype(BF16)
    gates = _sigmoid(jnp.dot(h, wg_ref[...], preferred_element_type=F32))
    e = e_ref[...]
    y = y_ref[0] + y_ref[1]
    mean = _mm_exact_rhs(y, e) * (1.0 / HEAD)
    yc = y - mean
    var = _mm_exact_rhs(yc * yc, e) * (1.0 / HEAD)
    yn = yc * lax.rsqrt(var + GN_EPS) * lnw_ref[...] + lnb_ref[...]
    ya = _mm((yn + bonus_ref[...]) * g_ref[...], woa_ref[...])
    yb = jnp.dot(ob_ref[...], wob_ref[...], preferred_element_type=F32)
    merged = gates[:, :D] * ya + gates[:, D:] * yb
    o_ref[...] = x + _mm(merged, wout_ref[...])


def _merge(x, y, bonus, g, ob, norm_mix, w_g, ln_w, ln_b, e_seg, w_oa, w_ob, w_out, tm):
    B, S, D = x.shape
    W = bonus.shape[-1]
    const = lambda b, i: (0, 0)
    row = lambda b, i: (b, i, 0)
    return pl.pallas_call(
        _merge_kernel,
        grid=(B, S // tm),
        in_specs=[
            pl.BlockSpec((None, tm, D), row),
            pl.BlockSpec((2, None, tm, W), lambda b, i: (0, b, i, 0)),
            pl.BlockSpec((None, tm, W), row),
            pl.BlockSpec((None, tm, W), row),
            pl.BlockSpec((None, tm, W), row),
            pl.BlockSpec((1, D), const),
            pl.BlockSpec((D, 2 * D), const),
            pl.BlockSpec((1, W), const),
            pl.BlockSpec((1, W), const),
            pl.BlockSpec((W, W), const),
            pl.BlockSpec((W, D), const),
            pl.BlockSpec((W, D), const),
            pl.BlockSpec((D, D), const),
        ],
        out_specs=pl.BlockSpec((None, tm, D), row),
        out_shape=jax.ShapeDtypeStruct((B, S, D), F32),
        compiler_params=pltpu.CompilerParams(
            dimension_semantics=("parallel", "parallel"), vmem_limit_bytes=VMEM_LIMIT),
        name="merge",
    )(x, y, bonus, g, ob, norm_mix, w_g, ln_w, ln_b, e_seg, w_oa, w_ob, w_out)


def _ffn_kernel(x_ref, p_ref, nf_ref, w1_ref, w2_ref, np_ref, wpg_ref, wpp_ref, nfin_ref, o_ref,
                *, n_chunk):
    x = x_ref[...]
    h = _rms(x, nf_ref[...], RMS_EPS).astype(BF16)
    ff = w1_ref.shape[1] // n_chunk
    acc = x
    for c in range(n_chunk):
        t = jnp.maximum(jnp.dot(h, w1_ref[:, c * ff:(c + 1) * ff], preferred_element_type=F32), 0.0)
        acc = acc + jnp.dot((t * t).astype(BF16), w2_ref[c * ff:(c + 1) * ff, :],
                            preferred_element_type=F32)
    x = acc
    h = _rms(x, np_ref[...], RMS_EPS).astype(BF16)
    gate = _sigmoid(jnp.dot(h, wpg_ref[...], preferred_element_type=F32))
    x = x + gate * _mm(p_ref[...], wpp_ref[...])
    o_ref[...] = x
    if nfin_ref is not None:
        o_ref[...] = _rms(x, nfin_ref[...], RMS_EPS)


def _ffn_kernel_plain(x_ref, p_ref, nf_ref, w1_ref, w2_ref, np_ref, wpg_ref, wpp_ref, o_ref, *, n_chunk):
    _ffn_kernel(x_ref, p_ref, nf_ref, w1_ref, w2_ref, np_ref, wpg_ref, wpp_ref, None, o_ref,
                n_chunk=n_chunk)


def _ffn(x, p, norm_ffn, w1, w2, norm_ple, w_pg, w_pp, norm_final, tm, n_chunk):
    B, S, D = x.shape
    PD = p.shape[-1]
    FF = w1.shape[1]
    const = lambda b, i: (0, 0)
    row = lambda b, i: (b, i, 0)
    in_specs = [
        pl.BlockSpec((None, tm, D), row),
        pl.BlockSpec((None, tm, PD), row),
        pl.BlockSpec((1, D), const),
        pl.BlockSpec((D, FF), const, pipeline_mode=pl.Buffered(1)),
        pl.BlockSpec((FF, D), const, pipeline_mode=pl.Buffered(1)),
        pl.BlockSpec((1, D), const),
        pl.BlockSpec((D, D), const, pipeline_mode=pl.Buffered(1)),
        pl.BlockSpec((PD, D), const, pipeline_mode=pl.Buffered(1)),
    ]
    args = [x, p, norm_ffn, w1, w2, norm_ple, w_pg, w_pp]
    if norm_final is not None:
        in_specs.append(pl.BlockSpec((1, D), const))
        args.append(norm_final)
        body = functools.partial(_ffn_kernel, n_chunk=n_chunk)
    else:
        body = functools.partial(_ffn_kernel_plain, n_chunk=n_chunk)
    return pl.pallas_call(
        body,
        grid=(B, S // tm),
        in_specs=in_specs,
        out_specs=pl.BlockSpec((None, tm, D), row),
        out_shape=jax.ShapeDtypeStruct((B, S, D), F32),
        compiler_params=pltpu.CompilerParams(
            dimension_semantics=("parallel", "parallel"), vmem_limit_bytes=VMEM_LIMIT),
        name="ffn",
    )(*args)


def _block_diag2(w):
    z = jnp.zeros_like(w[0])
    return jnp.concatenate([jnp.concatenate([w[0], z], axis=1),
                            jnp.concatenate([z, w[1]], axis=1)], axis=0)


def kernel(x, p, norm_mix, w_in, shift_mu_prev, shift_mu_next, rwkv_w0, rwkv_w2, rwkv_a0,
           rwkv_a2, rwkv_g2, rwkv_k_k, rwkv_k_a, rwkv_r_k, rwkv_ln_w, rwkv_ln_b, rwkv_w_o,
           da_lq1, da_lk1, da_lq2, da_lk2, da_subln_w, da_w_o, w_out, norm_ffn, w_ff1, w_ff2,
           norm_ple, w_ple_gate, w_ple_proj, norm_final):
    B, S, D = x.shape
    L = w_in.shape[0]
    W = rwkv_w0.shape[-1]
    n_r = shift_mu_prev.shape[-1]
    n_qk = 2 * da_w_o.shape[1]
    n_v = da_w_o.shape[1]
    cos_t, sin_t = _rope_tables(S)
    seg = jnp.arange(W) // HEAD
    e_seg = (seg[:, None] == seg[None, :]).astype(BF16)
    tm = min(256, S)
    for i in range(L):
        lambda_init = 0.8 - 0.6 * math.exp(-0.3 * i)
        w_i = w_in[i].astype(BF16)
        u_r, q, k, v = _in_proj(
            x, norm_mix[i][None], w_i[:, :n_r], w_i[:, n_r:n_r + n_qk],
            w_i[:, n_r + n_qk:n_r + n_qk + n_v].T, cos_t, sin_t, tm)
        r, vv, kk, g, bonus, lw, bb, kd = _rwkv_prep(
            u_r, shift_mu_prev[i][None], shift_mu_next[i][None],
            rwkv_w0[i].reshape(1, 2 * W), _block_diag2(rwkv_w2[i]).astype(BF16),
            rwkv_a0[i].reshape(1, 2 * W), _block_diag2(rwkv_a2[i]).astype(BF16),
            rwkv_g2[i].astype(BF16), rwkv_k_k[i][None], rwkv_k_a[i][None],
            rwkv_r_k[i].reshape(1, W), e_seg, tm)
        y = _rwkv_scan(r, vv, kk, lw, bb, kd, nsub=min(8, S // CHUNK))
        ob = _diff_attn(q, k, v, da_lq1[i][None], da_lk1[i][None], da_lq2[i][None],
                        da_lk2[i][None], da_subln_w[i][:, None], lambda_init,
                        tq=min(512, S), tk=min(512, S // 2))
        x = _merge(x, y, bonus, g, ob, norm_mix[i][None],
                   w_i[:, n_r + n_qk + n_v:], rwkv_ln_w[i][None], rwkv_ln_b[i][None], e_seg,
                   rwkv_w_o[i].astype(BF16), da_w_o[i].astype(BF16), w_out[i].astype(BF16), tm)
        x = _ffn(x, p[i], norm_ffn[i][None], w_ff1[i].astype(BF16), w_ff2[i].astype(BF16),
                 norm_ple[i][None], w_ple_gate[i].astype(BF16), w_ple_proj[i].astype(BF16),
                 norm_final[None] if i == L - 1 else None, tm, n_chunk=4)
    return x
```

```python
import functools
import math

import jax
import jax.numpy as jnp
from jax import lax
from jax.experimental import pallas as pl
from jax.experimental.pallas import tpu as pltpu

F32 = jnp.float32
BF16 = jnp.bfloat16

HEAD = 64
LANES = 128
CHUNK = 64
RMS_EPS = 1e-6
GN_EPS = 64e-5
SUBLN_EPS = 1e-5
ROPE_THETA = 500000.0
ROPE_DIM = 16
Q_SCALE = HEAD ** -0.5 * math.log2(math.e)
DECAY_SCALE = math.exp(-0.5)
VMEM_LIMIT = 56 * 1024 * 1024
TILES_PER_ITER = 4


def _rms(x, g, eps):
    return x * lax.rsqrt(jnp.mean(x * x, axis=-1, keepdims=True) + eps) * g


def _sigmoid(x):
    return 1.0 / (1.0 + jnp.exp(-x))


def _mm(a, b):
    return jnp.dot(a.astype(BF16), b.astype(BF16), preferred_element_type=F32)


def _mm_nt(a, b):
    return lax.dot_general(a.astype(BF16), b.astype(BF16), (((1,), (1,)), ((), ())),
                           preferred_element_type=F32)


def _split2(x):
    hi = x.astype(BF16)
    lo = (x - hi.astype(F32)).astype(BF16)
    return hi, lo


def _mm_exact_rhs(a, b_bf16):
    hi, lo = _split2(a)
    return (jnp.dot(hi, b_bf16, preferred_element_type=F32)
            + jnp.dot(lo, b_bf16, preferred_element_type=F32))


def _mm3(a, b):
    ah, al = _split2(a)
    bh, bl = _split2(b)
    return (jnp.dot(ah, bh, preferred_element_type=F32)
            + jnp.dot(ah, bl, preferred_element_type=F32)
            + jnp.dot(al, bh, preferred_element_type=F32))


def _inproj_kernel(x_ref, g_ref, wr_ref, wqk_ref, wv_ref, cos_ref, sin_ref,
                   ur_ref, q_ref, k_ref, v_ref):
    h = _rms(x_ref[...], g_ref[...], RMS_EPS).astype(BF16)
    ur_ref[...] = jnp.dot(h, wr_ref[...], preferred_element_type=F32)
    v_ref[...] = lax.dot_general(wv_ref[...], h, (((1,), (1,)), ((), ())),
                                 preferred_element_type=F32).astype(BF16)
    cos_t = cos_ref[...]
    sin_t = sin_ref[...]
    lane = lax.broadcasted_iota(jnp.int32, cos_t.shape, 1)
    first_half = (lane & (HEAD - 1)) < (ROPE_DIM // 2)
    n_tiles = wqk_ref.shape[1] // LANES
    qk = jnp.dot(h, wqk_ref[...], preferred_element_type=F32)
    for c in range(n_tiles):
        xq = qk[:, c * LANES:(c + 1) * LANES]
        partner = jnp.where(first_half,
                            pltpu.roll(xq, LANES - ROPE_DIM // 2, 1),
                            pltpu.roll(xq, ROPE_DIM // 2, 1))
        ro = xq * cos_t + partner * sin_t
        if c < n_tiles // 2:
            q_ref[:, c * LANES:(c + 1) * LANES] = (ro * Q_SCALE).astype(BF16)
        else:
            cc = c - n_tiles // 2
            k_ref[:, cc * LANES:(cc + 1) * LANES] = ro.astype(BF16)


def _rope_tables(S):
    pos = jnp.arange(S, dtype=F32)
    inv_freq = ROPE_THETA ** (-jnp.arange(0, ROPE_DIM, 2, dtype=F32) / ROPE_DIM)
    ang = pos[:, None] * inv_freq[None, :]
    cos8, sin8 = jnp.cos(ang), jnp.sin(ang)
    ones = jnp.ones((S, HEAD - ROPE_DIM), F32)
    zeros = jnp.zeros((S, HEAD - ROPE_DIM), F32)
    cos64 = jnp.concatenate([cos8, cos8, ones], axis=1)
    sin64 = jnp.concatenate([-sin8, sin8, zeros], axis=1)
    return jnp.tile(cos64, (1, 2)), jnp.tile(sin64, (1, 2))


def _in_proj(x, g, w_r, w_qk, w_vt, cos_t, sin_t, tm):
    B, S, D = x.shape
    nr, nqk, nv = w_r.shape[1], w_qk.shape[1], w_vt.shape[0]
    const = lambda b, i: (0, 0)
    row = lambda b, i: (b, i, 0)
    return pl.pallas_call(
        _inproj_kernel,
        grid=(B, S // tm),
        in_specs=[
            pl.BlockSpec((None, tm, D), row),
            pl.BlockSpec((1, D), const),
            pl.BlockSpec((D, nr), const),
            pl.BlockSpec((D, nqk), const),
            pl.BlockSpec((nv, D), const),
            pl.BlockSpec((tm, LANES), lambda b, i: (i, 0)),
            pl.BlockSpec((tm, LANES), lambda b, i: (i, 0)),
        ],
        out_specs=[
            pl.BlockSpec((None, tm, nr), row),
            pl.BlockSpec((None, tm, nqk // 2), row),
            pl.BlockSpec((None, tm, nqk // 2), row),
            pl.BlockSpec((None, nv, tm), lambda b, i: (b, 0, i)),
        ],
        out_shape=[
            jax.ShapeDtypeStruct((B, S, nr), F32),
            jax.ShapeDtypeStruct((B, S, nqk // 2), BF16),
            jax.ShapeDtypeStruct((B, S, nqk // 2), BF16),
            jax.ShapeDtypeStruct((B, nv, S), BF16),
        ],
        compiler_params=pltpu.CompilerParams(
            dimension_semantics=("parallel", "parallel"), vmem_limit_bytes=VMEM_LIMIT),
        name="in_proj",
    )(x, g, w_r, w_qk, w_vt, cos_t, sin_t)


def _prep_kernel(u_ref, up_ref, un_ref, mup_ref, mun_ref, w0_ref, w2_ref, a0_ref, a2_ref,
                 g2_ref, kk_w_ref, ka_ref, rk_ref, e_ref,
                 r_ref, v_ref, kk_ref, g_ref, bonus_ref, lw_ref, b_ref, kd_ref):
    i = pl.program_id(1)
    n = pl.num_programs(1)
    u = u_ref[...]
    tm = u.shape[0]
    W = r_ref.shape[-1]
    row = lax.broadcasted_iota(jnp.int32, (tm, 1), 0)
    prev_row = jnp.where(i > 0, up_ref[7:8, :], 0.0)
    next_row = jnp.where(i < n - 1, un_ref[0:1, :], 0.0)
    u_prev = jnp.where(row == 0, prev_row, pltpu.roll(u, 1, 0))
    u_next = jnp.where(row == tm - 1, next_row, pltpu.roll(u, tm - 1, 0))
    us = u + mup_ref[...] * (u_prev - u) + mun_ref[...] * (u_next - u)

    r = us[:, 0:W]
    k = us[:, W:2 * W]
    v = us[:, 2 * W:3 * W]
    wd = us[:, 3 * W:3 * W + LANES]
    ad = us[:, 3 * W + LANES:3 * W + 2 * LANES]
    gd = us[:, 3 * W + 2 * LANES:3 * W + 3 * LANES]

    z = w0_ref[...] + _mm(jnp.tanh(wd), w2_ref[...])
    lw = -(_sigmoid(z) * DECAY_SCALE)
    a = _sigmoid(a0_ref[...] + _mm(ad, a2_ref[...]))
    g_ref[...] = _mm(_sigmoid(gd), g2_ref[...])

    e = e_ref[...]
    kraw = k * kk_w_ref[...]
    ss = _mm_exact_rhs(kraw * kraw, e)
    kk = kraw / jnp.maximum(jnp.sqrt(ss), 1e-12)
    ka = ka_ref[...]
    kd_sum = jnp.zeros_like(k)
    for d in range(2):
        a_d = a[:, d * W:(d + 1) * W]
        kd = k * (1.0 + (a_d - 1.0) * ka)
        kd_sum = kd_sum + kd
        lw_ref[d] = lw[:, d * W:(d + 1) * W]
        b_ref[d] = kk * a_d
        kd_ref[d] = kd
    c = _mm_exact_rhs(r * rk_ref[...] * kd_sum, e)
    r_ref[...] = r
    v_ref[...] = v
    kk_ref[...] = kk
    bonus_ref[...] = c * v


def _rwkv_prep(u_r, mu_prev, mu_next, w0, w2bd, a0, a2bd, g2, k_k, k_a, r_k, e_seg, tm):
    B, S, C = u_r.shape
    W = k_k.shape[-1]
    nblk8 = S // 8
    const = lambda b, i: (0, 0)
    row = lambda b, i: (b, i, 0)
    drow = lambda b, i: (0, b, i, 0)
    tok = jax.ShapeDtypeStruct((B, S, W), F32)
    dtok = jax.ShapeDtypeStruct((2, B, S, W), F32)
    return pl.pallas_call(
        _prep_kernel,
        grid=(B, S // tm),
        in_specs=[
            pl.BlockSpec((None, tm, C), row),
            pl.BlockSpec((None, 8, C), lambda b, i: (b, jnp.maximum(i * (tm // 8) - 1, 0), 0)),
            pl.BlockSpec((None, 8, C), lambda b, i: (b, jnp.minimum((i + 1) * (tm // 8), nblk8 - 1), 0)),
            pl.BlockSpec((1, C), const),
            pl.BlockSpec((1, C), const),
            pl.BlockSpec((1, 2 * W), const),
            pl.BlockSpec((LANES, 2 * W), const),
            pl.BlockSpec((1, 2 * W), const),
            pl.BlockSpec((LANES, 2 * W), const),
            pl.BlockSpec((LANES, W), const),
            pl.BlockSpec((1, W), const),
            pl.BlockSpec((1, W), const),
            pl.BlockSpec((1, W), const),
            pl.BlockSpec((W, W), const),
        ],
        out_specs=[pl.BlockSpec((None, tm, W), row)] * 5
                  + [pl.BlockSpec((2, None, tm, W), drow)] * 3,
        out_shape=[tok] * 5 + [dtok] * 3,
        compiler_params=pltpu.CompilerParams(
            dimension_semantics=("parallel", "parallel"), vmem_limit_bytes=VMEM_LIMIT),
        name="rwkv_prep",
    )(u_r, u_r, u_r, mu_prev, mu_next, w0, w2bd, a0, a2bd, g2, k_k, k_a, r_k, e_seg)


def _scan_kernel(r_ref, v_ref, kk_ref, lw_ref, b_ref, kd_ref, y_ref,
                 t_scr, g_scr, h_scr, q_scr, yl_scr, *, nsub, n_pair, n_step):
    step = pl.program_id(0)
    last = pl.num_programs(0) - 2
    row_cur = jnp.minimum(step, last) // n_step
    step_prev = jnp.maximum(step - 1, 0)
    rev = (row_cur // n_pair) % 2
    rev_prev = ((step_prev // n_step) // n_pair) % 2
    sgn = 1 - 2 * rev
    C = CHUNK
    P2 = 2 * C

    @pl.when(step == 0)
    def _():
        t_scr[...] = jnp.zeros_like(t_scr)
        g_scr[...] = jnp.zeros_like(g_scr)
        h_scr[...] = jnp.zeros_like(h_scr)
        q_scr[...] = jnp.zeros_like(q_scr)
        yl_scr[...] = jnp.zeros_like(yl_scr)

    def slices(direction):
        out = []
        for s in range(nsub):
            ci = s + direction * (nsub - 1 - 2 * s)
            out.append(pl.ds(pl.multiple_of(ci * C, C), C))
        return out

    chunk_slices = slices(rev)
    chain_slices = slices(rev_prev)

    chain = {"t": jnp.where(step_prev % n_step == 0, 0.0, t_scr[...]), "next": 0}

    def chain_steps(n):
        for _ in range(n):
            s = chain["next"]
            if s == nsub:
                return
            t_cur = chain["t"]
            y_st = _mm(q_scr[s], t_cur) + yl_scr[s]
            chain["t"] = _mm3(g_scr[s], t_cur) + h_scr[s]
            y_ref[chain_slices[s], :] = y_st[:C, :] + y_st[C:, :]
            chain["next"] = s + 1

    row = lax.broadcasted_iota(jnp.int32, (P2, P2), 0)
    col = lax.broadcasted_iota(jnp.int32, (P2, P2), 1)
    rt = row & (C - 1)
    ct = col & (C - 1)
    delta = (rt - ct) * sgn
    strict = delta > 0
    incl = delta >= 0
    eye = row == col
    ident = jnp.where(eye, 1.0, 0.0)
    blk8 = (rt >> 3) == (ct >> 3)
    blk16 = (rt >> 4) == (ct >> 4)
    blk32 = (rt >> 5) == (ct >> 5)
    in8 = strict & blk8
    in16 = strict & blk16 & ~blk8
    in32 = strict & blk32 & ~blk16
    in64 = strict & ~blk32
    rc = lax.broadcasted_iota(jnp.int32, (C, C), 0)
    cc = lax.broadcasted_iota(jnp.int32, (C, C), 1)
    lcum = jnp.where((rc - cc) * sgn >= 0, 1.0, 0.0).astype(BF16)
    lane = lax.broadcasted_iota(jnp.int32, (C, LANES), 1)
    head0 = lane < HEAD

    def stack(x):
        return jnp.concatenate([jnp.where(head0, x, 0.0), jnp.where(head0, 0.0, x)], axis=0)

    subs = range(nsub)
    sls = chunk_slices

    lws = [lw_ref[sl, :] for sl in sls]
    cums = []
    for lw in lws:
        l1 = lw.astype(BF16)
        rem = lw - l1.astype(F32)
        l2 = rem.astype(BF16)
        l3 = (rem - l2.astype(F32)).astype(BF16)
        cums.append(jnp.dot(lcum, l1, preferred_element_type=F32)
                    + jnp.dot(lcum, l2, preferred_element_type=F32)
                    + jnp.dot(lcum, l3, preferred_element_type=F32))
    tots = [jnp.sum(lw, axis=0, keepdims=True) for lw in lws]
    a_st, r_st, b_st, k_st, bh_t, kh_t, v_st = [], [], [], [], [], [], []
    for s in subs:
        sl, lw, cum, tot = sls[s], lws[s], cums[s], tots[s]
        e_neg = jnp.exp(-cum)
        e_rem = jnp.exp(tot - cum)
        b = b_ref[sl, :]
        kd = kd_ref[sl, :]
        a_st.append(stack(kk_ref[sl, :] * -jnp.exp(cum - lw)).astype(BF16))
        r_st.append(stack(r_ref[sl, :] * jnp.exp(cum)))
        b_st.append(stack(b * e_neg).astype(BF16))
        k_st.append(stack(kd * e_neg).astype(BF16))
        bh_t.append(stack(b * e_rem).T.astype(BF16))
        kh_t.append(stack(kd * e_rem).T.astype(BF16))
        v_st.append(stack(v_ref[sl, :]).astype(BF16))

    nt = (((1,), (1,)), ((), ()))
    sc = [lax.dot_general(jnp.concatenate([a_st[s], r_st[s].astype(BF16)], axis=0),
                          jnp.concatenate([b_st[s], k_st[s]], axis=0), nt,
                          preferred_element_type=F32) for s in subs]
    per_stage = -(-nsub // 8)
    chain_steps(per_stage)
    l_ab = [x[:P2, :P2] for x in sc]
    l_ak = [jnp.where(strict, x[:P2, P2:], 0.0).astype(BF16) for x in sc]
    a_rb = [jnp.where(incl, x[P2:, :P2], 0.0).astype(BF16) for x in sc]
    a_rk = [jnp.where(incl, x[P2:, P2:], 0.0).astype(BF16) for x in sc]
    x_loc = [jnp.dot(l_ak[s], v_st[s], preferred_element_type=F32) for s in subs]
    y_rkv = [jnp.dot(a_rk[s], v_st[s], preferred_element_type=F32) for s in subs]
    chain_steps(per_stage)

    d8 = [jnp.where(in8, x, 0.0) for x in l_ab]
    m1 = [ident + x for x in d8]
    p1 = [_mm(x, x) for x in d8]
    chain_steps(per_stage)
    pm = [_mm(p1[s], jnp.concatenate([p1[s], m1[s]], axis=1)) for s in subs]
    m2 = [m1[s] + pm[s][:, P2:] for s in subs]
    m8 = [m2[s] + _mm(pm[s][:, :P2], m2[s]) for s in subs]
    chain_steps(per_stage)
    o16 = [jnp.where(in16, x, 0.0) for x in l_ab]
    om = [_mm(o16[s], m8[s]) for s in subs]
    m16 = [m8[s] + _mm(m8[s], om[s]) for s in subs]
    chain_steps(per_stage)
    o32 = [jnp.where(in32, x, 0.0) for x in l_ab]
    om = [_mm(o32[s], m16[s]) for s in subs]
    m32 = [(m16[s] + _mm(m16[s], om[s])).astype(BF16) for s in subs]
    chain_steps(per_stage)
    o64 = [jnp.where(in64, x, 0.0).astype(BF16) for x in l_ab]
    y1 = [jnp.dot(m32[s], jnp.concatenate([a_st[s], x_loc[s].astype(BF16)], axis=1),
                  preferred_element_type=F32) for s in subs]
    oy = [_mm(o64[s], y1[s]).astype(BF16) for s in subs]
    chain_steps(per_stage)
    wu16 = [(y1[s] + jnp.dot(m32[s], oy[s], preferred_element_type=F32)).astype(BF16)
            for s in subs]
    qy = [jnp.dot(a_rb[s], wu16[s], preferred_element_type=F32) for s in subs]
    chain_steps(per_stage)
    gh = [jnp.dot(bh_t[s], wu16[s], preferred_element_type=F32) for s in subs]
    chain_steps(nsub)
    t_scr[...] = chain["t"]
    for s in subs:
        q_scr[s] = r_st[s] + qy[s][:, :LANES]
        yl_scr[s] = qy[s][:, LANES:] + y_rkv[s]
        g_scr[s] = jnp.where(eye, jnp.exp(tots[s]), 0.0) + gh[s][:, :LANES]
        h_scr[s] = gh[s][:, LANES:] + jnp.dot(kh_t[s], v_st[s], preferred_element_type=F32)


def _rwkv_scan(r, v, kk, lw, bb, kd, nsub):
    B, S, W = r.shape
    n_pair = W // LANES
    T = nsub * CHUNK
    n_step = S // T

    n_total = B * 2 * n_pair * n_step

    def locate(t):
        g = t // n_step
        c = t % n_step
        d = (g // n_pair) % 2
        return d, g // (2 * n_pair), c + d * (n_step - 1 - 2 * c), g % n_pair

    def shared(t):
        _, b, blk, p = locate(jnp.minimum(t, n_total - 1))
        return (b, blk, p)

    def per_dir(t):
        return locate(jnp.minimum(t, n_total - 1))

    def out_map(t):
        return locate(jnp.maximum(t - 1, 0))

    mat = pltpu.VMEM((nsub, LANES, LANES), F32)
    return pl.pallas_call(
        functools.partial(_scan_kernel, nsub=nsub, n_pair=n_pair, n_step=n_step),
        grid=(n_total + 1,),
        in_specs=[pl.BlockSpec((None, T, LANES), shared)] * 3
                 + [pl.BlockSpec((None, None, T, LANES), per_dir)] * 3,
        out_specs=pl.BlockSpec((None, None, T, LANES), out_map),
        out_shape=jax.ShapeDtypeStruct((2, B, S, W), F32),
        scratch_shapes=[pltpu.VMEM((LANES, LANES), F32), mat, mat, mat, mat],
        compiler_params=pltpu.CompilerParams(
            dimension_semantics=("arbitrary",), vmem_limit_bytes=VMEM_LIMIT),
        name="rwkv_scan",
    )(r, v, kk, lw, bb, kd)


def _attn_kernel(q_ref, k_ref, vt_ref, lq1_ref, lk1_ref, lq2_ref, lk2_ref, sw_ref, o_ref,
                 m_scr, l_scr, alpha_scr, acc_scr, s_even, s_odd, p_even, p_odd, mx_even, mx_odd,
                 *, tk, lambda_init):
    n_kv = k_ref.shape[0] // tk
    q = q_ref[...]
    lane = lax.broadcasted_iota(jnp.int32, q.shape, 1)
    zero = jnp.zeros_like(q)
    qc = (jnp.where(lane < HEAD, q, zero), jnp.where(lane < HEAD, zero, q))
    m_scr[...] = jnp.full(m_scr.shape, -jnp.inf, F32)
    l_scr[...] = jnp.zeros(l_scr.shape, F32)
    acc_scr[...] = jnp.zeros(acc_scr.shape, F32)
    alpha_scr[...] = jnp.ones(alpha_scr.shape, F32)
    p_odd[...] = jnp.zeros(p_odd.shape, BF16)

    def scores(j, c):
        ks = k_ref[pl.ds(pl.multiple_of(j * tk, tk), tk), :]
        return lax.dot_general(ks, qc[c], (((1,), (1,)), ((), ())),
                               preferred_element_type=F32)

    def weighted_values(j, p_ref, c):
        vts = vt_ref[:, pl.ds(pl.multiple_of(j * tk, tk), tk)]
        acc_scr[c] = alpha_scr[c] * acc_scr[c] + jnp.dot(vts, p_ref[c],
                                                         preferred_element_type=F32)

    def put_scores(j, s_ref, mx_ref, c):
        s = scores(j, c)
        s_ref[c] = s
        mx_ref[c] = jnp.max(s, axis=0, keepdims=True)

    def tile_step(j, s_cur, mx_cur, s_nxt, mx_nxt, p_cur, p_prev):
        j_prev = jnp.maximum(j - 1, 0)
        j_next = jnp.minimum(j + 1, n_kv - 1)
        for c in range(2):
            weighted_values(j_prev, p_prev, c)
            put_scores(j_next, s_nxt, mx_nxt, c)
        for c in range(2):
            m_old = m_scr[c]
            m_new = jnp.maximum(m_old, mx_cur[c])
            alpha = jnp.exp2(m_old - m_new)
            p = jnp.exp2(s_cur[c] - m_new)
            l_scr[c] = alpha * l_scr[c] + jnp.sum(p, axis=0, keepdims=True)
            p_cur[c] = p.astype(BF16)
            alpha_scr[c] = alpha
            m_scr[c] = m_new

    for c in range(2):
        put_scores(0, s_even, mx_even, c)

    unroll = TILES_PER_ITER if n_kv % TILES_PER_ITER == 0 else 2

    def body(i, carry):
        for t in range(0, unroll, 2):
            tile_step(unroll * i + t, s_even, mx_even, s_odd, mx_odd, p_even, p_odd)
            tile_step(unroll * i + t + 1, s_odd, mx_odd, s_even, mx_even, p_odd, p_even)
        return carry

    lax.fori_loop(0, n_kv // unroll, body, 0)
    for c in range(2):
        weighted_values(n_kv - 1, p_odd, c)

    lam = (jnp.exp(jnp.sum(lq1_ref[...] * lk1_ref[...], axis=-1, keepdims=True))
           - jnp.exp(jnp.sum(lq2_ref[...] * lk2_ref[...], axis=-1, keepdims=True)) + lambda_init)
    o = acc_scr[0] / l_scr[0] - lam * (acc_scr[1] / l_scr[1])
    o = o * lax.rsqrt(jnp.mean(o * o, axis=0, keepdims=True) + SUBLN_EPS)
    o = o * (sw_ref[...] * (1.0 - lambda_init))
    o_ref[...] = o.T.astype(o_ref.dtype)


def _diff_attn(q, k, vt, lq1, lk1, lq2, lk2, subln_w, lambda_init, tq, tk):
    B, S, QW = q.shape
    H = QW // LANES
    DV = vt.shape[1] // H
    const = lambda b, h, i: (0, 0)
    return pl.pallas_call(
        functools.partial(_attn_kernel, tk=tk, lambda_init=lambda_init),
        grid=(B, H, S // tq),
        in_specs=[
            pl.BlockSpec((None, tq, LANES), lambda b, h, i: (b, i, h)),
            pl.BlockSpec((None, S, LANES), lambda b, h, i: (b, 0, h)),
            pl.BlockSpec((None, DV, S), lambda b, h, i: (b, h, 0)),
            pl.BlockSpec((1, HEAD), const),
            pl.BlockSpec((1, HEAD), const),
            pl.BlockSpec((1, HEAD), const),
            pl.BlockSpec((1, HEAD), const),
            pl.BlockSpec((DV, 1), const),
        ],
        out_specs=pl.BlockSpec((None, tq, DV), lambda b, h, i: (b, i, h)),
        out_shape=jax.ShapeDtypeStruct((B, S, H * DV), BF16),
        scratch_shapes=[pltpu.VMEM((2, 1, tq), F32),
                        pltpu.VMEM((2, 1, tq), F32),
                        pltpu.VMEM((2, 1, tq), F32),
                        pltpu.VMEM((2, DV, tq), F32),
                        pltpu.VMEM((2, tk, tq), F32),
                        pltpu.VMEM((2, tk, tq), F32),
                        pltpu.VMEM((2, tk, tq), BF16),
                        pltpu.VMEM((2, tk, tq), BF16),
                        pltpu.VMEM((2, 1, tq), F32),
                        pltpu.VMEM((2, 1, tq), F32)],
        compiler_params=pltpu.CompilerParams(
            dimension_semantics=("parallel", "parallel", "arbitrary"),
            vmem_limit_bytes=VMEM_LIMIT),
        name="diff_attn",
    )(q, k, vt, lq1, lk1, lq2, lk2, subln_w)


def _merge_kernel(x_ref, y_ref, bonus_ref, g_ref, ob_ref, nm_ref, wg_ref, lnw_ref, lnb_ref,
                  e_ref, woa_ref, wob_ref, wout_ref, o_ref):
    x = x_ref[...]
    D = x.shape[-1]
    h = _rms(x, nm_ref[...], RMS_EPS).astype(BF16)
    gates = _sigmoid(jnp.dot(h, wg_ref[...], preferred_element_type=F32))
    e = e_ref[...]
    y = y_ref[0] + y_ref[1]
    mean = _mm_exact_rhs(y, e) * (1.0 / HEAD)
    yc = y - mean
    var = _mm_exact_rhs(yc * yc, e) * (1.0 / HEAD)
    yn = yc * lax.rsqrt(var + GN_EPS) * lnw_ref[...] + lnb_ref[...]
    ya = _mm((yn + bonus_ref[...]) * g_ref[...], woa_ref[...])
    yb = jnp.dot(ob_ref[...], wob_ref[...], preferred_element_type=F32)
    merged = gates[:, :D] * ya + gates[:, D:] * yb
    o_ref[...] = x + _mm(merged, wout_ref[...])


def _merge(x, y, bonus, g, ob, norm_mix, w_g, ln_w, ln_b, e_seg, w_oa, w_ob, w_out, tm):
    B, S, D = x.shape
    W = bonus.shape[-1]
    const = lambda b, i: (0, 0)
    row = lambda b, i: (b, i, 0)
    return pl.pallas_call(
        _merge_kernel,
        grid=(B, S // tm),
        in_specs=[
            pl.BlockSpec((None, tm, D), row),
            pl.BlockSpec((2, None, tm, W), lambda b, i: (0, b, i, 0)),
            pl.BlockSpec((None, tm, W), row),
            pl.BlockSpec((None, tm, W), row),
            pl.BlockSpec((None, tm, W), row),
            pl.BlockSpec((1, D), const),
            pl.BlockSpec((D, 2 * D), const),
            pl.BlockSpec((1, W), const),
            pl.BlockSpec((1, W), const),
            pl.BlockSpec((W, W), const),
            pl.BlockSpec((W, D), const),
            pl.BlockSpec((W, D), const),
            pl.BlockSpec((D, D), const),
        ],
        out_specs=pl.BlockSpec((None, tm, D), row),
        out_shape=jax.ShapeDtypeStruct((B, S, D), F32),
        compiler_params=pltpu.CompilerParams(
            dimension_semantics=("parallel", "parallel"), vmem_limit_bytes=VMEM_LIMIT),
        name="merge",
    )(x, y, bonus, g, ob, norm_mix, w_g, ln_w, ln_b, e_seg, w_oa, w_ob, w_out)


def _ffn_kernel(x_ref, p_ref, nf_ref, w1_ref, w2_ref, np_ref, wpg_ref, wpp_ref, nfin_ref, o_ref,
                *, n_chunk):
    x = x_ref[...]
    h = _rms(x, nf_ref[...], RMS_EPS).astype(BF16)
    ff = w1_ref.shape[1] // n_chunk
    acc = x
    for c in range(n_chunk):
        t = jnp.maximum(jnp.dot(h, w1_ref[:, c * ff:(c + 1) * ff], preferred_element_type=F32), 0.0)
        acc = acc + jnp.dot((t * t).astype(BF16), w2_ref[c * ff:(c + 1) * ff, :],
                            preferred_element_type=F32)
    x = acc
    h = _rms(x, np_ref[...], RMS_EPS).astype(BF16)
    gate = _sigmoid(jnp.dot(h, wpg_ref[...], preferred_element_type=F32))
    x = x + gate * _mm(p_ref[...], wpp_ref[...])
    o_ref[...] = x
    if nfin_ref is not None:
        o_ref[...] = _rms(x, nfin_ref[...], RMS_EPS)


def _ffn_kernel_plain(x_ref, p_ref, nf_ref, w1_ref, w2_ref, np_ref, wpg_ref, wpp_ref, o_ref, *, n_chunk):
    _ffn_kernel(x_ref, p_ref, nf_ref, w1_ref, w2_ref, np_ref, wpg_ref, wpp_ref, None, o_ref,
                n_chunk=n_chunk)


def _ffn(x, p, norm_ffn, w1, w2, norm_ple, w_pg, w_pp, norm_final, tm, n_chunk):
    B, S, D = x.shape
    PD = p.shape[-1]
    FF = w1.shape[1]
    const = lambda b, i: (0, 0)
    row = lambda b, i: (b, i, 0)
    in_specs = [
        pl.BlockSpec((None, tm, D), row),
        pl.BlockSpec((None, tm, PD), row),
        pl.BlockSpec((1, D), const),
        pl.BlockSpec((D, FF), const, pipeline_mode=pl.Buffered(1)),
        pl.BlockSpec((FF, D), const, pipeline_mode=pl.Buffered(1)),
        pl.BlockSpec((1, D), const),
        pl.BlockSpec((D, D), const, pipeline_mode=pl.Buffered(1)),
        pl.BlockSpec((PD, D), const, pipeline_mode=pl.Buffered(1)),
    ]
    args = [x, p, norm_ffn, w1, w2, norm_ple, w_pg, w_pp]
    if norm_final is not None:
        in_specs.append(pl.BlockSpec((1, D), const))
        args.append(norm_final)
        body = functools.partial(_ffn_kernel, n_chunk=n_chunk)
    else:
        body = functools.partial(_ffn_kernel_plain, n_chunk=n_chunk)
    return pl.pallas_call(
        body,
        grid=(B, S // tm),
        in_specs=in_specs,
        out_specs=pl.BlockSpec((None, tm, D), row),
        out_shape=jax.ShapeDtypeStruct((B, S, D), F32),
        compiler_params=pltpu.CompilerParams(
            dimension_semantics=("parallel", "parallel"), vmem_limit_bytes=VMEM_LIMIT),
        name="ffn",
    )(*args)


def _block_diag2(w):
    z = jnp.zeros_like(w[0])
    return jnp.concatenate([jnp.concatenate([w[0], z], axis=1),
                            jnp.concatenate([z, w[1]], axis=1)], axis=0)


def kernel(x, p, norm_mix, w_in, shift_mu_prev, shift_mu_next, rwkv_w0, rwkv_w2, rwkv_a0,
           rwkv_a2, rwkv_g2, rwkv_k_k, rwkv_k_a, rwkv_r_k, rwkv_ln_w, rwkv_ln_b, rwkv_w_o,
           da_lq1, da_lk1, da_lq2, da_lk2, da_subln_w, da_w_o, w_out, norm_ffn, w_ff1, w_ff2,
           norm_ple, w_ple_gate, w_ple_proj, norm_final):
    B, S, D = x.shape
    L = w_in.shape[0]
    W = rwkv_w0.shape[-1]
    n_r = shift_mu_prev.shape[-1]
    n_qk = 2 * da_w_o.shape[1]
    n_v = da_w_o.shape[1]
    cos_t, sin_t = _rope_tables(S)
    seg = jnp.arange(W) // HEAD
    e_seg = (seg[:, None] == seg[None, :]).astype(BF16)
    tm = min(256, S)
    for i in range(L):
        lambda_init = 0.8 - 0.6 * math.exp(-0.3 * i)
        w_i = w_in[i].astype(BF16)
        u_r, q, k, v = _in_proj(
            x, norm_mix[i][None], w_i[:, :n_r], w_i[:, n_r:n_r + n_qk],
            w_i[:, n_r + n_qk:n_r + n_qk + n_v].T, cos_t, sin_t, tm)
        r, vv, kk, g, bonus, lw, bb, kd = _rwkv_prep(
            u_r, shift_mu_prev[i][None], shift_mu_next[i][None],
            rwkv_w0[i].reshape(1, 2 * W), _block_diag2(rwkv_w2[i]).astype(BF16),
            rwkv_a0[i].reshape(1, 2 * W), _block_diag2(rwkv_a2[i]).astype(BF16),
            rwkv_g2[i].astype(BF16), rwkv_k_k[i][None], rwkv_k_a[i][None],
            rwkv_r_k[i].reshape(1, W), e_seg, tm)
        y = _rwkv_scan(r, vv, kk, lw, bb, kd, nsub=min(8, S // CHUNK))
        ob = _diff_attn(q, k, v, da_lq1[i][None], da_lk1[i][None], da_lq2[i][None],
                        da_lk2[i][None], da_subln_w[i][:, None], lambda_init,
                        tq=min(512, S), tk=min(512, S // 2))
        x = _merge(x, y, bonus, g, ob, norm_mix[i][None],
                   w_i[:, n_r + n_qk + n_v:], rwkv_ln_w[i][None], rwkv_ln_b[i][None], e_seg,
                   rwkv_w_o[i].astype(BF16), da_w_o[i].astype(BF16), w_out[i].astype(BF16), tm)
        x = _ffn(x, p[i], norm_ffn[i][None], w_ff1[i].astype(BF16), w_ff2[i].astype(BF16),
                 norm_ple[i][None], w_ple_gate[i].astype(BF16), w_ple_proj[i].astype(BF16),
                 norm_final[None] if i == L - 1 else None, tm, n_chunk=4)
    return x
```

```python
import functools
import math

import jax
import jax.numpy as jnp
from jax import lax
from jax.experimental import pallas as pl
from jax.experimental.pallas import tpu as pltpu

F32 = jnp.float32
BF16 = jnp.bfloat16

HEAD = 64
LANES = 128
CHUNK = 64
RMS_EPS = 1e-6
GN_EPS = 64e-5
SUBLN_EPS = 1e-5
ROPE_THETA = 500000.0
ROPE_DIM = 16
Q_SCALE = HEAD ** -0.5 * math.log2(math.e)
DECAY_SCALE = math.exp(-0.5)
VMEM_LIMIT = 56 * 1024 * 1024
TILES_PER_ITER = 4
SUM_ROWS = 16


def _rms(x, g, eps):
    return x * lax.rsqrt(jnp.mean(x * x, axis=-1, keepdims=True) + eps) * g


def _sigmoid(x):
    return 1.0 / (1.0 + jnp.exp(-x))


def _mm(a, b):
    return jnp.dot(a.astype(BF16), b.astype(BF16), preferred_element_type=F32)


def _mm_nt(a, b):
    return lax.dot_general(a.astype(BF16), b.astype(BF16), (((1,), (1,)), ((), ())),
                           preferred_element_type=F32)


def _split2(x):
    hi = x.astype(BF16)
    lo = (x - hi.astype(F32)).astype(BF16)
    return hi, lo


def _mm_exact_rhs(a, b_bf16):
    hi, lo = _split2(a)
    return (jnp.dot(hi, b_bf16, preferred_element_type=F32)
            + jnp.dot(lo, b_bf16, preferred_element_type=F32))


def _mm3(a, b):
    ah, al = _split2(a)
    bh, bl = _split2(b)
    return (jnp.dot(ah, bh, preferred_element_type=F32)
            + jnp.dot(ah, bl, preferred_element_type=F32)
            + jnp.dot(al, bh, preferred_element_type=F32))


def _inproj_kernel(x_ref, g_ref, wr_ref, wqk_ref, wv_ref, cos_ref, sin_ref,
                   ur_ref, q_ref, k_ref, v_ref):
    h = _rms(x_ref[...], g_ref[...], RMS_EPS).astype(BF16)
    ur_ref[...] = jnp.dot(h, wr_ref[...], preferred_element_type=F32)
    v_ref[...] = lax.dot_general(wv_ref[...], h, (((1,), (1,)), ((), ())),
                                 preferred_element_type=F32).astype(BF16)
    cos_t = cos_ref[...]
    sin_t = sin_ref[...]
    lane = lax.broadcasted_iota(jnp.int32, cos_t.shape, 1)
    first_half = (lane & (HEAD - 1)) < (ROPE_DIM // 2)
    n_tiles = wqk_ref.shape[1] // LANES
    qk = jnp.dot(h, wqk_ref[...], preferred_element_type=F32)
    for c in range(n_tiles):
        xq = qk[:, c * LANES:(c + 1) * LANES]
        partner = jnp.where(first_half,
                            pltpu.roll(xq, LANES - ROPE_DIM // 2, 1),
                            pltpu.roll(xq, ROPE_DIM // 2, 1))
        ro = xq * cos_t + partner * sin_t
        if c < n_tiles // 2:
            q_ref[:, c * LANES:(c + 1) * LANES] = (ro * Q_SCALE).astype(BF16)
        else:
            cc = c - n_tiles // 2
            k_ref[:, cc * LANES:(cc + 1) * LANES] = ro.astype(BF16)


def _rope_tables(S):
    pos = jnp.arange(S, dtype=F32)
    inv_freq = ROPE_THETA ** (-jnp.arange(0, ROPE_DIM, 2, dtype=F32) / ROPE_DIM)
    ang = pos[:, None] * inv_freq[None, :]
    cos8, sin8 = jnp.cos(ang), jnp.sin(ang)
    ones = jnp.ones((S, HEAD - ROPE_DIM), F32)
    zeros = jnp.zeros((S, HEAD - ROPE_DIM), F32)
    cos64 = jnp.concatenate([cos8, cos8, ones], axis=1)
    sin64 = jnp.concatenate([-sin8, sin8, zeros], axis=1)
    return jnp.tile(cos64, (1, 2)), jnp.tile(sin64, (1, 2))


def _in_proj(x, g, w_r, w_qk, w_vt, cos_t, sin_t, tm):
    B, S, D = x.shape
    nr, nqk, nv = w_r.shape[1], w_qk.shape[1], w_vt.shape[0]
    const = lambda b, i: (0, 0)
    row = lambda b, i: (b, i, 0)
    return pl.pallas_call(
        _inproj_kernel,
        grid=(B, S // tm),
        in_specs=[
            pl.BlockSpec((None, tm, D), row),
            pl.BlockSpec((1, D), const),
            pl.BlockSpec((D, nr), const),
            pl.BlockSpec((D, nqk), const),
            pl.BlockSpec((nv, D), const),
            pl.BlockSpec((tm, LANES), lambda b, i: (i, 0)),
            pl.BlockSpec((tm, LANES), lambda b, i: (i, 0)),
        ],
        out_specs=[
            pl.BlockSpec((None, tm, nr), row),
            pl.BlockSpec((None, tm, nqk // 2), row),
            pl.BlockSpec((None, tm, nqk // 2), row),
            pl.BlockSpec((None, nv, tm), lambda b, i: (b, 0, i)),
        ],
        out_shape=[
            jax.ShapeDtypeStruct((B, S, nr), F32),
            jax.ShapeDtypeStruct((B, S, nqk // 2), BF16),
            jax.ShapeDtypeStruct((B, S, nqk // 2), BF16),
            jax.ShapeDtypeStruct((B, nv, S), BF16),
        ],
        compiler_params=pltpu.CompilerParams(
            dimension_semantics=("parallel", "parallel"), vmem_limit_bytes=VMEM_LIMIT),
        name="in_proj",
    )(x, g, w_r, w_qk, w_vt, cos_t, sin_t)


def _prep_kernel(u_ref, up_ref, un_ref, mup_ref, mun_ref, w0_ref, w2_ref, a0_ref, a2_ref,
                 g2_ref, kk_w_ref, ka_ref, rk_ref, e_ref,
                 r_ref, v_ref, kk_ref, g_ref, bonus_ref, lw_ref, b_ref, kd_ref):
    i = pl.program_id(1)
    n = pl.num_programs(1)
    u = u_ref[...]
    tm = u.shape[0]
    W = r_ref.shape[-1]
    row = lax.broadcasted_iota(jnp.int32, (tm, 1), 0)
    prev_row = jnp.where(i > 0, up_ref[7:8, :], 0.0)
    next_row = jnp.where(i < n - 1, un_ref[0:1, :], 0.0)
    u_prev = jnp.where(row == 0, prev_row, pltpu.roll(u, 1, 0))
    u_next = jnp.where(row == tm - 1, next_row, pltpu.roll(u, tm - 1, 0))
    us = u + mup_ref[...] * (u_prev - u) + mun_ref[...] * (u_next - u)

    r = us[:, 0:W]
    k = us[:, W:2 * W]
    v = us[:, 2 * W:3 * W]
    wd = us[:, 3 * W:3 * W + LANES]
    ad = us[:, 3 * W + LANES:3 * W + 2 * LANES]
    gd = us[:, 3 * W + 2 * LANES:3 * W + 3 * LANES]

    z = w0_ref[...] + _mm(jnp.tanh(wd), w2_ref[...])
    lw = -(_sigmoid(z) * DECAY_SCALE)
    a = _sigmoid(a0_ref[...] + _mm(ad, a2_ref[...]))
    g_ref[...] = _mm(_sigmoid(gd), g2_ref[...])

    e = e_ref[...]
    kraw = k * kk_w_ref[...]
    ss = _mm_exact_rhs(kraw * kraw, e)
    kk = kraw / jnp.maximum(jnp.sqrt(ss), 1e-12)
    ka = ka_ref[...]
    kd_sum = jnp.zeros_like(k)
    for d in range(2):
        a_d = a[:, d * W:(d + 1) * W]
        kd = k * (1.0 + (a_d - 1.0) * ka)
        kd_sum = kd_sum + kd
        lw_ref[d] = lw[:, d * W:(d + 1) * W]
        b_ref[d] = kk * a_d
        kd_ref[d] = kd
    c = _mm_exact_rhs(r * rk_ref[...] * kd_sum, e)
    r_ref[...] = r
    v_ref[...] = v
    kk_ref[...] = kk
    bonus_ref[...] = c * v


def _rwkv_prep(u_r, mu_prev, mu_next, w0, w2bd, a0, a2bd, g2, k_k, k_a, r_k, e_seg, tm):
    B, S, C = u_r.shape
    W = k_k.shape[-1]
    nblk8 = S // 8
    const = lambda b, i: (0, 0)
    row = lambda b, i: (b, i, 0)
    drow = lambda b, i: (0, b, i, 0)
    tok = jax.ShapeDtypeStruct((B, S, W), F32)
    dtok = jax.ShapeDtypeStruct((2, B, S, W), F32)
    return pl.pallas_call(
        _prep_kernel,
        grid=(B, S // tm),
        in_specs=[
            pl.BlockSpec((None, tm, C), row),
            pl.BlockSpec((None, 8, C), lambda b, i: (b, jnp.maximum(i * (tm // 8) - 1, 0), 0)),
            pl.BlockSpec((None, 8, C), lambda b, i: (b, jnp.minimum((i + 1) * (tm // 8), nblk8 - 1), 0)),
            pl.BlockSpec((1, C), const),
            pl.BlockSpec((1, C), const),
            pl.BlockSpec((1, 2 * W), const),
            pl.BlockSpec((LANES, 2 * W), const),
            pl.BlockSpec((1, 2 * W), const),
            pl.BlockSpec((LANES, 2 * W), const),
            pl.BlockSpec((LANES, W), const),
            pl.BlockSpec((1, W), const),
            pl.BlockSpec((1, W), const),
            pl.BlockSpec((1, W), const),
            pl.BlockSpec((W, W), const),
        ],
        out_specs=[pl.BlockSpec((None, tm, W), row)] * 5
                  + [pl.BlockSpec((2, None, tm, W), drow)] * 3,
        out_shape=[tok] * 5 + [dtok] * 3,
        compiler_params=pltpu.CompilerParams(
            dimension_semantics=("parallel", "parallel"), vmem_limit_bytes=VMEM_LIMIT),
        name="rwkv_prep",
    )(u_r, u_r, u_r, mu_prev, mu_next, w0, w2bd, a0, a2bd, g2, k_k, k_a, r_k, e_seg)


def _scan_kernel(r_ref, v_ref, kk_ref, lw_ref, b_ref, kd_ref, y_ref,
                 t_scr, g_scr, h_scr, q_scr, yl_scr, *, nsub, n_pair, n_step):
    step = pl.program_id(0)
    last = pl.num_programs(0) - 2
    row_cur = jnp.minimum(step, last) // n_step
    step_prev = jnp.maximum(step - 1, 0)
    rev = (row_cur // n_pair) % 2
    rev_prev = ((step_prev // n_step) // n_pair) % 2
    sgn = 1 - 2 * rev
    C = CHUNK
    P2 = 2 * C

    @pl.when(step == 0)
    def _():
        t_scr[...] = jnp.zeros_like(t_scr)
        g_scr[...] = jnp.zeros_like(g_scr)
        h_scr[...] = jnp.zeros_like(h_scr)
        q_scr[...] = jnp.zeros_like(q_scr)
        yl_scr[...] = jnp.zeros_like(yl_scr)

    def slices(direction):
        out = []
        for s in range(nsub):
            ci = s + direction * (nsub - 1 - 2 * s)
            out.append(pl.ds(pl.multiple_of(ci * C, C), C))
        return out

    chunk_slices = slices(rev)
    chain_slices = slices(rev_prev)

    chain = {"t": jnp.where(step_prev % n_step == 0, 0.0, t_scr[...]), "next": 0}

    def chain_steps(n):
        for _ in range(n):
            s = chain["next"]
            if s == nsub:
                return
            t_cur = chain["t"]
            t16 = t_cur.astype(BF16)
            y_st = jnp.dot(q_scr[s], t16, preferred_element_type=F32) + yl_scr[s]
            chain["t"] = jnp.dot(g_scr[s], t16, preferred_element_type=F32) + h_scr[s]
            y_ref[chain_slices[s], :] = y_st[:C, :] + y_st[C:, :]
            chain["next"] = s + 1

    row = lax.broadcasted_iota(jnp.int32, (P2, P2), 0)
    col = lax.broadcasted_iota(jnp.int32, (P2, P2), 1)
    rt = row & (C - 1)
    ct = col & (C - 1)
    delta = (rt - ct) * sgn
    strict = delta > 0
    incl = delta >= 0
    eye = row == col
    ident = jnp.where(eye, 1.0, 0.0)
    blk8 = (rt >> 3) == (ct >> 3)
    blk16 = (rt >> 4) == (ct >> 4)
    blk32 = (rt >> 5) == (ct >> 5)
    in8 = strict & blk8
    in16 = strict & blk16 & ~blk8
    in32 = strict & blk32 & ~blk16
    in64 = strict & ~blk32
    lane = lax.broadcasted_iota(jnp.int32, (C, LANES), 1)
    head0 = lane < HEAD

    def stack(x):
        return jnp.concatenate([jnp.where(head0, x, 0.0), jnp.where(head0, 0.0, x)], axis=0)

    subs = range(nsub)
    sls = chunk_slices

    lws = [lw_ref[sl, :] for sl in sls]
    tots = [jnp.sum(lw, axis=0, keepdims=True) for lw in lws]
    tok = lax.broadcasted_iota(jnp.int32, (C, LANES), 0)
    cums = []
    for lw, tot in zip(lws, tots):
        pre = lw
        shift = 1
        while shift < C:
            pre = pre + jnp.where(tok >= shift, pltpu.roll(pre, shift, 0), 0.0)
            shift *= 2
        cums.append(jnp.where(rev == 1, tot - pre + lw, pre))
    a_st, r_st, b_st, k_st, bh_t, kh_t, v_st = [], [], [], [], [], [], []
    for s in subs:
        sl, lw, cum, tot = sls[s], lws[s], cums[s], tots[s]
        e_neg = jnp.exp(-cum)
        e_rem = jnp.exp(tot - cum)
        b = b_ref[sl, :]
        kd = kd_ref[sl, :]
        a_st.append(stack(kk_ref[sl, :] * -jnp.exp(cum - lw)).astype(BF16))
        r_st.append(stack(r_ref[sl, :] * jnp.exp(cum)))
        b_st.append(stack(b * e_neg).astype(BF16))
        k_st.append(stack(kd * e_neg).astype(BF16))
        bh_t.append(stack(b * e_rem).T.astype(BF16))
        kh_t.append(stack(kd * e_rem).T.astype(BF16))
        v_st.append(stack(v_ref[sl, :]).astype(BF16))

    nt = (((1,), (1,)), ((), ()))
    sc = [lax.dot_general(jnp.concatenate([a_st[s], r_st[s].astype(BF16)], axis=0),
                          jnp.concatenate([b_st[s], k_st[s]], axis=0), nt,
                          preferred_element_type=F32) for s in subs]
    per_stage = -(-nsub // 8)
    chain_steps(per_stage)
    l_ab = [x[:P2, :P2] for x in sc]
    l_ak = [jnp.where(strict, x[:P2, P2:], 0.0).astype(BF16) for x in sc]
    a_rb = [jnp.where(incl, x[P2:, :P2], 0.0).astype(BF16) for x in sc]
    a_rk = [jnp.where(incl, x[P2:, P2:], 0.0).astype(BF16) for x in sc]
    x_loc = [jnp.dot(l_ak[s], v_st[s], preferred_element_type=F32) for s in subs]
    y_rkv = [jnp.dot(a_rk[s], v_st[s], preferred_element_type=F32) for s in subs]
    chain_steps(per_stage)

    d8 = [jnp.where(in8, x, 0.0) for x in l_ab]
    m1 = [ident + x for x in d8]
    p1 = [_mm(x, x) for x in d8]
    chain_steps(per_stage)
    pm = [_mm(p1[s], jnp.concatenate([p1[s], m1[s]], axis=1)) for s in subs]
    m2 = [m1[s] + pm[s][:, P2:] for s in subs]
    m8 = [m2[s] + _mm(pm[s][:, :P2], m2[s]) for s in subs]
    chain_steps(per_stage)
    o16 = [jnp.where(in16, x, 0.0) for x in l_ab]
    om = [_mm(o16[s], m8[s]) for s in subs]
    m16 = [m8[s] + _mm(m8[s], om[s]) for s in subs]
    chain_steps(per_stage)
    o32 = [jnp.where(in32, x, 0.0) for x in l_ab]
    om = [_mm(o32[s], m16[s]) for s in subs]
    m32 = [(m16[s] + _mm(m16[s], om[s])).astype(BF16) for s in subs]
    chain_steps(per_stage)
    o64 = [jnp.where(in64, x, 0.0).astype(BF16) for x in l_ab]
    y1 = [jnp.dot(m32[s], jnp.concatenate([a_st[s], x_loc[s].astype(BF16)], axis=1),
                  preferred_element_type=F32) for s in subs]
    oy = [_mm(o64[s], y1[s]).astype(BF16) for s in subs]
    chain_steps(per_stage)
    wu16 = [(y1[s] + jnp.dot(m32[s], oy[s], preferred_element_type=F32)).astype(BF16)
            for s in subs]
    qy = [jnp.dot(a_rb[s], wu16[s], preferred_element_type=F32) for s in subs]
    chain_steps(per_stage)
    gh = [jnp.dot(bh_t[s], wu16[s], preferred_element_type=F32) for s in subs]
    chain_steps(nsub)
    t_scr[...] = chain["t"]
    for s in subs:
        q_scr[s] = (r_st[s] + qy[s][:, :LANES]).astype(BF16)
        yl_scr[s] = qy[s][:, LANES:] + y_rkv[s]
        g_scr[s] = (jnp.where(eye, jnp.exp(tots[s]), 0.0) + gh[s][:, :LANES]).astype(BF16)
        h_scr[s] = gh[s][:, LANES:] + jnp.dot(kh_t[s], v_st[s], preferred_element_type=F32)


def _rwkv_scan(r, v, kk, lw, bb, kd, nsub):
    B, S, W = r.shape
    n_pair = W // LANES
    T = nsub * CHUNK
    n_step = S // T

    n_total = B * 2 * n_pair * n_step

    def locate(t):
        g = t // n_step
        c = t % n_step
        d = (g // n_pair) % 2
        return d, g // (2 * n_pair), c + d * (n_step - 1 - 2 * c), g % n_pair

    def shared(t):
        _, b, blk, p = locate(jnp.minimum(t, n_total - 1))
        return (b, blk, p)

    def per_dir(t):
        return locate(jnp.minimum(t, n_total - 1))

    def out_map(t):
        return locate(jnp.maximum(t - 1, 0))

    mat = pltpu.VMEM((nsub, LANES, LANES), F32)
    mat16 = pltpu.VMEM((nsub, LANES, LANES), BF16)
    return pl.pallas_call(
        functools.partial(_scan_kernel, nsub=nsub, n_pair=n_pair, n_step=n_step),
        grid=(n_total + 1,),
        in_specs=[pl.BlockSpec((None, T, LANES), shared)] * 3
                 + [pl.BlockSpec((None, None, T, LANES), per_dir)] * 3,
        out_specs=pl.BlockSpec((None, None, T, LANES), out_map),
        out_shape=jax.ShapeDtypeStruct((2, B, S, W), F32),
        scratch_shapes=[pltpu.VMEM((LANES, LANES), F32), mat16, mat, mat16, mat],
        compiler_params=pltpu.CompilerParams(
            dimension_semantics=("arbitrary",), vmem_limit_bytes=VMEM_LIMIT),
        name="rwkv_scan",
    )(r, v, kk, lw, bb, kd)


def _attn_kernel(q_ref, k_ref, vt_ref, lq1_ref, lk1_ref, lq2_ref, lk2_ref, sw_ref, o_ref,
                 m_scr, alpha_scr, acc_scr, s_even, s_odd, p_even, p_odd, mx_even, mx_odd,
                 *, tk, lambda_init):
    n_kv = k_ref.shape[0] // tk
    q = q_ref[...]
    lane = lax.broadcasted_iota(jnp.int32, q.shape, 1)
    zero = jnp.zeros_like(q)
    qc = (jnp.where(lane < HEAD, q, zero), jnp.where(lane < HEAD, zero, q))
    m_scr[...] = jnp.full(m_scr.shape, -jnp.inf, F32)
    acc_scr[...] = jnp.zeros(acc_scr.shape, F32)

    def scores(j, c):
        ks = k_ref[pl.ds(pl.multiple_of(j * tk, tk), tk), :]
        return lax.dot_general(ks, qc[c], (((1,), (1,)), ((), ())),
                               preferred_element_type=F32)

    ones_rows = jnp.ones((SUM_ROWS, tk), BF16)

    def weighted_values(j, p_ref, c):
        vts = vt_ref[:, pl.ds(pl.multiple_of(j * tk, tk), tk)]
        vts = jnp.concatenate([vts, ones_rows], axis=0)
        acc_scr[c] = alpha_scr[c] * acc_scr[c] + jnp.dot(vts, p_ref[c],
                                                         preferred_element_type=F32)

    def put_scores(j, s_ref, mx_ref, c):
        s = scores(j, c)
        s_ref[c] = s
        mx_ref[c] = jnp.max(s, axis=0, keepdims=True)

    def tile_step(j, s_cur, mx_cur, s_nxt, mx_nxt, p_cur, p_prev, first=False, last=False):
        for c in range(2):
            if not first:
                weighted_values(j - 1, p_prev, c)
            if not last:
                put_scores(j + 1, s_nxt, mx_nxt, c)
        for c in range(2):
            m_old = m_scr[c]
            m_new = jnp.maximum(m_old, mx_cur[c])
            alpha = jnp.exp2(m_old - m_new)
            p = jnp.exp2(s_cur[c] - m_new)
            p_cur[c] = p.astype(BF16)
            alpha_scr[c] = alpha
            m_scr[c] = m_new

    for c in range(2):
        put_scores(0, s_even, mx_even, c)

    unroll = TILES_PER_ITER if n_kv % TILES_PER_ITER == 0 else 2

    def steps(base, head=False, tail=False):
        for t in range(0, unroll, 2):
            tile_step(base + t, s_even, mx_even, s_odd, mx_odd, p_even, p_odd,
                      first=head and t == 0)
            tile_step(base + t + 1, s_odd, mx_odd, s_even, mx_even, p_odd, p_even,
                      last=tail and t + 2 == unroll)

    def body(i, carry):
        steps(unroll * i)
        return carry

    n_groups = n_kv // unroll
    if n_groups == 1:
        steps(0, head=True, tail=True)
    else:
        steps(0, head=True)
        lax.fori_loop(1, n_groups - 1, body, 0)
        steps(n_kv - unroll, tail=True)
    for c in range(2):
        weighted_values(n_kv - 1, p_odd, c)

    lam = (jnp.exp(jnp.sum(lq1_ref[...] * lk1_ref[...], axis=-1, keepdims=True))
           - jnp.exp(jnp.sum(lq2_ref[...] * lk2_ref[...], axis=-1, keepdims=True)) + lambda_init)
    dv = vt_ref.shape[0]
    o = (acc_scr[0, :dv, :] / acc_scr[0, dv:dv + 1, :]
         - lam * (acc_scr[1, :dv, :] / acc_scr[1, dv:dv + 1, :]))
    o = o * lax.rsqrt(jnp.mean(o * o, axis=0, keepdims=True) + SUBLN_EPS)
    o = o * (sw_ref[...] * (1.0 - lambda_init))
    o_ref[...] = o.T.astype(o_ref.dtype)


def _diff_attn(q, k, vt, lq1, lk1, lq2, lk2, subln_w, lambda_init, tq, tk):
    B, S, QW = q.shape
    H = QW // LANES
    DV = vt.shape[1] // H
    const = lambda b, h, i: (0, 0)
    return pl.pallas_call(
        functools.partial(_attn_kernel, tk=tk, lambda_init=lambda_init),
        grid=(B, H, S // tq),
        in_specs=[
            pl.BlockSpec((None, tq, LANES), lambda b, h, i: (b, i, h)),
            pl.BlockSpec((None, S, LANES), lambda b, h, i: (b, 0, h)),
            pl.BlockSpec((None, DV, S), lambda b, h, i: (b, h, 0)),
            pl.BlockSpec((1, HEAD), const),
            pl.BlockSpec((1, HEAD), const),
            pl.BlockSpec((1, HEAD), const),
            pl.BlockSpec((1, HEAD), const),
            pl.BlockSpec((DV, 1), const),
        ],
        out_specs=pl.BlockSpec((None, tq, DV), lambda b, h, i: (b, i, h)),
        out_shape=jax.ShapeDtypeStruct((B, S, H * DV), BF16),
        scratch_shapes=[pltpu.VMEM((2, 1, tq), F32),
                        pltpu.VMEM((2, 1, tq), F32),
                        pltpu.VMEM((2, DV + SUM_ROWS, tq), F32),
                        pltpu.VMEM((2, tk, tq), F32),
                        pltpu.VMEM((2, tk, tq), F32),
                        pltpu.VMEM((2, tk, tq), BF16),
                        pltpu.VMEM((2, tk, tq), BF16),
                        pltpu.VMEM((2, 1, tq), F32),
                        pltpu.VMEM((2, 1, tq), F32)],
        compiler_params=pltpu.CompilerParams(
            dimension_semantics=("parallel", "parallel", "arbitrary"),
            vmem_limit_bytes=VMEM_LIMIT),
        name="diff_attn",
    )(q, k, vt, lq1, lk1, lq2, lk2, subln_w)


def _merge_kernel(x_ref, y_ref, bonus_ref, g_ref, ob_ref, nm_ref, wg_ref, lnw_ref, lnb_ref,
                  e_ref, woa_ref, wob_ref, wout_ref, o_ref):
    x = x_ref[...]
    D = x.shape[-1]
    h = _rms(x, nm_ref[...], RMS_EPS).astype(BF16)
    gates = _sigmoid(jnp.dot(h, wg_ref[...], preferred_element_type=F32))
    e = e_ref[...]
    y = y_ref[0] + y_ref[1]
    mean = _mm_exact_rhs(y, e) * (1.0 / HEAD)
    yc = y - mean
    var = _mm_exact_rhs(yc * yc, e) * (1.0 / HEAD)
    yn = yc * lax.rsqrt(var + GN_EPS) * lnw_ref[...] + lnb_ref[...]
    ya = _mm((yn + bonus_ref[...]) * g_ref[...], woa_ref[...])
    yb = jnp.dot(ob_ref[...], wob_ref[...], preferred_element_type=F32)
    merged = gates[:, :D] * ya + gates[:, D:] * yb
    o_ref[...] = x + _mm(merged, wout_ref[...])


def _merge(x, y, bonus, g, ob, norm_mix, w_g, ln_w, ln_b, e_seg, w_oa, w_ob, w_out, tm):
    B, S, D = x.shape
    W = bonus.shape[-1]
    const = lambda b, i: (0, 0)
    row = lambda b, i: (b, i, 0)
    return pl.pallas_call(
        _merge_kernel,
        grid=(B, S // tm),
        in_specs=[
            pl.BlockSpec((None, tm, D), row),
            pl.BlockSpec((2, None, tm, W), lambda b, i: (0, b, i, 0)),
            pl.BlockSpec((None, tm, W), row),
            pl.BlockSpec((None, tm, W), row),
            pl.BlockSpec((None, tm, W), row),
            pl.BlockSpec((1, D), const),
            pl.BlockSpec((D, 2 * D), const),
            pl.BlockSpec((1, W), const),
            pl.BlockSpec((1, W), const),
            pl.BlockSpec((W, W), const),
            pl.BlockSpec((W, D), const),
            pl.BlockSpec((W, D), const),
            pl.BlockSpec((D, D), const),
        ],
        out_specs=pl.BlockSpec((None, tm, D), row),
        out_shape=jax.ShapeDtypeStruct((B, S, D), F32),
        compiler_params=pltpu.CompilerParams(
            dimension_semantics=("parallel", "parallel"), vmem_limit_bytes=VMEM_LIMIT),
        name="merge",
    )(x, y, bonus, g, ob, norm_mix, w_g, ln_w, ln_b, e_seg, w_oa, w_ob, w_out)


def _ffn_kernel(x_ref, p_ref, nf_ref, w1_ref, w2_ref, np_ref, wpg_ref, wpp_ref, nfin_ref, o_ref,
                *, n_chunk):
    x = x_ref[...]
    h = _rms(x, nf_ref[...], RMS_EPS).astype(BF16)
    ff = w1_ref.shape[1] // n_chunk
    acc = x
    for c in range(n_chunk):
        t = jnp.maximum(jnp.dot(h, w1_ref[:, c * ff:(c + 1) * ff], preferred_element_type=F32), 0.0)
        acc = acc + jnp.dot((t * t).astype(BF16), w2_ref[c * ff:(c + 1) * ff, :],
                            preferred_element_type=F32)
    x = acc
    h = _rms(x, np_ref[...], RMS_EPS).astype(BF16)
    gate = _sigmoid(jnp.dot(h, wpg_ref[...], preferred_element_type=F32))
    x = x + gate * _mm(p_ref[...], wpp_ref[...])
    o_ref[...] = x
    if nfin_ref is not None:
        o_ref[...] = _rms(x, nfin_ref[...], RMS_EPS)


def _ffn_kernel_plain(x_ref, p_ref, nf_ref, w1_ref, w2_ref, np_ref, wpg_ref, wpp_ref, o_ref, *, n_chunk):
    _ffn_kernel(x_ref, p_ref, nf_ref, w1_ref, w2_ref, np_ref, wpg_ref, wpp_ref, None, o_ref,
                n_chunk=n_chunk)


def _ffn(x, p, norm_ffn, w1, w2, norm_ple, w_pg, w_pp, norm_final, tm, n_chunk):
    B, S, D = x.shape
    PD = p.shape[-1]
    FF = w1.shape[1]
    const = lambda b, i: (0, 0)
    row = lambda b, i: (b, i, 0)
    in_specs = [
        pl.BlockSpec((None, tm, D), row),
        pl.BlockSpec((None, tm, PD), row),
        pl.BlockSpec((1, D), const),
        pl.BlockSpec((D, FF), const, pipeline_mode=pl.Buffered(1)),
        pl.BlockSpec((FF, D), const, pipeline_mode=pl.Buffered(1)),
        pl.BlockSpec((1, D), const),
        pl.BlockSpec((D, D), const, pipeline_mode=pl.Buffered(1)),
        pl.BlockSpec((PD, D), const, pipeline_mode=pl.Buffered(1)),
    ]
    args = [x, p, norm_ffn, w1, w2, norm_ple, w_pg, w_pp]
    if norm_final is not None:
        in_specs.append(pl.BlockSpec((1, D), const))
        args.append(norm_final)
        body = functools.partial(_ffn_kernel, n_chunk=n_chunk)
    else:
        body = functools.partial(_ffn_kernel_plain, n_chunk=n_chunk)
    return pl.pallas_call(
        body,
        grid=(B, S // tm),
        in_specs=in_specs,
        out_specs=pl.BlockSpec((None, tm, D), row),
        out_shape=jax.ShapeDtypeStruct((B, S, D), F32),
        compiler_params=pltpu.CompilerParams(
            dimension_semantics=("parallel", "parallel"), vmem_limit_bytes=VMEM_LIMIT),
        name="ffn",
    )(*args)


def _block_diag2(w):
    z = jnp.zeros_like(w[0])
    return jnp.concatenate([jnp.concatenate([w[0], z], axis=1),
                            jnp.concatenate([z, w[1]], axis=1)], axis=0)


def kernel(x, p, norm_mix, w_in, shift_mu_prev, shift_mu_next, rwkv_w0, rwkv_w2, rwkv_a0,
           rwkv_a2, rwkv_g2, rwkv_k_k, rwkv_k_a, rwkv_r_k, rwkv_ln_w, rwkv_ln_b, rwkv_w_o,
           da_lq1, da_lk1, da_lq2, da_lk2, da_subln_w, da_w_o, w_out, norm_ffn, w_ff1, w_ff2,
           norm_ple, w_ple_gate, w_ple_proj, norm_final):
    B, S, D = x.shape
    L = w_in.shape[0]
    W = rwkv_w0.shape[-1]
    n_r = shift_mu_prev.shape[-1]
    n_qk = 2 * da_w_o.shape[1]
    n_v = da_w_o.shape[1]
    cos_t, sin_t = _rope_tables(S)
    seg = jnp.arange(W) // HEAD
    e_seg = (seg[:, None] == seg[None, :]).astype(BF16)
    tm = min(256, S)
    for i in range(L):
        lambda_init = 0.8 - 0.6 * math.exp(-0.3 * i)
        w_i = w_in[i].astype(BF16)
        u_r, q, k, v = _in_proj(
            x, norm_mix[i][None], w_i[:, :n_r], w_i[:, n_r:n_r + n_qk],
            w_i[:, n_r + n_qk:n_r + n_qk + n_v].T, cos_t, sin_t, tm)
        r, vv, kk, g, bonus, lw, bb, kd = _rwkv_prep(
            u_r, shift_mu_prev[i][None], shift_mu_next[i][None],
            rwkv_w0[i].reshape(1, 2 * W), _block_diag2(rwkv_w2[i]).astype(BF16),
            rwkv_a0[i].reshape(1, 2 * W), _block_diag2(rwkv_a2[i]).astype(BF16),
            rwkv_g2[i].astype(BF16), rwkv_k_k[i][None], rwkv_k_a[i][None],
            rwkv_r_k[i].reshape(1, W), e_seg, tm)
        y = _rwkv_scan(r, vv, kk, lw, bb, kd, nsub=min(8, S // CHUNK))
        ob = _diff_attn(q, k, v, da_lq1[i][None], da_lk1[i][None], da_lq2[i][None],
                        da_lk2[i][None], da_subln_w[i][:, None], lambda_init,
                        tq=min(512, S), tk=min(512, S // 2))
        x = _merge(x, y, bonus, g, ob, norm_mix[i][None],
                   w_i[:, n_r + n_qk + n_v:], rwkv_ln_w[i][None], rwkv_ln_b[i][None], e_seg,
                   rwkv_w_o[i].astype(BF16), da_w_o[i].astype(BF16), w_out[i].astype(BF16), tm)
        x = _ffn(x, p[i], norm_ffn[i][None], w_ff1[i].astype(BF16), w_ff2[i].astype(BF16),
                 norm_ple[i][None], w_ple_gate[i].astype(BF16), w_ple_proj[i].astype(BF16),
                 norm_final[None] if i == L - 1 else None, tm, n_chunk=4)
    return x
```

```python
import functools
import math

import jax
import jax.numpy as jnp
from jax import lax
from jax.experimental import pallas as pl
from jax.experimental.pallas import tpu as pltpu

F32 = jnp.float32
BF16 = jnp.bfloat16

HEAD = 64
LANES = 128
CHUNK = 64
RMS_EPS = 1e-6
GN_EPS = 64e-5
SUBLN_EPS = 1e-5
ROPE_THETA = 500000.0
ROPE_DIM = 16
Q_SCALE = HEAD ** -0.5 * math.log2(math.e)
DECAY_SCALE = math.exp(-0.5)
VMEM_LIMIT = 56 * 1024 * 1024
TILES_PER_ITER = 4
SUM_ROWS = 16
HALO = 8
SEG_BLOCK = 256


def _rms(x, g, eps):
    return x * lax.rsqrt(jnp.mean(x * x, axis=-1, keepdims=True) + eps) * g


def _sigmoid(x):
    return 1.0 / (1.0 + jnp.exp(-x))


def _mm(a, b):
    return jnp.dot(a.astype(BF16), b.astype(BF16), preferred_element_type=F32)


def _mm_nt(a, b):
    return lax.dot_general(a.astype(BF16), b.astype(BF16), (((1,), (1,)), ((), ())),
                           preferred_element_type=F32)


def _split2(x):
    hi = x.astype(BF16)
    lo = (x - hi.astype(F32)).astype(BF16)
    return hi, lo


def _mm_exact_rhs(a, b_bf16):
    hi, lo = _split2(a)
    return (jnp.dot(hi, b_bf16, preferred_element_type=F32)
            + jnp.dot(lo, b_bf16, preferred_element_type=F32))


def _seg_sum(x, e):
    n = e.shape[0]
    return jnp.concatenate([_mm_exact_rhs(x[:, c:c + n], e) for c in range(0, x.shape[1], n)],
                           axis=1)


def _mm3(a, b):
    ah, al = _split2(a)
    bh, bl = _split2(b)
    return (jnp.dot(ah, bh, preferred_element_type=F32)
            + jnp.dot(ah, bl, preferred_element_type=F32)
            + jnp.dot(al, bh, preferred_element_type=F32))


def _inproj_kernel(x_ref, g_ref, wr_ref, wqk_ref, wv_ref, cos_ref, sin_ref,
                   ur_ref, q_ref, k_ref, v_ref):
    h = _rms(x_ref[...], g_ref[...], RMS_EPS).astype(BF16)
    ur_ref[...] = jnp.dot(h, wr_ref[...], preferred_element_type=F32)
    v_ref[...] = lax.dot_general(wv_ref[...], h, (((1,), (1,)), ((), ())),
                                 preferred_element_type=F32).astype(BF16)
    cos_t = cos_ref[...]
    sin_t = sin_ref[...]
    lane = lax.broadcasted_iota(jnp.int32, cos_t.shape, 1)
    first_half = (lane & (HEAD - 1)) < (ROPE_DIM // 2)
    n_tiles = wqk_ref.shape[1] // LANES
    qk = jnp.dot(h, wqk_ref[...], preferred_element_type=F32)
    for c in range(n_tiles):
        xq = qk[:, c * LANES:(c + 1) * LANES]
        partner = jnp.where(first_half,
                            pltpu.roll(xq, LANES - ROPE_DIM // 2, 1),
                            pltpu.roll(xq, ROPE_DIM // 2, 1))
        ro = xq * cos_t + partner * sin_t
        if c < n_tiles // 2:
            q_ref[:, c * LANES:(c + 1) * LANES] = (ro * Q_SCALE).astype(BF16)
        else:
            cc = c - n_tiles // 2
            k_ref[:, cc * LANES:(cc + 1) * LANES] = ro.astype(BF16)


def _rope_tables(S):
    pos = jnp.arange(S, dtype=F32)
    inv_freq = ROPE_THETA ** (-jnp.arange(0, ROPE_DIM, 2, dtype=F32) / ROPE_DIM)
    ang = pos[:, None] * inv_freq[None, :]
    cos8, sin8 = jnp.cos(ang), jnp.sin(ang)
    ones = jnp.ones((S, HEAD - ROPE_DIM), F32)
    zeros = jnp.zeros((S, HEAD - ROPE_DIM), F32)
    cos64 = jnp.concatenate([cos8, cos8, ones], axis=1)
    sin64 = jnp.concatenate([-sin8, sin8, zeros], axis=1)
    return jnp.tile(cos64, (1, 2)), jnp.tile(sin64, (1, 2))


def _in_proj(x, g, w_r, w_qk, w_vt, cos_t, sin_t, tm):
    B, S, D = x.shape
    nr, nqk, nv = w_r.shape[1], w_qk.shape[1], w_vt.shape[0]
    const = lambda b, i: (0, 0)
    row = lambda b, i: (b, i, 0)
    return pl.pallas_call(
        _inproj_kernel,
        grid=(B, S // tm),
        in_specs=[
            pl.BlockSpec((None, tm, D), row),
            pl.BlockSpec((1, D), const),
            pl.BlockSpec((D, nr), const),
            pl.BlockSpec((D, nqk), const),
            pl.BlockSpec((nv, D), const),
            pl.BlockSpec((tm, LANES), lambda b, i: (i, 0)),
            pl.BlockSpec((tm, LANES), lambda b, i: (i, 0)),
        ],
        out_specs=[
            pl.BlockSpec((None, tm, nr), row),
            pl.BlockSpec((None, tm, nqk // 2), row),
            pl.BlockSpec((None, tm, nqk // 2), row),
            pl.BlockSpec((None, nv, tm), lambda b, i: (b, 0, i)),
        ],
        out_shape=[
            jax.ShapeDtypeStruct((B, S, nr), F32),
            jax.ShapeDtypeStruct((B, S, nqk // 2), BF16),
            jax.ShapeDtypeStruct((B, S, nqk // 2), BF16),
            jax.ShapeDtypeStruct((B, nv, S), BF16),
        ],
        compiler_params=pltpu.CompilerParams(
            dimension_semantics=("parallel", "parallel"), vmem_limit_bytes=VMEM_LIMIT),
        name="in_proj",
    )(x, g, w_r, w_qk, w_vt, cos_t, sin_t)


def _prep_kernel(u_ref, up_ref, un_ref, mup_ref, mun_ref, w0_ref, w2_ref, a0_ref, a2_ref,
                 g2_ref, kk_w_ref, ka_ref, rk_ref, e_ref,
                 r_ref, v_ref, kk_ref, g_ref, bonus_ref, lw_ref, b_ref, kd_ref):
    i = pl.program_id(1)
    n = pl.num_programs(1)
    u = u_ref[...]
    tm = u.shape[0]
    W = r_ref.shape[-1]
    row8 = lax.broadcasted_iota(jnp.int32, (HALO, 1), 0)
    prev_row = jnp.where(i > 0, up_ref[HALO - 1:HALO, :], 0.0)
    next_row = jnp.where(i < n - 1, un_ref[0:1, :], 0.0)
    rolled = pltpu.roll(u, 1, 0)
    u_prev = jnp.concatenate([jnp.where(row8 == 0, prev_row, rolled[:HALO]), rolled[HALO:]], axis=0)
    rolled = pltpu.roll(u, tm - 1, 0)
    u_next = jnp.concatenate([rolled[:tm - HALO],
                              jnp.where(row8 == HALO - 1, next_row, rolled[tm - HALO:])], axis=0)
    mup = mup_ref[...]
    mun = mun_ref[...]
    us = (1.0 - mup - mun) * u + mup * u_prev + mun * u_next

    r = us[:, 0:W]
    k = us[:, W:2 * W]
    v = us[:, 2 * W:3 * W]
    wd = us[:, 3 * W:3 * W + LANES]
    ad = us[:, 3 * W + LANES:3 * W + 2 * LANES]
    gd = us[:, 3 * W + 2 * LANES:3 * W + 3 * LANES]

    z = w0_ref[...] + _mm(jnp.tanh(wd), w2_ref[...])
    lw = -(_sigmoid(z) * DECAY_SCALE)
    a = _sigmoid(a0_ref[...] + _mm(ad, a2_ref[...]))
    g_ref[...] = _mm(_sigmoid(gd), g2_ref[...]).astype(g_ref.dtype)

    e = e_ref[...]
    kraw = k * kk_w_ref[...]
    ss = _seg_sum(kraw * kraw, e)
    kk = kraw / jnp.maximum(jnp.sqrt(ss), 1e-12)
    ka = ka_ref[...]
    kd_sum = jnp.zeros_like(k)
    for d in range(2):
        a_d = a[:, d * W:(d + 1) * W]
        kd = k * (1.0 + (a_d - 1.0) * ka)
        kd_sum = kd_sum + kd
        lw_ref[d] = lw[:, d * W:(d + 1) * W]
        b_ref[d] = kk * a_d
        kd_ref[d] = kd.astype(kd_ref.dtype)
    c = _seg_sum(r * rk_ref[...] * kd_sum, e)
    r_ref[...] = r.astype(r_ref.dtype)
    v_ref[...] = v.astype(v_ref.dtype)
    kk_ref[...] = kk
    bonus_ref[...] = (c * v).astype(bonus_ref.dtype)


def _rwkv_prep(u_r, mu_prev, mu_next, w0, w2bd, a0, a2bd, g2, k_k, k_a, r_k, e_seg, tm):
    B, S, C = u_r.shape
    W = k_k.shape[-1]
    nblk8 = S // 8
    const = lambda b, i: (0, 0)
    row = lambda b, i: (b, i, 0)
    drow = lambda b, i: (0, b, i, 0)
    tok = jax.ShapeDtypeStruct((B, S, W), F32)
    tok16 = jax.ShapeDtypeStruct((B, S, W), BF16)
    dtok = jax.ShapeDtypeStruct((2, B, S, W), F32)
    dtok16 = jax.ShapeDtypeStruct((2, B, S, W), BF16)
    return pl.pallas_call(
        _prep_kernel,
        grid=(B, S // tm),
        in_specs=[
            pl.BlockSpec((None, tm, C), row),
            pl.BlockSpec((None, 8, C), lambda b, i: (b, jnp.maximum(i * (tm // 8) - 1, 0), 0)),
            pl.BlockSpec((None, 8, C), lambda b, i: (b, jnp.minimum((i + 1) * (tm // 8), nblk8 - 1), 0)),
            pl.BlockSpec((1, C), const),
            pl.BlockSpec((1, C), const),
            pl.BlockSpec((1, 2 * W), const),
            pl.BlockSpec((LANES, 2 * W), const),
            pl.BlockSpec((1, 2 * W), const),
            pl.BlockSpec((LANES, 2 * W), const),
            pl.BlockSpec((LANES, W), const),
            pl.BlockSpec((1, W), const),
            pl.BlockSpec((1, W), const),
            pl.BlockSpec((1, W), const),
            pl.BlockSpec(e_seg.shape, const),
        ],
        out_specs=[pl.BlockSpec((None, tm, W), row)] * 5
                  + [pl.BlockSpec((2, None, tm, W), drow)] * 3,
        out_shape=[tok16, tok16, tok, tok16, tok16, dtok, dtok, dtok16],
        compiler_params=pltpu.CompilerParams(
            dimension_semantics=("parallel", "parallel"), vmem_limit_bytes=VMEM_LIMIT),
        name="rwkv_prep",
    )(u_r, u_r, u_r, mu_prev, mu_next, w0, w2bd, a0, a2bd, g2, k_k, k_a, r_k, e_seg)


def _scan_kernel(r_ref, v_ref, kk_ref, lw_ref, b_ref, kd_ref, y_ref,
                 t_scr, g_scr, h_scr, q_scr, yl_scr, *, nsub, n_pair, n_step):
    step = pl.program_id(0)
    last = pl.num_programs(0) - 2
    row_cur = jnp.minimum(step, last) // n_step
    step_prev = jnp.maximum(step - 1, 0)
    rev = (row_cur // n_pair) % 2
    rev_prev = ((step_prev // n_step) // n_pair) % 2
    sgn = 1 - 2 * rev
    C = CHUNK
    P2 = 2 * C

    @pl.when(step == 0)
    def _():
        t_scr[...] = jnp.zeros_like(t_scr)
        g_scr[...] = jnp.zeros_like(g_scr)
        h_scr[...] = jnp.zeros_like(h_scr)
        q_scr[...] = jnp.zeros_like(q_scr)
        yl_scr[...] = jnp.zeros_like(yl_scr)

    def slices(direction):
        out = []
        for s in range(nsub):
            ci = s + direction * (nsub - 1 - 2 * s)
            out.append(pl.ds(pl.multiple_of(ci * C, C), C))
        return out

    chunk_slices = slices(rev)
    chain_slices = slices(rev_prev)

    chain = {"t": jnp.where(step_prev % n_step == 0, 0.0, t_scr[...]), "next": 0}

    def chain_steps(n):
        for _ in range(n):
            s = chain["next"]
            if s == nsub:
                return
            t_cur = chain["t"]
            t16 = t_cur.astype(BF16)
            y_st = jnp.dot(q_scr[s], t16, preferred_element_type=F32) + yl_scr[s]
            chain["t"] = jnp.dot(g_scr[s], t16, preferred_element_type=F32) + h_scr[s]
            y_ref[chain_slices[s], :] = y_st[:C, :] + y_st[C:, :]
            chain["next"] = s + 1

    row = lax.broadcasted_iota(jnp.int32, (P2, P2), 0)
    col = lax.broadcasted_iota(jnp.int32, (P2, P2), 1)
    rt = row & (C - 1)
    ct = col & (C - 1)
    delta = (rt - ct) * sgn
    strict = delta > 0
    incl = delta >= 0
    eye = row == col
    ident = jnp.where(eye, 1.0, 0.0)
    blk8 = (rt >> 3) == (ct >> 3)
    blk16 = (rt >> 4) == (ct >> 4)
    blk32 = (rt >> 5) == (ct >> 5)
    in8 = strict & blk8
    in16 = strict & blk16 & ~blk8
    in32 = strict & blk32 & ~blk16
    in64 = strict & ~blk32
    lane = lax.broadcasted_iota(jnp.int32, (C, LANES), 1)
    head0 = lane < HEAD

    def stack(x):
        return jnp.concatenate([jnp.where(head0, x, 0.0), jnp.where(head0, 0.0, x)], axis=0)

    subs = range(nsub)
    sls = chunk_slices

    lws = [lw_ref[sl, :] for sl in sls]
    tots = [jnp.sum(lw, axis=0, keepdims=True) for lw in lws]
    tok = lax.broadcasted_iota(jnp.int32, (C, LANES), 0)
    cums = []
    for lw, tot in zip(lws, tots):
        pre = lw
        shift = 1
        while shift < C:
            pre = pre + jnp.where(tok >= shift, pltpu.roll(pre, shift, 0), 0.0)
            shift *= 2
        cums.append(jnp.where(rev == 1, tot - pre + lw, pre))
    a_st, r_st, b_st, k_st, bh_t, kh_t, v_st = [], [], [], [], [], [], []
    for s in subs:
        sl, lw, cum, tot = sls[s], lws[s], cums[s], tots[s]
        e_neg = jnp.exp(-cum)
        e_rem = jnp.exp(tot - cum)
        b = b_ref[sl, :]
        kd = kd_ref[sl, :]
        a_st.append(stack(kk_ref[sl, :] * -jnp.exp(cum - lw)).astype(BF16))
        r_st.append(stack(r_ref[sl, :] * jnp.exp(cum)))
        b_st.append(stack(b * e_neg).astype(BF16))
        k_st.append(stack(kd * e_neg).astype(BF16))
        bh_t.append(stack(b * e_rem).T.astype(BF16))
        kh_t.append(stack(kd * e_rem).T.astype(BF16))
        v_st.append(stack(v_ref[sl, :]).astype(BF16))

    nt = (((1,), (1,)), ((), ()))
    sc = [lax.dot_general(jnp.concatenate([a_st[s], r_st[s].astype(BF16)], axis=0),
                          jnp.concatenate([b_st[s], k_st[s]], axis=0), nt,
                          preferred_element_type=F32) for s in subs]
    per_stage = -(-nsub // 8)
    chain_steps(per_stage)
    l_ab = [x[:P2, :P2] for x in sc]
    l_ak = [jnp.where(strict, x[:P2, P2:], 0.0).astype(BF16) for x in sc]
    a_rb = [jnp.where(incl, x[P2:, :P2], 0.0).astype(BF16) for x in sc]
    a_rk = [jnp.where(incl, x[P2:, P2:], 0.0).astype(BF16) for x in sc]
    x_loc = [jnp.dot(l_ak[s], v_st[s], preferred_element_type=F32) for s in subs]
    chain_steps(per_stage)

    d8 = [jnp.where(in8, x, 0.0) for x in l_ab]
    m1 = [ident + x for x in d8]
    p1 = [_mm(x, x) for x in d8]
    chain_steps(per_stage)
    pm = [_mm(p1[s], jnp.concatenate([p1[s], m1[s]], axis=1)) for s in subs]
    m2 = [m1[s] + pm[s][:, P2:] for s in subs]
    m8 = [m2[s] + _mm(pm[s][:, :P2], m2[s]) for s in subs]
    chain_steps(per_stage)
    o16 = [jnp.where(in16, x, 0.0) for x in l_ab]
    om = [_mm(o16[s], m8[s]) for s in subs]
    m16 = [m8[s] + _mm(m8[s], om[s]) for s in subs]
    chain_steps(per_stage)
    o32 = [jnp.where(in32, x, 0.0) for x in l_ab]
    om = [_mm(o32[s], m16[s]) for s in subs]
    m32 = [(m16[s] + _mm(m16[s], om[s])).astype(BF16) for s in subs]
    chain_steps(per_stage)
    o64 = [jnp.where(in64, x, 0.0).astype(BF16) for x in l_ab]
    y1 = [jnp.dot(m32[s], jnp.concatenate([a_st[s], x_loc[s].astype(BF16)], axis=1),
                  preferred_element_type=F32) for s in subs]
    oy = [_mm(o64[s], y1[s]).astype(BF16) for s in subs]
    chain_steps(per_stage)
    wu16 = [(y1[s] + jnp.dot(m32[s], oy[s], preferred_element_type=F32)).astype(BF16)
            for s in subs]
    chain_steps(per_stage)
    zero_blk = jnp.zeros((P2, LANES), BF16)
    fin = [jnp.dot(jnp.concatenate([jnp.concatenate([a_rb[s], a_rk[s]], axis=1),
                                    jnp.concatenate([bh_t[s], kh_t[s]], axis=1)], axis=0),
                   jnp.concatenate([wu16[s], jnp.concatenate([zero_blk, v_st[s]], axis=1)], axis=0),
                   preferred_element_type=F32) for s in subs]
    chain_steps(nsub)
    t_scr[...] = chain["t"]
    for s in subs:
        q_scr[s] = (r_st[s] + fin[s][:P2, :LANES]).astype(BF16)
        yl_scr[s] = fin[s][:P2, LANES:]
        g_scr[s] = (jnp.where(eye, jnp.exp(tots[s]), 0.0) + fin[s][P2:, :LANES]).astype(BF16)
        h_scr[s] = fin[s][P2:, LANES:]


def _rwkv_scan(r, v, kk, lw, bb, kd, nsub):
    B, S, W = r.shape
    n_pair = W // LANES
    T = nsub * CHUNK
    n_step = S // T

    n_total = B * 2 * n_pair * n_step

    def locate(t):
        g = t // n_step
        c = t % n_step
        d = (g // n_pair) % 2
        return d, g // (2 * n_pair), c + d * (n_step - 1 - 2 * c), g % n_pair

    def shared(t):
        _, b, blk, p = locate(jnp.minimum(t, n_total - 1))
        return (b, blk, p)

    def per_dir(t):
        return locate(jnp.minimum(t, n_total - 1))

    def out_map(t):
        return locate(jnp.maximum(t - 1, 0))

    mat = pltpu.VMEM((nsub, LANES, LANES), F32)
    mat16 = pltpu.VMEM((nsub, LANES, LANES), BF16)
    return pl.pallas_call(
        functools.partial(_scan_kernel, nsub=nsub, n_pair=n_pair, n_step=n_step),
        grid=(n_total + 1,),
        in_specs=[pl.BlockSpec((None, T, LANES), shared)] * 3
                 + [pl.BlockSpec((None, None, T, LANES), per_dir)] * 3,
        out_specs=pl.BlockSpec((None, None, T, LANES), out_map),
        out_shape=jax.ShapeDtypeStruct((2, B, S, W), F32),
        scratch_shapes=[pltpu.VMEM((LANES, LANES), F32), mat16, mat, mat16, mat],
        compiler_params=pltpu.CompilerParams(
            dimension_semantics=("arbitrary",), vmem_limit_bytes=VMEM_LIMIT),
        name="rwkv_scan",
    )(r, v, kk, lw, bb, kd)


def _attn_kernel(q_ref, k_ref, vt_ref, lq1_ref, lk1_ref, lq2_ref, lk2_ref, sw_ref, o_ref,
                 m_scr, alpha_scr, acc_scr, s_even, s_odd, p_even, p_odd, mx_even, mx_odd,
                 *, tk, lambda_init):
    n_kv = k_ref.shape[0] // tk
    q = q_ref[...]
    lane = lax.broadcasted_iota(jnp.int32, q.shape, 1)
    zero = jnp.zeros_like(q)
    qc = (jnp.where(lane < HEAD, q, zero), jnp.where(lane < HEAD, zero, q))
    m_scr[...] = jnp.full(m_scr.shape, -jnp.inf, F32)
    acc_scr[...] = jnp.zeros(acc_scr.shape, F32)

    def scores(j, c):
        ks = k_ref[pl.ds(pl.multiple_of(j * tk, tk), tk), :]
        return lax.dot_general(ks, qc[c], (((1,), (1,)), ((), ())),
                               preferred_element_type=F32)

    ones_rows = jnp.ones((SUM_ROWS, tk), BF16)

    def weighted_values(j, p_ref, c):
        vts = vt_ref[:, pl.ds(pl.multiple_of(j * tk, tk), tk)]
        vts = jnp.concatenate([vts, ones_rows], axis=0)
        acc_scr[c] = alpha_scr[c] * acc_scr[c] + jnp.dot(vts, p_ref[c],
                                                         preferred_element_type=F32)

    def put_scores(j, s_ref, mx_ref, c):
        s = scores(j, c)
        s_ref[c] = s
        mx_ref[c] = jnp.max(s, axis=0, keepdims=True)

    def tile_step(j, s_cur, mx_cur, s_nxt, mx_nxt, p_cur, p_prev, first=False, last=False):
        for c in range(2):
            if not first:
                weighted_values(j - 1, p_prev, c)
            if not last:
                put_scores(j + 1, s_nxt, mx_nxt, c)
        for c in range(2):
            m_old = m_scr[c]
            m_new = jnp.maximum(m_old, mx_cur[c])
            alpha = jnp.exp2(m_old - m_new)
            p = jnp.exp2(s_cur[c] - m_new)
            p_cur[c] = p.astype(BF16)
            alpha_scr[c] = alpha
            m_scr[c] = m_new

    for c in range(2):
        put_scores(0, s_even, mx_even, c)

    unroll = TILES_PER_ITER if n_kv % TILES_PER_ITER == 0 else 2

    def steps(base, head=False, tail=False):
        for t in range(0, unroll, 2):
            tile_step(base + t, s_even, mx_even, s_odd, mx_odd, p_even, p_odd,
                      first=head and t == 0)
            tile_step(base + t + 1, s_odd, mx_odd, s_even, mx_even, p_odd, p_even,
                      last=tail and t + 2 == unroll)

    def body(i, carry):
        steps(unroll * i)
        return carry

    n_groups = n_kv // unroll
    if n_groups == 1:
        steps(0, head=True, tail=True)
    else:
        steps(0, head=True)
        lax.fori_loop(1, n_groups - 1, body, 0)
        steps(n_kv - unroll, tail=True)
    for c in range(2):
        weighted_values(n_kv - 1, p_odd, c)

    lam = (jnp.exp(jnp.sum(lq1_ref[...] * lk1_ref[...], axis=-1, keepdims=True))
           - jnp.exp(jnp.sum(lq2_ref[...] * lk2_ref[...], axis=-1, keepdims=True)) + lambda_init)
    dv = vt_ref.shape[0]
    o = (acc_scr[0, :dv, :] / acc_scr[0, dv:dv + 1, :]
         - lam * (acc_scr[1, :dv, :] / acc_scr[1, dv:dv + 1, :]))
    o = o * lax.rsqrt(jnp.mean(o * o, axis=0, keepdims=True) + SUBLN_EPS)
    o = o * (sw_ref[...] * (1.0 - lambda_init))
    o_ref[...] = o.T.astype(o_ref.dtype)


def _diff_attn(q, k, vt, lq1, lk1, lq2, lk2, subln_w, lambda_init, tq, tk):
    B, S, QW = q.shape
    H = QW // LANES
    DV = vt.shape[1] // H
    const = lambda b, h, i: (0, 0)
    return pl.pallas_call(
        functools.partial(_attn_kernel, tk=tk, lambda_init=lambda_init),
        grid=(B, H, S // tq),
        in_specs=[
            pl.BlockSpec((None, tq, LANES), lambda b, h, i: (b, i, h)),
            pl.BlockSpec((None, S, LANES), lambda b, h, i: (b, 0, h)),
            pl.BlockSpec((None, DV, S), lambda b, h, i: (b, h, 0)),
            pl.BlockSpec((1, HEAD), const),
            pl.BlockSpec((1, HEAD), const),
            pl.BlockSpec((1, HEAD), const),
            pl.BlockSpec((1, HEAD), const),
            pl.BlockSpec((DV, 1), const),
        ],
        out_specs=pl.BlockSpec((None, tq, DV), lambda b, h, i: (b, i, h)),
        out_shape=jax.ShapeDtypeStruct((B, S, H * DV), BF16),
        scratch_shapes=[pltpu.VMEM((2, 1, tq), F32),
                        pltpu.VMEM((2, 1, tq), F32),
                        pltpu.VMEM((2, DV + SUM_ROWS, tq), F32),
                        pltpu.VMEM((2, tk, tq), F32),
                        pltpu.VMEM((2, tk, tq), F32),
                        pltpu.VMEM((2, tk, tq), BF16),
                        pltpu.VMEM((2, tk, tq), BF16),
                        pltpu.VMEM((2, 1, tq), F32),
                        pltpu.VMEM((2, 1, tq), F32)],
        compiler_params=pltpu.CompilerParams(
            dimension_semantics=("parallel", "parallel", "arbitrary"),
            vmem_limit_bytes=VMEM_LIMIT),
        name="diff_attn",
    )(q, k, vt, lq1, lk1, lq2, lk2, subln_w)


def _merge_kernel(x_ref, y_ref, bonus_ref, g_ref, ob_ref, nm_ref, wg_ref, lnw_ref, lnb_ref,
                  e_ref, woa_ref, wob_ref, wout_ref, o_ref):
    x = x_ref[...]
    D = x.shape[-1]
    h = _rms(x, nm_ref[...], RMS_EPS).astype(BF16)
    gates = _sigmoid(jnp.dot(h, wg_ref[...], preferred_element_type=F32))
    e = e_ref[...]
    y = y_ref[0] + y_ref[1]
    mean = _seg_sum(y, e) * (1.0 / HEAD)
    yc = y - mean
    var = _seg_sum(yc * yc, e) * (1.0 / HEAD)
    yn = yc * lax.rsqrt(var + GN_EPS) * lnw_ref[...] + lnb_ref[...]
    ya = _mm((yn + bonus_ref[...]) * g_ref[...], woa_ref[...])
    yb = jnp.dot(ob_ref[...], wob_ref[...], preferred_element_type=F32)
    merged = gates[:, :D] * ya + gates[:, D:] * yb
    o_ref[...] = x + _mm(merged, wout_ref[...])


def _merge(x, y, bonus, g, ob, norm_mix, w_g, ln_w, ln_b, e_seg, w_oa, w_ob, w_out, tm):
    B, S, D = x.shape
    W = bonus.shape[-1]
    const = lambda b, i: (0, 0)
    row = lambda b, i: (b, i, 0)
    return pl.pallas_call(
        _merge_kernel,
        grid=(B, S // tm),
        in_specs=[
            pl.BlockSpec((None, tm, D), row),
            pl.BlockSpec((2, None, tm, W), lambda b, i: (0, b, i, 0)),
            pl.BlockSpec((None, tm, W), row),
            pl.BlockSpec((None, tm, W), row),
            pl.BlockSpec((None, tm, W), row),
            pl.BlockSpec((1, D), const),
            pl.BlockSpec((D, 2 * D), const),
            pl.BlockSpec((1, W), const),
            pl.BlockSpec((1, W), const),
            pl.BlockSpec(e_seg.shape, const),
            pl.BlockSpec((W, D), const),
            pl.BlockSpec((W, D), const),
            pl.BlockSpec((D, D), const),
        ],
        out_specs=pl.BlockSpec((None, tm, D), row),
        out_shape=jax.ShapeDtypeStruct((B, S, D), F32),
        compiler_params=pltpu.CompilerParams(
            dimension_semantics=("parallel", "parallel"), vmem_limit_bytes=VMEM_LIMIT),
        name="merge",
    )(x, y, bonus, g, ob, norm_mix, w_g, ln_w, ln_b, e_seg, w_oa, w_ob, w_out)


def _ffn_kernel(x_ref, p_ref, nf_ref, w1_ref, w2_ref, np_ref, wpg_ref, wpp_ref, nfin_ref, o_ref,
                *, n_chunk):
    x = x_ref[...]
    h = _rms(x, nf_ref[...], RMS_EPS).astype(BF16)
    ff = w1_ref.shape[1] // n_chunk
    acc = x
    for c in range(n_chunk):
        t = jnp.maximum(jnp.dot(h, w1_ref[:, c * ff:(c + 1) * ff], preferred_element_type=F32), 0.0)
        acc = acc + jnp.dot((t * t).astype(BF16), w2_ref[c * ff:(c + 1) * ff, :],
                            preferred_element_type=F32)
    x = acc
    h = _rms(x, np_ref[...], RMS_EPS).astype(BF16)
    gate = _sigmoid(jnp.dot(h, wpg_ref[...], preferred_element_type=F32))
    x = x + gate * _mm(p_ref[...], wpp_ref[...])
    o_ref[...] = x
    if nfin_ref is not None:
        o_ref[...] = _rms(x, nfin_ref[...], RMS_EPS)


def _ffn_kernel_plain(x_ref, p_ref, nf_ref, w1_ref, w2_ref, np_ref, wpg_ref, wpp_ref, o_ref, *, n_chunk):
    _ffn_kernel(x_ref, p_ref, nf_ref, w1_ref, w2_ref, np_ref, wpg_ref, wpp_ref, None, o_ref,
                n_chunk=n_chunk)


def _ffn(x, p, norm_ffn, w1, w2, norm_ple, w_pg, w_pp, norm_final, tm, n_chunk):
    B, S, D = x.shape
    PD = p.shape[-1]
    FF = w1.shape[1]
    const = lambda b, i: (0, 0)
    row = lambda b, i: (b, i, 0)
    in_specs = [
        pl.BlockSpec((None, tm, D), row),
        pl.BlockSpec((None, tm, PD), row),
        pl.BlockSpec((1, D), const),
        pl.BlockSpec((D, FF), const, pipeline_mode=pl.Buffered(1)),
        pl.BlockSpec((FF, D), const, pipeline_mode=pl.Buffered(1)),
        pl.BlockSpec((1, D), const),
        pl.BlockSpec((D, D), const, pipeline_mode=pl.Buffered(1)),
        pl.BlockSpec((PD, D), const, pipeline_mode=pl.Buffered(1)),
    ]
    args = [x, p, norm_ffn, w1, w2, norm_ple, w_pg, w_pp]
    if norm_final is not None:
        in_specs.append(pl.BlockSpec((1, D), const))
        args.append(norm_final)
        body = functools.partial(_ffn_kernel, n_chunk=n_chunk)
    else:
        body = functools.partial(_ffn_kernel_plain, n_chunk=n_chunk)
    return pl.pallas_call(
        body,
        grid=(B, S // tm),
        in_specs=in_specs,
        out_specs=pl.BlockSpec((None, tm, D), row),
        out_shape=jax.ShapeDtypeStruct((B, S, D), F32),
        compiler_params=pltpu.CompilerParams(
            dimension_semantics=("parallel", "parallel"), vmem_limit_bytes=VMEM_LIMIT),
        name="ffn",
    )(*args)


def _block_diag2(w):
    z = jnp.zeros_like(w[0])
    return jnp.concatenate([jnp.concatenate([w[0], z], axis=1),
                            jnp.concatenate([z, w[1]], axis=1)], axis=0)


def kernel(x, p, norm_mix, w_in, shift_mu_prev, shift_mu_next, rwkv_w0, rwkv_w2, rwkv_a0,
           rwkv_a2, rwkv_g2, rwkv_k_k, rwkv_k_a, rwkv_r_k, rwkv_ln_w, rwkv_ln_b, rwkv_w_o,
           da_lq1, da_lk1, da_lq2, da_lk2, da_subln_w, da_w_o, w_out, norm_ffn, w_ff1, w_ff2,
           norm_ple, w_ple_gate, w_ple_proj, norm_final):
    B, S, D = x.shape
    L = w_in.shape[0]
    W = rwkv_w0.shape[-1]
    n_r = shift_mu_prev.shape[-1]
    n_qk = 2 * da_w_o.shape[1]
    n_v = da_w_o.shape[1]
    cos_t, sin_t = _rope_tables(S)
    seg = jnp.arange(SEG_BLOCK) // HEAD
    e_seg = (seg[:, None] == seg[None, :]).astype(BF16)
    tm = min(256, S)
    for i in range(L):
        lambda_init = 0.8 - 0.6 * math.exp(-0.3 * i)
        w_i = w_in[i].astype(BF16)
        u_r, q, k, v = _in_proj(
            x, norm_mix[i][None], w_i[:, :n_r], w_i[:, n_r:n_r + n_qk],
            w_i[:, n_r + n_qk:n_r + n_qk + n_v].T, cos_t, sin_t, tm)
        r, vv, kk, g, bonus, lw, bb, kd = _rwkv_prep(
            u_r, shift_mu_prev[i][None], shift_mu_next[i][None],
            rwkv_w0[i].reshape(1, 2 * W), _block_diag2(rwkv_w2[i]).astype(BF16),
            rwkv_a0[i].reshape(1, 2 * W), _block_diag2(rwkv_a2[i]).astype(BF16),
            rwkv_g2[i].astype(BF16), rwkv_k_k[i][None], rwkv_k_a[i][None],
            rwkv_r_k[i].reshape(1, W), e_seg, tm)
        y = _rwkv_scan(r, vv, kk, lw, bb, kd, nsub=min(8, S // CHUNK))
        ob = _diff_attn(q, k, v, da_lq1[i][None], da_lk1[i][None], da_lq2[i][None],
                        da_lk2[i][None], da_subln_w[i][:, None], lambda_init,
                        tq=min(512, S), tk=min(512, S // 2))
        x = _merge(x, y, bonus, g, ob, norm_mix[i][None],
                   w_i[:, n_r + n_qk + n_v:], rwkv_ln_w[i][None], rwkv_ln_b[i][None], e_seg,
                   rwkv_w_o[i].astype(BF16), da_w_o[i].astype(BF16), w_out[i].astype(BF16), tm)
        x = _ffn(x, p[i], norm_ffn[i][None], w_ff1[i].astype(BF16), w_ff2[i].astype(BF16),
                 norm_ple[i][None], w_ple_gate[i].astype(BF16), w_ple_proj[i].astype(BF16),
                 norm_final[None] if i == L - 1 else None, tm, n_chunk=4)
    return x
```

```python
import functools
import math

import jax
import jax.numpy as jnp
from jax import lax
from jax.experimental import pallas as pl
from jax.experimental.pallas import tpu as pltpu

F32 = jnp.float32
BF16 = jnp.bfloat16

HEAD = 64
LANES = 128
CHUNK = 64
RMS_EPS = 1e-6
GN_EPS = 64e-5
SUBLN_EPS = 1e-5
ROPE_THETA = 500000.0
ROPE_DIM = 16
Q_SCALE = HEAD ** -0.5 * math.log2(math.e)
DECAY_SCALE = math.exp(-0.5)
VMEM_LIMIT = 56 * 1024 * 1024
TILES_PER_ITER = 8
HEADS_PER_STEP = 1
SUM_ROWS = 16
HALO = 8
SEG_BLOCK = 256


def _rms(x, g, eps):
    return x * lax.rsqrt(jnp.mean(x * x, axis=-1, keepdims=True) + eps) * g


def _sigmoid(x):
    return 1.0 / (1.0 + jnp.exp(-x))


def _mm(a, b):
    return jnp.dot(a.astype(BF16), b.astype(BF16), preferred_element_type=F32)


def _mm_nt(a, b):
    return lax.dot_general(a.astype(BF16), b.astype(BF16), (((1,), (1,)), ((), ())),
                           preferred_element_type=F32)


def _split2(x):
    hi = x.astype(BF16)
    lo = (x - hi.astype(F32)).astype(BF16)
    return hi, lo


def _mm_exact_rhs(a, b_bf16):
    hi, lo = _split2(a)
    return (jnp.dot(hi, b_bf16, preferred_element_type=F32)
            + jnp.dot(lo, b_bf16, preferred_element_type=F32))


def _seg_sum(x, e):
    n = e.shape[0]
    return jnp.concatenate([_mm_exact_rhs(x[:, c:c + n], e) for c in range(0, x.shape[1], n)],
                           axis=1)


def _mm3(a, b):
    ah, al = _split2(a)
    bh, bl = _split2(b)
    return (jnp.dot(ah, bh, preferred_element_type=F32)
            + jnp.dot(ah, bl, preferred_element_type=F32)
            + jnp.dot(al, bh, preferred_element_type=F32))


def _inproj_kernel(x_ref, g_ref, w_ref, wv_ref, cos_ref, sin_ref,
                   ur_ref, q_ref, k_ref, v_ref):
    n_r = ur_ref.shape[1]
    n_qk = 2 * q_ref.shape[1]
    h = _rms(x_ref[...], g_ref[...], RMS_EPS).astype(BF16)
    ur_ref[...] = jnp.dot(h, w_ref[:, :n_r], preferred_element_type=F32)
    v_ref[...] = lax.dot_general(wv_ref[...], h, (((1,), (1,)), ((), ())),
                                 preferred_element_type=F32).astype(BF16)
    cos_t = cos_ref[...]
    sin_t = sin_ref[...]
    lane = lax.broadcasted_iota(jnp.int32, cos_t.shape, 1)
    first_half = (lane & (HEAD - 1)) < (ROPE_DIM // 2)
    n_tiles = n_qk // LANES
    qk = jnp.dot(h, w_ref[:, n_r:n_r + n_qk], preferred_element_type=F32)
    for c in range(n_tiles):
        xq = qk[:, c * LANES:(c + 1) * LANES]
        partner = jnp.where(first_half,
                            pltpu.roll(xq, LANES - ROPE_DIM // 2, 1),
                            pltpu.roll(xq, ROPE_DIM // 2, 1))
        ro = xq * cos_t + partner * sin_t
        if c < n_tiles // 2:
            q_ref[:, c * LANES:(c + 1) * LANES] = (ro * Q_SCALE).astype(BF16)
        else:
            cc = c - n_tiles // 2
            k_ref[:, cc * LANES:(cc + 1) * LANES] = ro.astype(BF16)


def _rope_tables(S):
    pos = jnp.arange(S, dtype=F32)
    inv_freq = ROPE_THETA ** (-jnp.arange(0, ROPE_DIM, 2, dtype=F32) / ROPE_DIM)
    ang = pos[:, None] * inv_freq[None, :]
    cos8, sin8 = jnp.cos(ang), jnp.sin(ang)
    ones = jnp.ones((S, HEAD - ROPE_DIM), F32)
    zeros = jnp.zeros((S, HEAD - ROPE_DIM), F32)
    cos64 = jnp.concatenate([cos8, cos8, ones], axis=1)
    sin64 = jnp.concatenate([-sin8, sin8, zeros], axis=1)
    return jnp.tile(cos64, (1, 2)), jnp.tile(sin64, (1, 2))


def _in_proj(x, g, w_all, w_vt, nr, nqk, cos_t, sin_t, tm):
    B, S, D = x.shape
    nv = w_vt.shape[0]
    const = lambda b, i: (0, 0)
    row = lambda b, i: (b, i, 0)
    return pl.pallas_call(
        _inproj_kernel,
        grid=(B, S // tm),
        in_specs=[
            pl.BlockSpec((None, tm, D), row),
            pl.BlockSpec((1, D), const),
            pl.BlockSpec(w_all.shape, const),
            pl.BlockSpec((nv, D), const),
            pl.BlockSpec((tm, LANES), lambda b, i: (i, 0)),
            pl.BlockSpec((tm, LANES), lambda b, i: (i, 0)),
        ],
        out_specs=[
            pl.BlockSpec((None, tm, nr), row),
            pl.BlockSpec((None, tm, nqk // 2), row),
            pl.BlockSpec((None, tm, nqk // 2), row),
            pl.BlockSpec((None, nv, tm), lambda b, i: (b, 0, i)),
        ],
        out_shape=[
            jax.ShapeDtypeStruct((B, S, nr), F32),
            jax.ShapeDtypeStruct((B, S, nqk // 2), BF16),
            jax.ShapeDtypeStruct((B, S, nqk // 2), BF16),
            jax.ShapeDtypeStruct((B, nv, S), BF16),
        ],
        compiler_params=pltpu.CompilerParams(
            dimension_semantics=("parallel", "parallel"), vmem_limit_bytes=VMEM_LIMIT),
        name="in_proj",
    )(x, g, w_all, w_vt, cos_t, sin_t)


def _prep_kernel(u_ref, up_ref, un_ref, mup_ref, mun_ref, w0_ref, w2_ref, a0_ref, a2_ref,
                 g2_ref, kk_w_ref, ka_ref, rk_ref, e_ref,
                 r_ref, v_ref, kk_ref, g_ref, bonus_ref, lw_ref, b_ref, kd_ref):
    i = pl.program_id(1)
    n = pl.num_programs(1)
    u = u_ref[...]
    tm = u.shape[0]
    W = r_ref.shape[-1]
    row8 = lax.broadcasted_iota(jnp.int32, (HALO, 1), 0)
    prev_row = jnp.where(i > 0, up_ref[HALO - 1:HALO, :], 0.0)
    next_row = jnp.where(i < n - 1, un_ref[0:1, :], 0.0)
    rolled = pltpu.roll(u, 1, 0)
    u_prev = jnp.concatenate([jnp.where(row8 == 0, prev_row, rolled[:HALO]), rolled[HALO:]], axis=0)
    rolled = pltpu.roll(u, tm - 1, 0)
    u_next = jnp.concatenate([rolled[:tm - HALO],
                              jnp.where(row8 == HALO - 1, next_row, rolled[tm - HALO:])], axis=0)
    mup = mup_ref[...]
    mun = mun_ref[...]
    us = (1.0 - mup - mun) * u + mup * u_prev + mun * u_next

    r = us[:, 0:W]
    k = us[:, W:2 * W]
    v = us[:, 2 * W:3 * W]
    wd = us[:, 3 * W:3 * W + LANES]
    ad = us[:, 3 * W + LANES:3 * W + 2 * LANES]
    gd = us[:, 3 * W + 2 * LANES:3 * W + 3 * LANES]

    z = w0_ref[...] + _mm(jnp.tanh(wd), w2_ref[...])
    lw = -(_sigmoid(z) * DECAY_SCALE)
    a = _sigmoid(a0_ref[...] + _mm(ad, a2_ref[...]))
    g_ref[...] = _mm(_sigmoid(gd), g2_ref[...]).astype(g_ref.dtype)

    e = e_ref[...]
    kraw = k * kk_w_ref[...]
    ss = _seg_sum(kraw * kraw, e)
    kk = kraw / jnp.maximum(jnp.sqrt(ss), 1e-12)
    ka = ka_ref[...]
    kd_sum = jnp.zeros_like(k)
    for d in range(2):
        a_d = a[:, d * W:(d + 1) * W]
        kd = k * (1.0 + (a_d - 1.0) * ka)
        kd_sum = kd_sum + kd
        lw_ref[d] = lw[:, d * W:(d + 1) * W]
        b_ref[d] = kk * a_d
        kd_ref[d] = kd.astype(kd_ref.dtype)
    c = _seg_sum(r * rk_ref[...] * kd_sum, e)
    r_ref[...] = r.astype(r_ref.dtype)
    v_ref[...] = v.astype(v_ref.dtype)
    kk_ref[...] = kk
    bonus_ref[...] = (c * v).astype(bonus_ref.dtype)


def _rwkv_prep(u_r, mu_prev, mu_next, w0, w2bd, a0, a2bd, g2, k_k, k_a, r_k, e_seg, tm):
    B, S, C = u_r.shape
    W = k_k.shape[-1]
    nblk8 = S // 8
    const = lambda b, i: (0, 0)
    row = lambda b, i: (b, i, 0)
    drow = lambda b, i: (0, b, i, 0)
    tok = jax.ShapeDtypeStruct((B, S, W), F32)
    tok16 = jax.ShapeDtypeStruct((B, S, W), BF16)
    dtok = jax.ShapeDtypeStruct((2, B, S, W), F32)
    dtok16 = jax.ShapeDtypeStruct((2, B, S, W), BF16)
    return pl.pallas_call(
        _prep_kernel,
        grid=(B, S // tm),
        in_specs=[
            pl.BlockSpec((None, tm, C), row),
            pl.BlockSpec((None, 8, C), lambda b, i: (b, jnp.maximum(i * (tm // 8) - 1, 0), 0)),
            pl.BlockSpec((None, 8, C), lambda b, i: (b, jnp.minimum((i + 1) * (tm // 8), nblk8 - 1), 0)),
            pl.BlockSpec((1, C), const),
            pl.BlockSpec((1, C), const),
            pl.BlockSpec((1, 2 * W), const),
            pl.BlockSpec((LANES, 2 * W), const),
            pl.BlockSpec((1, 2 * W), const),
            pl.BlockSpec((LANES, 2 * W), const),
            pl.BlockSpec((LANES, W), const),
            pl.BlockSpec((1, W), const),
            pl.BlockSpec((1, W), const),
            pl.BlockSpec((1, W), const),
            pl.BlockSpec(e_seg.shape, const),
        ],
        out_specs=[pl.BlockSpec((None, tm, W), row)] * 5
                  + [pl.BlockSpec((2, None, tm, W), drow)] * 3,
        out_shape=[tok16, tok16, tok, tok16, tok16, dtok, dtok, dtok16],
        compiler_params=pltpu.CompilerParams(
            dimension_semantics=("parallel", "parallel"), vmem_limit_bytes=VMEM_LIMIT),
        name="rwkv_prep",
    )(u_r, u_r, u_r, mu_prev, mu_next, w0, w2bd, a0, a2bd, g2, k_k, k_a, r_k, e_seg)


def _scan_kernel(r_ref, v_ref, kk_ref, lw_ref, b_ref, kd_ref, y_ref,
                 t_scr, g_scr, h_scr, q_scr, yl_scr, *, nsub, n_pair, n_step):
    step = pl.program_id(0)
    last = pl.num_programs(0) - 2
    row_cur = jnp.minimum(step, last) // n_step
    step_prev = jnp.maximum(step - 1, 0)
    rev = (row_cur // n_pair) % 2
    rev_prev = ((step_prev // n_step) // n_pair) % 2
    sgn = 1 - 2 * rev
    C = CHUNK
    P2 = 2 * C

    @pl.when(step == 0)
    def _():
        t_scr[...] = jnp.zeros_like(t_scr)
        g_scr[...] = jnp.zeros_like(g_scr)
        h_scr[...] = jnp.zeros_like(h_scr)
        q_scr[...] = jnp.zeros_like(q_scr)
        yl_scr[...] = jnp.zeros_like(yl_scr)

    def slices(direction):
        out = []
        for s in range(nsub):
            ci = s + direction * (nsub - 1 - 2 * s)
            out.append(pl.ds(pl.multiple_of(ci * C, C), C))
        return out

    chunk_slices = slices(rev)
    chain_slices = slices(rev_prev)

    chain = {"t": jnp.where(step_prev % n_step == 0, 0.0, t_scr[...]), "next": 0}

    def chain_steps(n):
        for _ in range(n):
            s = chain["next"]
            if s == nsub:
                return
            t_cur = chain["t"]
            t16 = t_cur.astype(BF16)
            y_st = jnp.dot(q_scr[s], t16, preferred_element_type=F32) + yl_scr[s]
            chain["t"] = jnp.dot(g_scr[s], t16, preferred_element_type=F32) + h_scr[s]
            y_ref[chain_slices[s], :] = y_st[:C, :] + y_st[C:, :]
            chain["next"] = s + 1

    row = lax.broadcasted_iota(jnp.int32, (P2, P2), 0)
    col = lax.broadcasted_iota(jnp.int32, (P2, P2), 1)
    rt = row & (C - 1)
    ct = col & (C - 1)
    delta = (rt - ct) * sgn
    strict = delta > 0
    incl = delta >= 0
    eye = row == col
    ident = jnp.where(eye, 1.0, 0.0)
    blk8 = (rt >> 3) == (ct >> 3)
    blk16 = (rt >> 4) == (ct >> 4)
    blk32 = (rt >> 5) == (ct >> 5)
    in8 = strict & blk8
    in16 = strict & blk16 & ~blk8
    in32 = strict & blk32 & ~blk16
    in64 = strict & ~blk32
    lane = lax.broadcasted_iota(jnp.int32, (C, LANES), 1)
    head0 = lane < HEAD

    def stack(x):
        return jnp.concatenate([jnp.where(head0, x, 0.0), jnp.where(head0, 0.0, x)], axis=0)

    subs = range(nsub)
    sls = chunk_slices

    lws = [lw_ref[sl, :] for sl in sls]
    tots = [jnp.sum(lw, axis=0, keepdims=True) for lw in lws]
    tok = lax.broadcasted_iota(jnp.int32, (C, LANES), 0)
    cums = []
    for lw, tot in zip(lws, tots):
        pre = lw
        shift = 1
        while shift < C:
            pre = pre + jnp.where(tok >= shift, pltpu.roll(pre, shift, 0), 0.0)
            shift *= 2
        cums.append(jnp.where(rev == 1, tot - pre + lw, pre))
    a_st, r_st, b_st, k_st, bh_t, kh_t, v_st = [], [], [], [], [], [], []
    for s in subs:
        sl, lw, cum, tot = sls[s], lws[s], cums[s], tots[s]
        e_neg = jnp.exp(-cum)
        e_rem = jnp.exp(tot - cum)
        b = b_ref[sl, :]
        kd = kd_ref[sl, :]
        a_st.append(stack(kk_ref[sl, :] * -jnp.exp(cum - lw)).astype(BF16))
        r_st.append(stack(r_ref[sl, :] * jnp.exp(cum)))
        b_st.append(stack(b * e_neg).astype(BF16))
        k_st.append(stack(kd * e_neg).astype(BF16))
        bh_t.append(stack(b * e_rem).T.astype(BF16))
        kh_t.append(stack(kd * e_rem).T.astype(BF16))
        v_st.append(stack(v_ref[sl, :]).astype(BF16))

    nt = (((1,), (1,)), ((), ()))
    sc = [lax.dot_general(jnp.concatenate([a_st[s], r_st[s].astype(BF16)], axis=0),
                          jnp.concatenate([b_st[s], k_st[s]], axis=0), nt,
                          preferred_element_type=F32) for s in subs]
    per_stage = -(-nsub // 8)
    chain_steps(per_stage)
    l_ab = [x[:P2, :P2] for x in sc]
    l_ak = [jnp.where(strict, x[:P2, P2:], 0.0).astype(BF16) for x in sc]
    a_rb = [jnp.where(incl, x[P2:, :P2], 0.0).astype(BF16) for x in sc]
    a_rk = [jnp.where(incl, x[P2:, P2:], 0.0).astype(BF16) for x in sc]
    x_loc = [jnp.dot(l_ak[s], v_st[s], preferred_element_type=F32) for s in subs]
    chain_steps(per_stage)

    d8 = [jnp.where(in8, x, 0.0) for x in l_ab]
    m1 = [ident + x for x in d8]
    p1 = [_mm(x, x) for x in d8]
    chain_steps(per_stage)
    pm = [_mm(p1[s], jnp.concatenate([p1[s], m1[s]], axis=1)) for s in subs]
    m2 = [m1[s] + pm[s][:, P2:] for s in subs]
    m8 = [m2[s] + _mm(pm[s][:, :P2], m2[s]) for s in subs]
    chain_steps(per_stage)
    o16 = [jnp.where(in16, x, 0.0) for x in l_ab]
    om = [_mm(o16[s], m8[s]) for s in subs]
    m16 = [m8[s] + _mm(m8[s], om[s]) for s in subs]
    chain_steps(per_stage)
    o32 = [jnp.where(in32, x, 0.0) for x in l_ab]
    om = [_mm(o32[s], m16[s]) for s in subs]
    m32 = [(m16[s] + _mm(m16[s], om[s])).astype(BF16) for s in subs]
    chain_steps(per_stage)
    o64 = [jnp.where(in64, x, 0.0).astype(BF16) for x in l_ab]
    y1 = [jnp.dot(m32[s], jnp.concatenate([a_st[s], x_loc[s].astype(BF16)], axis=1),
                  preferred_element_type=F32) for s in subs]
    oy = [_mm(o64[s], y1[s]).astype(BF16) for s in subs]
    chain_steps(per_stage)
    wu16 = [(y1[s] + jnp.dot(m32[s], oy[s], preferred_element_type=F32)).astype(BF16)
            for s in subs]
    chain_steps(per_stage)
    zero_blk = jnp.zeros((P2, LANES), BF16)
    fin = [jnp.dot(jnp.concatenate([jnp.concatenate([a_rb[s], a_rk[s]], axis=1),
                                    jnp.concatenate([bh_t[s], kh_t[s]], axis=1)], axis=0),
                   jnp.concatenate([wu16[s], jnp.concatenate([zero_blk, v_st[s]], axis=1)], axis=0),
                   preferred_element_type=F32) for s in subs]
    chain_steps(nsub)
    t_scr[...] = chain["t"]
    for s in subs:
        q_scr[s] = (r_st[s] + fin[s][:P2, :LANES]).astype(BF16)
        yl_scr[s] = fin[s][:P2, LANES:]
        g_scr[s] = (jnp.where(eye, jnp.exp(tots[s]), 0.0) + fin[s][P2:, :LANES]).astype(BF16)
        h_scr[s] = fin[s][P2:, LANES:]


def _rwkv_scan(r, v, kk, lw, bb, kd, nsub):
    B, S, W = r.shape
    n_pair = W // LANES
    T = nsub * CHUNK
    n_step = S // T

    n_total = B * 2 * n_pair * n_step

    def locate(t):
        g = t // n_step
        c = t % n_step
        d = (g // n_pair) % 2
        return d, g // (2 * n_pair), c + d * (n_step - 1 - 2 * c), g % n_pair

    def shared(t):
        _, b, blk, p = locate(jnp.minimum(t, n_total - 1))
        return (b, blk, p)

    def per_dir(t):
        return locate(jnp.minimum(t, n_total - 1))

    def out_map(t):
        return locate(jnp.maximum(t - 1, 0))

    mat = pltpu.VMEM((nsub, LANES, LANES), F32)
    mat16 = pltpu.VMEM((nsub, LANES, LANES), BF16)
    return pl.pallas_call(
        functools.partial(_scan_kernel, nsub=nsub, n_pair=n_pair, n_step=n_step),
        grid=(n_total + 1,),
        in_specs=[pl.BlockSpec((None, T, LANES), shared)] * 3
                 + [pl.BlockSpec((None, None, T, LANES), per_dir)] * 3,
        out_specs=pl.BlockSpec((None, None, T, LANES), out_map),
        out_shape=jax.ShapeDtypeStruct((2, B, S, W), F32),
        scratch_shapes=[pltpu.VMEM((LANES, LANES), F32), mat16, mat, mat16, mat],
        compiler_params=pltpu.CompilerParams(
            dimension_semantics=("arbitrary",), vmem_limit_bytes=VMEM_LIMIT),
        name="rwkv_scan",
    )(r, v, kk, lw, bb, kd)


def _attn_kernel(q_ref, k_ref, vt_ref, lq1_ref, lk1_ref, lq2_ref, lk2_ref, sw_ref, o_ref,
                 m_scr, alpha_scr, acc_scr, s_even, s_odd, p_even, p_odd, mx_even, mx_odd,
                 *, tk, lambda_init):
    n_kv = k_ref.shape[0] // tk
    dv = vt_ref.shape[0] // HEADS_PER_STEP
    n_streams = 2 * HEADS_PER_STEP
    lane = lax.broadcasted_iota(jnp.int32, (q_ref.shape[0], LANES), 1)
    qc = []
    for hd in range(HEADS_PER_STEP):
        q = q_ref[:, hd * LANES:(hd + 1) * LANES]
        zero = jnp.zeros_like(q)
        qc += [jnp.where(lane < HEAD, q, zero), jnp.where(lane < HEAD, zero, q)]
    m_scr[...] = jnp.full(m_scr.shape, -jnp.inf, F32)
    acc_scr[...] = jnp.zeros(acc_scr.shape, F32)

    def scores(j, u):
        hd = u // 2
        ks = k_ref[pl.ds(pl.multiple_of(j * tk, tk), tk), hd * LANES:(hd + 1) * LANES]
        return lax.dot_general(ks, qc[u], (((1,), (1,)), ((), ())),
                               preferred_element_type=F32)

    ones_rows = jnp.ones((SUM_ROWS, tk), BF16)

    def weighted_values(j, p_ref, u):
        hd = u // 2
        vts = vt_ref[hd * dv:(hd + 1) * dv, pl.ds(pl.multiple_of(j * tk, tk), tk)]
        vts = jnp.concatenate([vts, ones_rows], axis=0)
        acc_scr[u] = alpha_scr[u] * acc_scr[u] + jnp.dot(vts, p_ref[u],
                                                         preferred_element_type=F32)

    def put_scores(j, s_ref, mx_ref, u):
        s = scores(j, u)
        s_ref[u] = s
        mx_ref[u] = jnp.max(s, axis=0, keepdims=True)

    def tile_step(j, s_cur, mx_cur, s_nxt, mx_nxt, p_cur, p_prev, first=False, last=False):
        for u in range(n_streams):
            if not first:
                weighted_values(j - 1, p_prev, u)
            if not last:
                put_scores(j + 1, s_nxt, mx_nxt, u)
        for u in range(n_streams):
            m_old = m_scr[u]
            m_new = jnp.maximum(m_old, mx_cur[u])
            alpha = jnp.exp2(m_old - m_new)
            p = jnp.exp2(s_cur[u] - m_new)
            p_cur[u] = p.astype(BF16)
            alpha_scr[u] = alpha
            m_scr[u] = m_new

    for u in range(n_streams):
        put_scores(0, s_even, mx_even, u)

    unroll = TILES_PER_ITER if n_kv % TILES_PER_ITER == 0 else 2

    def steps(base, head=False, tail=False):
        for t in range(0, unroll, 2):
            tile_step(base + t, s_even, mx_even, s_odd, mx_odd, p_even, p_odd,
                      first=head and t == 0)
            tile_step(base + t + 1, s_odd, mx_odd, s_even, mx_even, p_odd, p_even,
                      last=tail and t + 2 == unroll)

    def body(i, carry):
        steps(unroll * i)
        return carry

    n_groups = n_kv // unroll
    if n_groups == 1:
        steps(0, head=True, tail=True)
    else:
        steps(0, head=True)
        lax.fori_loop(1, n_groups - 1, body, 0)
        steps(n_kv - unroll, tail=True)
    for u in range(n_streams):
        weighted_values(n_kv - 1, p_odd, u)

    lam = (jnp.exp(jnp.sum(lq1_ref[...] * lk1_ref[...], axis=-1, keepdims=True))
           - jnp.exp(jnp.sum(lq2_ref[...] * lk2_ref[...], axis=-1, keepdims=True)) + lambda_init)
    for hd in range(HEADS_PER_STEP):
        u1, u2 = 2 * hd, 2 * hd + 1
        o = (acc_scr[u1, :dv, :] / acc_scr[u1, dv:dv + 1, :]
             - lam * (acc_scr[u2, :dv, :] / acc_scr[u2, dv:dv + 1, :]))
        o = o * lax.rsqrt(jnp.mean(o * o, axis=0, keepdims=True) + SUBLN_EPS)
        o = o * (sw_ref[...] * (1.0 - lambda_init))
        o_ref[:, hd * dv:(hd + 1) * dv] = o.T.astype(o_ref.dtype)


def _diff_attn(q, k, vt, lq1, lk1, lq2, lk2, subln_w, lambda_init, tq, tk):
    B, S, QW = q.shape
    H = QW // LANES
    DV = vt.shape[1] // H
    hps = HEADS_PER_STEP
    ns = 2 * hps
    const = lambda b, h, i: (0, 0)
    return pl.pallas_call(
        functools.partial(_attn_kernel, tk=tk, lambda_init=lambda_init),
        grid=(B, H // hps, S // tq),
        in_specs=[
            pl.BlockSpec((None, tq, hps * LANES), lambda b, h, i: (b, i, h)),
            pl.BlockSpec((None, S, hps * LANES), lambda b, h, i: (b, 0, h)),
            pl.BlockSpec((None, hps * DV, S), lambda b, h, i: (b, h, 0)),
            pl.BlockSpec((1, HEAD), const),
            pl.BlockSpec((1, HEAD), const),
            pl.BlockSpec((1, HEAD), const),
            pl.BlockSpec((1, HEAD), const),
            pl.BlockSpec((DV, 1), const),
        ],
        out_specs=pl.BlockSpec((None, tq, hps * DV), lambda b, h, i: (b, i, h)),
        out_shape=jax.ShapeDtypeStruct((B, S, H * DV), BF16),
        scratch_shapes=[pltpu.VMEM((ns, 1, tq), F32),
                        pltpu.VMEM((ns, 1, tq), F32),
                        pltpu.VMEM((ns, DV + SUM_ROWS, tq), F32),
                        pltpu.VMEM((ns, tk, tq), F32),
                        pltpu.VMEM((ns, tk, tq), F32),
                        pltpu.VMEM((ns, tk, tq), BF16),
                        pltpu.VMEM((ns, tk, tq), BF16),
                        pltpu.VMEM((ns, 1, tq), F32),
                        pltpu.VMEM((ns, 1, tq), F32)],
        compiler_params=pltpu.CompilerParams(
            dimension_semantics=("parallel", "parallel", "arbitrary"),
            vmem_limit_bytes=VMEM_LIMIT),
        name="diff_attn",
    )(q, k, vt, lq1, lk1, lq2, lk2, subln_w)


def _merge_kernel(x_ref, y_ref, bonus_ref, g_ref, ob_ref, nm_ref, wg_ref, lnw_ref, lnb_ref,
                  e_ref, woa_ref, wob_ref, wout_ref, o_ref):
    x = x_ref[...]
    D = x.shape[-1]
    h = _rms(x, nm_ref[...], RMS_EPS).astype(BF16)
    n_in = wg_ref.shape[1]
    gates = _sigmoid(jnp.dot(h, wg_ref[:, n_in - 2 * D:], preferred_element_type=F32))
    e = e_ref[...]
    y = y_ref[0] + y_ref[1]
    mean = _seg_sum(y, e) * (1.0 / HEAD)
    yc = y - mean
    var = _seg_sum(yc * yc, e) * (1.0 / HEAD)
    yn = yc * lax.rsqrt(var + GN_EPS) * lnw_ref[...] + lnb_ref[...]
    ya = _mm((yn + bonus_ref[...]) * g_ref[...], woa_ref[...])
    yb = jnp.dot(ob_ref[...], wob_ref[...], preferred_element_type=F32)
    merged = gates[:, :D] * ya + gates[:, D:] * yb
    o_ref[...] = x + _mm(merged, wout_ref[...])


def _merge(x, y, bonus, g, ob, norm_mix, w_g, ln_w, ln_b, e_seg, w_oa, w_ob, w_out, tm):
    B, S, D = x.shape
    W = bonus.shape[-1]
    const = lambda b, i: (0, 0)
    row = lambda b, i: (b, i, 0)
    return pl.pallas_call(
        _merge_kernel,
        grid=(B, S // tm),
        in_specs=[
            pl.BlockSpec((None, tm, D), row),
            pl.BlockSpec((2, None, tm, W), lambda b, i: (0, b, i, 0)),
            pl.BlockSpec((None, tm, W), row),
            pl.BlockSpec((None, tm, W), row),
            pl.BlockSpec((None, tm, W), row),
            pl.BlockSpec((1, D), const),
            pl.BlockSpec(w_g.shape, const),
            pl.BlockSpec((1, W), const),
            pl.BlockSpec((1, W), const),
            pl.BlockSpec(e_seg.shape, const),
            pl.BlockSpec((W, D), const),
            pl.BlockSpec((W, D), const),
            pl.BlockSpec((D, D), const),
        ],
        out_specs=pl.BlockSpec((None, tm, D), row),
        out_shape=jax.ShapeDtypeStruct((B, S, D), F32),
        compiler_params=pltpu.CompilerParams(
            dimension_semantics=("parallel", "parallel"), vmem_limit_bytes=VMEM_LIMIT),
        name="merge",
    )(x, y, bonus, g, ob, norm_mix, w_g, ln_w, ln_b, e_seg, w_oa, w_ob, w_out)


def _ffn_kernel(x_ref, p_ref, nf_ref, w1_ref, w2_ref, np_ref, wpg_ref, wpp_ref, nfin_ref, o_ref,
                *, n_chunk):
    x = x_ref[...]
    h = _rms(x, nf_ref[...], RMS_EPS).astype(BF16)
    ff = w1_ref.shape[1] // n_chunk
    acc = x
    for c in range(n_chunk):
        t = jnp.maximum(jnp.dot(h, w1_ref[:, c * ff:(c + 1) * ff], preferred_element_type=F32), 0.0)
        acc = acc + jnp.dot((t * t).astype(BF16), w2_ref[c * ff:(c + 1) * ff, :],
                            preferred_element_type=F32)
    x = acc
    h = _rms(x, np_ref[...], RMS_EPS).astype(BF16)
    gate = _sigmoid(jnp.dot(h, wpg_ref[...], preferred_element_type=F32))
    x = x + gate * _mm(p_ref[...], wpp_ref[...])
    o_ref[...] = x if nfin_ref is None else _rms(x, nfin_ref[...], RMS_EPS)


def _ffn_kernel_plain(x_ref, p_ref, nf_ref, w1_ref, w2_ref, np_ref, wpg_ref, wpp_ref, o_ref, *, n_chunk):
    _ffn_kernel(x_ref, p_ref, nf_ref, w1_ref, w2_ref, np_ref, wpg_ref, wpp_ref, None, o_ref,
                n_chunk=n_chunk)


def _ffn(x, p, norm_ffn, w1, w2, norm_ple, w_pg, w_pp, norm_final, tm, n_chunk):
    B, S, D = x.shape
    PD = p.shape[-1]
    FF = w1.shape[1]
    const = lambda b, i: (0, 0)
    row = lambda b, i: (b, i, 0)
    in_specs = [
        pl.BlockSpec((None, tm, D), row),
        pl.BlockSpec((None, tm, PD), row),
        pl.BlockSpec((1, D), const),
        pl.BlockSpec((D, FF), const, pipeline_mode=pl.Buffered(1)),
        pl.BlockSpec((FF, D), const, pipeline_mode=pl.Buffered(1)),
        pl.BlockSpec((1, D), const),
        pl.BlockSpec((D, D), const, pipeline_mode=pl.Buffered(1)),
        pl.BlockSpec((PD, D), const, pipeline_mode=pl.Buffered(1)),
    ]
    args = [x, p, norm_ffn, w1, w2, norm_ple, w_pg, w_pp]
    if norm_final is not None:
        in_specs.append(pl.BlockSpec((1, D), const))
        args.append(norm_final)
        body = functools.partial(_ffn_kernel, n_chunk=n_chunk)
    else:
        body = functools.partial(_ffn_kernel_plain, n_chunk=n_chunk)
    return pl.pallas_call(
        body,
        grid=(B, S // tm),
        in_specs=in_specs,
        out_specs=pl.BlockSpec((None, tm, D), row),
        out_shape=jax.ShapeDtypeStruct((B, S, D), F32),
        compiler_params=pltpu.CompilerParams(
            dimension_semantics=("parallel", "parallel"), vmem_limit_bytes=VMEM_LIMIT),
        name="ffn",
    )(*args)


def _block_diag2(w):
    z = jnp.zeros_like(w[0])
    return jnp.concatenate([jnp.concatenate([w[0], z], axis=1),
                            jnp.concatenate([z, w[1]], axis=1)], axis=0)


def kernel(x, p, norm_mix, w_in, shift_mu_prev, shift_mu_next, rwkv_w0, rwkv_w2, rwkv_a0,
           rwkv_a2, rwkv_g2, rwkv_k_k, rwkv_k_a, rwkv_r_k, rwkv_ln_w, rwkv_ln_b, rwkv_w_o,
           da_lq1, da_lk1, da_lq2, da_lk2, da_subln_w, da_w_o, w_out, norm_ffn, w_ff1, w_ff2,
           norm_ple, w_ple_gate, w_ple_proj, norm_final):
    B, S, D = x.shape
    L = w_in.shape[0]
    W = rwkv_w0.shape[-1]
    n_r = shift_mu_prev.shape[-1]
    n_qk = 2 * da_w_o.shape[1]
    n_v = da_w_o.shape[1]
    cos_t, sin_t = _rope_tables(S)
    seg = jnp.arange(SEG_BLOCK) // HEAD
    e_seg = (seg[:, None] == seg[None, :]).astype(BF16)
    tm = min(256, S)
    for i in range(L):
        lambda_init = 0.8 - 0.6 * math.exp(-0.3 * i)
        w_i = w_in[i].astype(BF16)
        u_r, q, k, v = _in_proj(
            x, norm_mix[i][None], w_i, w_i[:, n_r + n_qk:n_r + n_qk + n_v].T, n_r, n_qk,
            cos_t, sin_t, min(512, S))
        r, vv, kk, g, bonus, lw, bb, kd = _rwkv_prep(
            u_r, shift_mu_prev[i][None], shift_mu_next[i][None],
            rwkv_w0[i].reshape(1, 2 * W), _block_diag2(rwkv_w2[i]).astype(BF16),
            rwkv_a0[i].reshape(1, 2 * W), _block_diag2(rwkv_a2[i]).astype(BF16),
            rwkv_g2[i].astype(BF16), rwkv_k_k[i][None], rwkv_k_a[i][None],
            rwkv_r_k[i].reshape(1, W), e_seg, tm)
        y = _rwkv_scan(r, vv, kk, lw, bb, kd, nsub=min(8, S // CHUNK))
        ob = _diff_attn(q, k, v, da_lq1[i][None], da_lk1[i][None], da_lq2[i][None],
                        da_lk2[i][None], da_subln_w[i][:, None], lambda_init,
                        tq=min(512, S), tk=min(512, S // 2))
        x = _merge(x, y, bonus, g, ob, norm_mix[i][None],
                   w_i, rwkv_ln_w[i][None], rwkv_ln_b[i][None], e_seg,
                   rwkv_w_o[i].astype(BF16), da_w_o[i].astype(BF16), w_out[i].astype(BF16),
                   min(512, S))
        x = _ffn(x, p[i], norm_ffn[i][None], w_ff1[i].astype(BF16), w_ff2[i].astype(BF16),
                 norm_ple[i][None], w_ple_gate[i].astype(BF16), w_ple_proj[i].astype(BF16),
                 norm_final[None] if i == L - 1 else None, min(512, S), n_chunk=4)
    return x
```

```python
import functools
import math

import jax
import jax.numpy as jnp
from jax import lax
from jax.experimental import pallas as pl
from jax.experimental.pallas import tpu as pltpu

F32 = jnp.float32
BF16 = jnp.bfloat16

HEAD = 64
LANES = 128
CHUNK = 64
RMS_EPS = 1e-6
GN_EPS = 64e-5
SUBLN_EPS = 1e-5
ROPE_THETA = 500000.0
ROPE_DIM = 16
Q_SCALE = HEAD ** -0.5 * math.log2(math.e)
DECAY_SCALE = math.exp(-0.5)
VMEM_LIMIT = 56 * 1024 * 1024
TILES_PER_ITER = 8
HEADS_PER_STEP = 1
SUM_ROWS = 16
HALO = 8
SEG_BLOCK = 256


def _rms(x, g, eps):
    return x * lax.rsqrt(jnp.mean(x * x, axis=-1, keepdims=True) + eps) * g


def _sigmoid(x):
    return 1.0 / (1.0 + jnp.exp(-x))


def _mm(a, b):
    return jnp.dot(a.astype(BF16), b.astype(BF16), preferred_element_type=F32)


def _mm_nt(a, b):
    return lax.dot_general(a.astype(BF16), b.astype(BF16), (((1,), (1,)), ((), ())),
                           preferred_element_type=F32)


def _split2(x):
    hi = x.astype(BF16)
    lo = (x - hi.astype(F32)).astype(BF16)
    return hi, lo


def _mm_exact_rhs(a, b_bf16):
    hi, lo = _split2(a)
    return (jnp.dot(hi, b_bf16, preferred_element_type=F32)
            + jnp.dot(lo, b_bf16, preferred_element_type=F32))


def _seg_sum(x, e):
    n = e.shape[0]
    return jnp.concatenate([_mm_exact_rhs(x[:, c:c + n], e) for c in range(0, x.shape[1], n)],
                           axis=1)


def _mm3(a, b):
    ah, al = _split2(a)
    bh, bl = _split2(b)
    return (jnp.dot(ah, bh, preferred_element_type=F32)
            + jnp.dot(ah, bl, preferred_element_type=F32)
            + jnp.dot(al, bh, preferred_element_type=F32))


def _inproj_kernel(x_ref, g_ref, w_ref, wv_ref, cos_ref, sin_ref,
                   ur_ref, q_ref, k_ref, v_ref):
    n_r = ur_ref.shape[1]
    n_qk = 2 * q_ref.shape[1]
    h = _rms(x_ref[...], g_ref[...], RMS_EPS).astype(BF16)
    ur_ref[...] = jnp.dot(h, w_ref[:, :n_r], preferred_element_type=F32)
    v_ref[...] = lax.dot_general(wv_ref[...], h, (((1,), (1,)), ((), ())),
                                 preferred_element_type=F32).astype(BF16)
    cos_t = cos_ref[...]
    sin_t = sin_ref[...]
    lane = lax.broadcasted_iota(jnp.int32, cos_t.shape, 1)
    first_half = (lane & (HEAD - 1)) < (ROPE_DIM // 2)
    n_tiles = n_qk // LANES
    qk = jnp.dot(h, w_ref[:, n_r:n_r + n_qk], preferred_element_type=F32)
    for c in range(n_tiles):
        xq = qk[:, c * LANES:(c + 1) * LANES]
        partner = jnp.where(first_half,
                            pltpu.roll(xq, LANES - ROPE_DIM // 2, 1),
                            pltpu.roll(xq, ROPE_DIM // 2, 1))
        ro = xq * cos_t + partner * sin_t
        if c < n_tiles // 2:
            q_ref[:, c * LANES:(c + 1) * LANES] = (ro * Q_SCALE).astype(BF16)
        else:
            cc = c - n_tiles // 2
            k_ref[:, cc * LANES:(cc + 1) * LANES] = ro.astype(BF16)


def _rope_tables(S):
    pos = jnp.arange(S, dtype=F32)
    inv_freq = ROPE_THETA ** (-jnp.arange(0, ROPE_DIM, 2, dtype=F32) / ROPE_DIM)
    ang = pos[:, None] * inv_freq[None, :]
    cos8, sin8 = jnp.cos(ang), jnp.sin(ang)
    ones = jnp.ones((S, HEAD - ROPE_DIM), F32)
    zeros = jnp.zeros((S, HEAD - ROPE_DIM), F32)
    cos64 = jnp.concatenate([cos8, cos8, ones], axis=1)
    sin64 = jnp.concatenate([-sin8, sin8, zeros], axis=1)
    return jnp.tile(cos64, (1, 2)), jnp.tile(sin64, (1, 2))


def _in_proj(x, g, w_all, w_vt, nr, nqk, cos_t, sin_t, tm):
    B, S, D = x.shape
    nv = w_vt.shape[0]
    const = lambda b, i: (0, 0)
    row = lambda b, i: (b, i, 0)
    return pl.pallas_call(
        _inproj_kernel,
        grid=(B, S // tm),
        in_specs=[
            pl.BlockSpec((None, tm, D), row),
            pl.BlockSpec((1, D), const),
            pl.BlockSpec(w_all.shape, const),
            pl.BlockSpec((nv, D), const),
            pl.BlockSpec((tm, LANES), lambda b, i: (i, 0)),
            pl.BlockSpec((tm, LANES), lambda b, i: (i, 0)),
        ],
        out_specs=[
            pl.BlockSpec((None, tm, nr), row),
            pl.BlockSpec((None, tm, nqk // 2), row),
            pl.BlockSpec((None, tm, nqk // 2), row),
            pl.BlockSpec((None, nv, tm), lambda b, i: (b, 0, i)),
        ],
        out_shape=[
            jax.ShapeDtypeStruct((B, S, nr), F32),
            jax.ShapeDtypeStruct((B, S, nqk // 2), BF16),
            jax.ShapeDtypeStruct((B, S, nqk // 2), BF16),
            jax.ShapeDtypeStruct((B, nv, S), BF16),
        ],
        compiler_params=pltpu.CompilerParams(
            dimension_semantics=("parallel", "parallel"), vmem_limit_bytes=VMEM_LIMIT),
        name="in_proj",
    )(x, g, w_all, w_vt, cos_t, sin_t)


def _prep_kernel(u_ref, up_ref, un_ref, mup_ref, mun_ref, w0_ref, w2_ref, a0_ref, a2_ref,
                 g2_ref, kk_w_ref, ka_ref, rk_ref, e_ref,
                 r_ref, v_ref, kk_ref, g_ref, bonus_ref, lw_ref, b_ref, kd_ref):
    i = pl.program_id(1)
    n = pl.num_programs(1)
    u = u_ref[...]
    tm = u.shape[0]
    W = r_ref.shape[-1]
    row8 = lax.broadcasted_iota(jnp.int32, (HALO, 1), 0)
    prev_row = jnp.where(i > 0, up_ref[HALO - 1:HALO, :], 0.0)
    next_row = jnp.where(i < n - 1, un_ref[0:1, :], 0.0)
    rolled = pltpu.roll(u, 1, 0)
    u_prev = jnp.concatenate([jnp.where(row8 == 0, prev_row, rolled[:HALO]), rolled[HALO:]], axis=0)
    rolled = pltpu.roll(u, tm - 1, 0)
    u_next = jnp.concatenate([rolled[:tm - HALO],
                              jnp.where(row8 == HALO - 1, next_row, rolled[tm - HALO:])], axis=0)
    mup = mup_ref[...]
    mun = mun_ref[...]
    us = (1.0 - mup - mun) * u + mup * u_prev + mun * u_next

    r = us[:, 0:W]
    k = us[:, W:2 * W]
    v = us[:, 2 * W:3 * W]
    wd = us[:, 3 * W:3 * W + LANES]
    ad = us[:, 3 * W + LANES:3 * W + 2 * LANES]
    gd = us[:, 3 * W + 2 * LANES:3 * W + 3 * LANES]

    z = w0_ref[...] + _mm(jnp.tanh(wd), w2_ref[...])
    lw = -(_sigmoid(z) * DECAY_SCALE)
    a = _sigmoid(a0_ref[...] + _mm(ad, a2_ref[...]))
    g_ref[...] = _mm(_sigmoid(gd), g2_ref[...]).astype(g_ref.dtype)

    e = e_ref[...]
    kraw = k * kk_w_ref[...]
    ss = _seg_sum(kraw * kraw, e)
    kk = kraw / jnp.maximum(jnp.sqrt(ss), 1e-12)
    ka = ka_ref[...]
    kd_sum = jnp.zeros_like(k)
    for d in range(2):
        a_d = a[:, d * W:(d + 1) * W]
        kd = k * (1.0 + (a_d - 1.0) * ka)
        kd_sum = kd_sum + kd
        lw_ref[d] = lw[:, d * W:(d + 1) * W]
        b_ref[d] = kk * a_d
        kd_ref[d] = kd.astype(kd_ref.dtype)
    c = _seg_sum(r * rk_ref[...] * kd_sum, e)
    r_ref[...] = r.astype(r_ref.dtype)
    v_ref[...] = v.astype(v_ref.dtype)
    kk_ref[...] = kk
    bonus_ref[...] = (c * v).astype(bonus_ref.dtype)


def _rwkv_prep(u_r, mu_prev, mu_next, w0, w2bd, a0, a2bd, g2, k_k, k_a, r_k, e_seg, tm):
    B, S, C = u_r.shape
    W = k_k.shape[-1]
    nblk8 = S // 8
    const = lambda b, i: (0, 0)
    row = lambda b, i: (b, i, 0)
    drow = lambda b, i: (0, b, i, 0)
    tok = jax.ShapeDtypeStruct((B, S, W), F32)
    tok16 = jax.ShapeDtypeStruct((B, S, W), BF16)
    dtok = jax.ShapeDtypeStruct((2, B, S, W), F32)
    dtok16 = jax.ShapeDtypeStruct((2, B, S, W), BF16)
    return pl.pallas_call(
        _prep_kernel,
        grid=(B, S // tm),
        in_specs=[
            pl.BlockSpec((None, tm, C), row),
            pl.BlockSpec((None, 8, C), lambda b, i: (b, jnp.maximum(i * (tm // 8) - 1, 0), 0)),
            pl.BlockSpec((None, 8, C), lambda b, i: (b, jnp.minimum((i + 1) * (tm // 8), nblk8 - 1), 0)),
            pl.BlockSpec((1, C), const),
            pl.BlockSpec((1, C), const),
            pl.BlockSpec((1, 2 * W), const),
            pl.BlockSpec((LANES, 2 * W), const),
            pl.BlockSpec((1, 2 * W), const),
            pl.BlockSpec((LANES, 2 * W), const),
            pl.BlockSpec((LANES, W), const),
            pl.BlockSpec((1, W), const),
            pl.BlockSpec((1, W), const),
            pl.BlockSpec((1, W), const),
            pl.BlockSpec(e_seg.shape, const),
        ],
        out_specs=[pl.BlockSpec((None, tm, W), row)] * 5
                  + [pl.BlockSpec((2, None, tm, W), drow)] * 3,
        out_shape=[tok16, tok16, tok, tok16, tok16, dtok, dtok, dtok16],
        compiler_params=pltpu.CompilerParams(
            dimension_semantics=("parallel", "parallel"), vmem_limit_bytes=VMEM_LIMIT),
        name="rwkv_prep",
    )(u_r, u_r, u_r, mu_prev, mu_next, w0, w2bd, a0, a2bd, g2, k_k, k_a, r_k, e_seg)


def _scan_kernel(r_ref, v_ref, kk_ref, lw_ref, b_ref, kd_ref, y_ref,
                 t_scr, g_scr, h_scr, q_scr, yl_scr, *, nsub, n_pair, n_step):
    step = pl.program_id(0)
    last = pl.num_programs(0) - 2
    row_cur = jnp.minimum(step, last) // n_step
    step_prev = jnp.maximum(step - 1, 0)
    rev = (row_cur // n_pair) % 2
    rev_prev = ((step_prev // n_step) // n_pair) % 2
    sgn = 1 - 2 * rev
    C = CHUNK
    P2 = 2 * C

    @pl.when(step == 0)
    def _():
        t_scr[...] = jnp.zeros_like(t_scr)
        g_scr[...] = jnp.zeros_like(g_scr)
        h_scr[...] = jnp.zeros_like(h_scr)
        q_scr[...] = jnp.zeros_like(q_scr)
        yl_scr[...] = jnp.zeros_like(yl_scr)

    def slices(direction):
        out = []
        for s in range(nsub):
            ci = s + direction * (nsub - 1 - 2 * s)
            out.append(pl.ds(pl.multiple_of(ci * C, C), C))
        return out

    chunk_slices = slices(rev)
    chain_slices = slices(rev_prev)

    chain = {"t": jnp.where(step_prev % n_step == 0, 0.0, t_scr[...]), "next": 0}

    def chain_steps(n):
        for _ in range(n):
            s = chain["next"]
            if s == nsub:
                return
            t_cur = chain["t"]
            t16 = t_cur.astype(BF16)
            y_st = jnp.dot(q_scr[s], t16, preferred_element_type=F32) + yl_scr[s]
            chain["t"] = jnp.dot(g_scr[s], t16, preferred_element_type=F32) + h_scr[s]
            y_ref[chain_slices[s], :] = y_st[:C, :] + y_st[C:, :]
            chain["next"] = s + 1

    row = lax.broadcasted_iota(jnp.int32, (P2, P2), 0)
    col = lax.broadcasted_iota(jnp.int32, (P2, P2), 1)
    rt = row & (C - 1)
    ct = col & (C - 1)
    delta = (rt - ct) * sgn
    strict = delta > 0
    incl = delta >= 0
    eye = row == col
    ident = jnp.where(eye, 1.0, 0.0)
    blk8 = (rt >> 3) == (ct >> 3)
    blk16 = (rt >> 4) == (ct >> 4)
    blk32 = (rt >> 5) == (ct >> 5)
    in8 = strict & blk8
    in16 = strict & blk16 & ~blk8
    in32 = strict & blk32 & ~blk16
    in64 = strict & ~blk32
    lane = lax.broadcasted_iota(jnp.int32, (C, LANES), 1)
    head0 = lane < HEAD

    def stack(x):
        return jnp.concatenate([jnp.where(head0, x, 0.0), jnp.where(head0, 0.0, x)], axis=0)

    subs = range(nsub)
    sls = chunk_slices

    lws = [lw_ref[sl, :] for sl in sls]
    tots = [jnp.sum(lw, axis=0, keepdims=True) for lw in lws]
    tok = lax.broadcasted_iota(jnp.int32, (C, LANES), 0)
    cums = []
    for lw, tot in zip(lws, tots):
        pre = lw
        shift = 1
        while shift < C:
            pre = pre + jnp.where(tok >= shift, pltpu.roll(pre, shift, 0), 0.0)
            shift *= 2
        cums.append(jnp.where(rev == 1, tot - pre + lw, pre))
    a_st, r_st, b_st, k_st, bh_t, kh_t, v_st = [], [], [], [], [], [], []
    for s in subs:
        sl, lw, cum, tot = sls[s], lws[s], cums[s], tots[s]
        e_neg = jnp.exp(-cum)
        e_rem = jnp.exp(tot - cum)
        b = b_ref[sl, :]
        kd = kd_ref[sl, :]
        a_st.append(stack(kk_ref[sl, :] * -jnp.exp(cum - lw)).astype(BF16))
        r_st.append(stack(r_ref[sl, :] * jnp.exp(cum)))
        b_st.append(stack(b * e_neg).astype(BF16))
        k_st.append(stack(kd * e_neg).astype(BF16))
        bh_t.append(stack(b * e_rem).T.astype(BF16))
        kh_t.append(stack(kd * e_rem).T.astype(BF16))
        v_st.append(stack(v_ref[sl, :]).astype(BF16))

    nt = (((1,), (1,)), ((), ()))
    sc = [lax.dot_general(jnp.concatenate([a_st[s], r_st[s].astype(BF16)], axis=0),
                          jnp.concatenate([b_st[s], k_st[s]], axis=0), nt,
                          preferred_element_type=F32) for s in subs]
    per_stage = -(-nsub // 8)
    chain_steps(per_stage)
    l_ab = [x[:P2, :P2] for x in sc]
    l_ak = [jnp.where(strict, x[:P2, P2:], 0.0).astype(BF16) for x in sc]
    a_rb = [jnp.where(incl, x[P2:, :P2], 0.0).astype(BF16) for x in sc]
    a_rk = [jnp.where(incl, x[P2:, P2:], 0.0).astype(BF16) for x in sc]
    x_loc = [jnp.dot(l_ak[s], v_st[s], preferred_element_type=F32) for s in subs]
    chain_steps(per_stage)

    d8 = [jnp.where(in8, x, 0.0) for x in l_ab]
    m1 = [ident + x for x in d8]
    p1 = [_mm(x, x) for x in d8]
    chain_steps(per_stage)
    pm = [_mm(p1[s], jnp.concatenate([p1[s], m1[s]], axis=1)) for s in subs]
    m2 = [m1[s] + pm[s][:, P2:] for s in subs]
    m8 = [m2[s] + _mm(pm[s][:, :P2], m2[s]) for s in subs]
    chain_steps(per_stage)
    o16 = [jnp.where(in16, x, 0.0) for x in l_ab]
    om = [_mm(o16[s], m8[s]) for s in subs]
    m16 = [m8[s] + _mm(m8[s], om[s]) for s in subs]
    chain_steps(per_stage)
    o32 = [jnp.where(in32, x, 0.0) for x in l_ab]
    om = [_mm(o32[s], m16[s]) for s in subs]
    m32 = [(m16[s] + _mm(m16[s], om[s])).astype(BF16) for s in subs]
    chain_steps(per_stage)
    o64 = [jnp.where(in64, x, 0.0).astype(BF16) for x in l_ab]
    y1 = [jnp.dot(m32[s], jnp.concatenate([a_st[s], x_loc[s].astype(BF16)], axis=1),
                  preferred_element_type=F32) for s in subs]
    oy = [_mm(o64[s], y1[s]).astype(BF16) for s in subs]
    chain_steps(per_stage)
    wu16 = [(y1[s] + jnp.dot(m32[s], oy[s], preferred_element_type=F32)).astype(BF16)
            for s in subs]
    chain_steps(per_stage)
    zero_blk = jnp.zeros((P2, LANES), BF16)
    fin = [jnp.dot(jnp.concatenate([jnp.concatenate([a_rb[s], a_rk[s]], axis=1),
                                    jnp.concatenate([bh_t[s], kh_t[s]], axis=1)], axis=0),
                   jnp.concatenate([wu16[s], jnp.concatenate([zero_blk, v_st[s]], axis=1)], axis=0),
                   preferred_element_type=F32) for s in subs]
    chain_steps(nsub)
    t_scr[...] = chain["t"]
    for s in subs:
        q_scr[s] = (r_st[s] + fin[s][:P2, :LANES]).astype(BF16)
        yl_scr[s] = fin[s][:P2, LANES:]
        g_scr[s] = (jnp.where(eye, jnp.exp(tots[s]), 0.0) + fin[s][P2:, :LANES]).astype(BF16)
        h_scr[s] = fin[s][P2:, LANES:]


def _rwkv_scan(r, v, kk, lw, bb, kd, nsub):
    B, S, W = r.shape
    n_pair = W // LANES
    T = nsub * CHUNK
    n_step = S // T

    n_total = B * 2 * n_pair * n_step

    def locate(t):
        g = t // n_step
        c = t % n_step
        d = (g // n_pair) % 2
        return d, g // (2 * n_pair), c + d * (n_step - 1 - 2 * c), g % n_pair

    def shared(t):
        _, b, blk, p = locate(jnp.minimum(t, n_total - 1))
        return (b, blk, p)

    def per_dir(t):
        return locate(jnp.minimum(t, n_total - 1))

    def out_map(t):
        return locate(jnp.maximum(t - 1, 0))

    mat = pltpu.VMEM((nsub, LANES, LANES), F32)
    mat16 = pltpu.VMEM((nsub, LANES, LANES), BF16)
    return pl.pallas_call(
        functools.partial(_scan_kernel, nsub=nsub, n_pair=n_pair, n_step=n_step),
        grid=(n_total + 1,),
        in_specs=[pl.BlockSpec((None, T, LANES), shared)] * 3
                 + [pl.BlockSpec((None, None, T, LANES), per_dir)] * 3,
        out_specs=pl.BlockSpec((None, None, T, LANES), out_map),
        out_shape=jax.ShapeDtypeStruct((2, B, S, W), F32),
        scratch_shapes=[pltpu.VMEM((LANES, LANES), F32), mat16, mat, mat16, mat],
        compiler_params=pltpu.CompilerParams(
            dimension_semantics=("arbitrary",), vmem_limit_bytes=VMEM_LIMIT),
        name="rwkv_scan",
    )(r, v, kk, lw, bb, kd)


def _attn_kernel(q_ref, qn_ref, k_ref, vt_ref, lq1_ref, lk1_ref, lq2_ref, lk2_ref, sw_ref, o_ref,
                 m_scr, alpha_scr, acc_scr, s_even, s_odd, p_even, p_odd, mx_even, mx_odd,
                 *, tk, lambda_init):
    n_kv = k_ref.shape[0] // tk
    dv = vt_ref.shape[0] // HEADS_PER_STEP
    n_streams = 2 * HEADS_PER_STEP
    lane = lax.broadcasted_iota(jnp.int32, (q_ref.shape[0], LANES), 1)

    def component_queries(ref):
        out = []
        for hd in range(HEADS_PER_STEP):
            q = ref[:, hd * LANES:(hd + 1) * LANES]
            zero = jnp.zeros_like(q)
            out += [jnp.where(lane < HEAD, q, zero), jnp.where(lane < HEAD, zero, q)]
        return out

    qc = component_queries(q_ref)
    m_scr[...] = jnp.full(m_scr.shape, -jnp.inf, F32)
    acc_scr[...] = jnp.zeros(acc_scr.shape, F32)

    def scores(j, u, queries):
        hd = u // 2
        ks = k_ref[pl.ds(pl.multiple_of(j * tk, tk), tk), hd * LANES:(hd + 1) * LANES]
        return lax.dot_general(ks, queries[u], (((1,), (1,)), ((), ())),
                               preferred_element_type=F32)

    ones_rows = jnp.ones((SUM_ROWS, tk), BF16)

    def weighted_values(j, p_ref, u):
        hd = u // 2
        vts = vt_ref[hd * dv:(hd + 1) * dv, pl.ds(pl.multiple_of(j * tk, tk), tk)]
        vts = jnp.concatenate([vts, ones_rows], axis=0)
        acc_scr[u] = alpha_scr[u] * acc_scr[u] + jnp.dot(vts, p_ref[u],
                                                         preferred_element_type=F32)

    def put_scores(j, s_ref, mx_ref, u, queries):
        s = scores(j, u, queries)
        s_ref[u] = s
        mx_ref[u] = jnp.max(s, axis=0, keepdims=True)

    def tile_step(j, s_cur, mx_cur, s_nxt, mx_nxt, p_cur, p_prev, first=False, last=False):
        for u in range(n_streams):
            if not first:
                weighted_values(j - 1, p_prev, u)
            if not last:
                put_scores(j + 1, s_nxt, mx_nxt, u, qc)
            else:
                put_scores(0, s_nxt, mx_nxt, u, qc_next)
        for u in range(n_streams):
            m_old = m_scr[u]
            m_new = jnp.maximum(m_old, mx_cur[u])
            alpha = jnp.exp2(m_old - m_new)
            p = jnp.exp2(s_cur[u] - m_new)
            p_cur[u] = p.astype(BF16)
            alpha_scr[u] = alpha
            m_scr[u] = m_new

    qc_next = component_queries(qn_ref)

    @pl.when(pl.program_id(2) == 0)
    def _():
        for u in range(n_streams):
            put_scores(0, s_even, mx_even, u, qc)

    unroll = TILES_PER_ITER if n_kv % TILES_PER_ITER == 0 else 2

    def steps(base, head=False, tail=False):
        for t in range(0, unroll, 2):
            tile_step(base + t, s_even, mx_even, s_odd, mx_odd, p_even, p_odd,
                      first=head and t == 0)
            tile_step(base + t + 1, s_odd, mx_odd, s_even, mx_even, p_odd, p_even,
                      last=tail and t + 2 == unroll)

    def body(i, carry):
        steps(unroll * i)
        return carry

    n_groups = n_kv // unroll
    if n_groups == 1:
        steps(0, head=True, tail=True)
    else:
        steps(0, head=True)
        lax.fori_loop(1, n_groups - 1, body, 0)
        steps(n_kv - unroll, tail=True)
    for u in range(n_streams):
        weighted_values(n_kv - 1, p_odd, u)

    lam = (jnp.exp(jnp.sum(lq1_ref[...] * lk1_ref[...], axis=-1, keepdims=True))
           - jnp.exp(jnp.sum(lq2_ref[...] * lk2_ref[...], axis=-1, keepdims=True)) + lambda_init)
    for hd in range(HEADS_PER_STEP):
        u1, u2 = 2 * hd, 2 * hd + 1
        o = (acc_scr[u1, :dv, :] / acc_scr[u1, dv:dv + 1, :]
             - lam * (acc_scr[u2, :dv, :] / acc_scr[u2, dv:dv + 1, :]))
        o = o * lax.rsqrt(jnp.mean(o * o, axis=0, keepdims=True) + SUBLN_EPS)
        o = o * (sw_ref[...] * (1.0 - lambda_init))
        o_ref[:, hd * dv:(hd + 1) * dv] = o.T.astype(o_ref.dtype)


def _diff_attn(q, k, vt, lq1, lk1, lq2, lk2, subln_w, lambda_init, tq, tk):
    B, S, QW = q.shape
    H = QW // LANES
    DV = vt.shape[1] // H
    hps = HEADS_PER_STEP
    ns = 2 * hps
    const = lambda b, h, i: (0, 0)
    return pl.pallas_call(
        functools.partial(_attn_kernel, tk=tk, lambda_init=lambda_init),
        grid=(B, H // hps, S // tq),
        in_specs=[
            pl.BlockSpec((None, tq, hps * LANES), lambda b, h, i: (b, i, h)),
            pl.BlockSpec((None, tq, hps * LANES),
                         lambda b, h, i: (b, jnp.minimum(i + 1, S // tq - 1), h)),
            pl.BlockSpec((None, S, hps * LANES), lambda b, h, i: (b, 0, h)),
            pl.BlockSpec((None, hps * DV, S), lambda b, h, i: (b, h, 0)),
            pl.BlockSpec((1, HEAD), const),
            pl.BlockSpec((1, HEAD), const),
            pl.BlockSpec((1, HEAD), const),
            pl.BlockSpec((1, HEAD), const),
            pl.BlockSpec((DV, 1), const),
        ],
        out_specs=pl.BlockSpec((None, tq, hps * DV), lambda b, h, i: (b, i, h)),
        out_shape=jax.ShapeDtypeStruct((B, S, H * DV), BF16),
        scratch_shapes=[pltpu.VMEM((ns, 1, tq), F32),
                        pltpu.VMEM((ns, 1, tq), F32),
                        pltpu.VMEM((ns, DV + SUM_ROWS, tq), F32),
                        pltpu.VMEM((ns, tk, tq), F32),
                        pltpu.VMEM((ns, tk, tq), F32),
                        pltpu.VMEM((ns, tk, tq), BF16),
                        pltpu.VMEM((ns, tk, tq), BF16),
                        pltpu.VMEM((ns, 1, tq), F32),
                        pltpu.VMEM((ns, 1, tq), F32)],
        compiler_params=pltpu.CompilerParams(
            dimension_semantics=("parallel", "parallel", "arbitrary"),
            vmem_limit_bytes=VMEM_LIMIT),
        name="diff_attn",
    )(q, q, k, vt, lq1, lk1, lq2, lk2, subln_w)


def _merge_kernel(x_ref, y_ref, bonus_ref, g_ref, ob_ref, nm_ref, wg_ref, lnw_ref, lnb_ref,
                  e_ref, woa_ref, wob_ref, wout_ref, o_ref):
    x = x_ref[...]
    D = x.shape[-1]
    h = _rms(x, nm_ref[...], RMS_EPS).astype(BF16)
    n_in = wg_ref.shape[1]
    gates = _sigmoid(jnp.dot(h, wg_ref[:, n_in - 2 * D:], preferred_element_type=F32))
    e = e_ref[...]
    y = y_ref[0] + y_ref[1]
    mean = _seg_sum(y, e) * (1.0 / HEAD)
    yc = y - mean
    var = _seg_sum(yc * yc, e) * (1.0 / HEAD)
    yn = yc * lax.rsqrt(var + GN_EPS) * lnw_ref[...] + lnb_ref[...]
    ya = _mm((yn + bonus_ref[...]) * g_ref[...], woa_ref[...])
    yb = jnp.dot(ob_ref[...], wob_ref[...], preferred_element_type=F32)
    merged = gates[:, :D] * ya + gates[:, D:] * yb
    o_ref[...] = x + _mm(merged, wout_ref[...])


def _merge(x, y, bonus, g, ob, norm_mix, w_g, ln_w, ln_b, e_seg, w_oa, w_ob, w_out, tm):
    B, S, D = x.shape
    W = bonus.shape[-1]
    const = lambda b, i: (0, 0)
    row = lambda b, i: (b, i, 0)
    return pl.pallas_call(
        _merge_kernel,
        grid=(B, S // tm),
        in_specs=[
            pl.BlockSpec((None, tm, D), row),
            pl.BlockSpec((2, None, tm, W), lambda b, i: (0, b, i, 0)),
            pl.BlockSpec((None, tm, W), row),
            pl.BlockSpec((None, tm, W), row),
            pl.BlockSpec((None, tm, W), row),
            pl.BlockSpec((1, D), const),
            pl.BlockSpec(w_g.shape, const),
            pl.BlockSpec((1, W), const),
            pl.BlockSpec((1, W), const),
            pl.BlockSpec(e_seg.shape, const),
            pl.BlockSpec((W, D), const),
            pl.BlockSpec((W, D), const),
            pl.BlockSpec((D, D), const),
        ],
        out_specs=pl.BlockSpec((None, tm, D), row),
        out_shape=jax.ShapeDtypeStruct((B, S, D), F32),
        compiler_params=pltpu.CompilerParams(
            dimension_semantics=("parallel", "parallel"), vmem_limit_bytes=VMEM_LIMIT),
        name="merge",
    )(x, y, bonus, g, ob, norm_mix, w_g, ln_w, ln_b, e_seg, w_oa, w_ob, w_out)


def _ffn_kernel(x_ref, p_ref, nf_ref, w1_ref, w2_ref, np_ref, wpg_ref, wpp_ref, nfin_ref, o_ref,
                *, n_chunk):
    x = x_ref[...]
    h = _rms(x, nf_ref[...], RMS_EPS).astype(BF16)
    ff = w1_ref.shape[1] // n_chunk
    acc = x
    for c in range(n_chunk):
        t = jnp.maximum(jnp.dot(h, w1_ref[:, c * ff:(c + 1) * ff], preferred_element_type=F32), 0.0)
        acc = acc + jnp.dot((t * t).astype(BF16), w2_ref[c * ff:(c + 1) * ff, :],
                            preferred_element_type=F32)
    x = acc
    h = _rms(x, np_ref[...], RMS_EPS).astype(BF16)
    gate = _sigmoid(jnp.dot(h, wpg_ref[...], preferred_element_type=F32))
    x = x + gate * _mm(p_ref[...], wpp_ref[...])
    o_ref[...] = x if nfin_ref is None else _rms(x, nfin_ref[...], RMS_EPS)


def _ffn_kernel_plain(x_ref, p_ref, nf_ref, w1_ref, w2_ref, np_ref, wpg_ref, wpp_ref, o_ref, *, n_chunk):
    _ffn_kernel(x_ref, p_ref, nf_ref, w1_ref, w2_ref, np_ref, wpg_ref, wpp_ref, None, o_ref,
                n_chunk=n_chunk)


def _ffn(x, p, norm_ffn, w1, w2, norm_ple, w_pg, w_pp, norm_final, tm, n_chunk):
    B, S, D = x.shape
    PD = p.shape[-1]
    FF = w1.shape[1]
    const = lambda b, i: (0, 0)
    row = lambda b, i: (b, i, 0)
    in_specs = [
        pl.BlockSpec((None, tm, D), row),
        pl.BlockSpec((None, tm, PD), row),
        pl.BlockSpec((1, D), const),
        pl.BlockSpec((D, FF), const, pipeline_mode=pl.Buffered(1)),
        pl.BlockSpec((FF, D), const, pipeline_mode=pl.Buffered(1)),
        pl.BlockSpec((1, D), const),
        pl.BlockSpec((D, D), const, pipeline_mode=pl.Buffered(1)),
        pl.BlockSpec((PD, D), const, pipeline_mode=pl.Buffered(1)),
    ]
    args = [x, p, norm_ffn, w1, w2, norm_ple, w_pg, w_pp]
    if norm_final is not None:
        in_specs.append(pl.BlockSpec((1, D), const))
        args.append(norm_final)
        body = functools.partial(_ffn_kernel, n_chunk=n_chunk)
    else:
        body = functools.partial(_ffn_kernel_plain, n_chunk=n_chunk)
    return pl.pallas_call(
        body,
        grid=(B, S // tm),
        in_specs=in_specs,
        out_specs=pl.BlockSpec((None, tm, D), row),
        out_shape=jax.ShapeDtypeStruct((B, S, D), F32),
        compiler_params=pltpu.CompilerParams(
            dimension_semantics=("parallel", "parallel"), vmem_limit_bytes=VMEM_LIMIT),
        name="ffn",
    )(*args)


def _block_diag2(w):
    z = jnp.zeros_like(w[0])
    return jnp.concatenate([jnp.concatenate([w[0], z], axis=1),
                            jnp.concatenate([z, w[1]], axis=1)], axis=0)


def kernel(x, p, norm_mix, w_in, shift_mu_prev, shift_mu_next, rwkv_w0, rwkv_w2, rwkv_a0,
           rwkv_a2, rwkv_g2, rwkv_k_k, rwkv_k_a, rwkv_r_k, rwkv_ln_w, rwkv_ln_b, rwkv_w_o,
           da_lq1, da_lk1, da_lq2, da_lk2, da_subln_w, da_w_o, w_out, norm_ffn, w_ff1, w_ff2,
           norm_ple, w_ple_gate, w_ple_proj, norm_final):
    B, S, D = x.shape
    L = w_in.shape[0]
    W = rwkv_w0.shape[-1]
    n_r = shift_mu_prev.shape[-1]
    n_qk = 2 * da_w_o.shape[1]
    n_v = da_w_o.shape[1]
    cos_t, sin_t = _rope_tables(S)
    seg = jnp.arange(SEG_BLOCK) // HEAD
    e_seg = (seg[:, None] == seg[None, :]).astype(BF16)
    tm = min(256, S)
    for i in range(L):
        lambda_init = 0.8 - 0.6 * math.exp(-0.3 * i)
        w_i = w_in[i].astype(BF16)
        u_r, q, k, v = _in_proj(
            x, norm_mix[i][None], w_i, w_i[:, n_r + n_qk:n_r + n_qk + n_v].T, n_r, n_qk,
            cos_t, sin_t, min(512, S))
        r, vv, kk, g, bonus, lw, bb, kd = _rwkv_prep(
            u_r, shift_mu_prev[i][None], shift_mu_next[i][None],
            rwkv_w0[i].reshape(1, 2 * W), _block_diag2(rwkv_w2[i]).astype(BF16),
            rwkv_a0[i].reshape(1, 2 * W), _block_diag2(rwkv_a2[i]).astype(BF16),
            rwkv_g2[i].astype(BF16), rwkv_k_k[i][None], rwkv_k_a[i][None],
            rwkv_r_k[i].reshape(1, W), e_seg, tm)
        y = _rwkv_scan(r, vv, kk, lw, bb, kd, nsub=min(16, S // CHUNK))
        ob = _diff_attn(q, k, v, da_lq1[i][None], da_lk1[i][None], da_lq2[i][None],
                        da_lk2[i][None], da_subln_w[i][:, None], lambda_init,
                        tq=min(512, S), tk=min(512, S // 2))
        x = _merge(x, y, bonus, g, ob, norm_mix[i][None],
                   w_i, rwkv_ln_w[i][None], rwkv_ln_b[i][None], e_seg,
                   rwkv_w_o[i].astype(BF16), da_w_o[i].astype(BF16), w_out[i].astype(BF16),
                   min(512, S))
        x = _ffn(x, p[i], norm_ffn[i][None], w_ff1[i].astype(BF16), w_ff2[i].astype(BF16),
                 norm_ple[i][None], w_ple_gate[i].astype(BF16), w_ple_proj[i].astype(BF16),
                 norm_final[None] if i == L - 1 else None, min(512, S), n_chunk=4)
    return x
```

```python
import functools
import math

import jax
import jax.numpy as jnp
from jax import lax
from jax.experimental import pallas as pl
from jax.experimental.pallas import tpu as pltpu

F32 = jnp.float32
BF16 = jnp.bfloat16

HEAD = 64
LANES = 128
CHUNK = 64
RMS_EPS = 1e-6
GN_EPS = 64e-5
SUBLN_EPS = 1e-5
ROPE_THETA = 500000.0
ROPE_DIM = 16
Q_SCALE = HEAD ** -0.5 * math.log2(math.e)
DECAY_SCALE = math.exp(-0.5)
VMEM_LIMIT = 56 * 1024 * 1024
TOKEN_TILE = 512
ATTN_TILE = 512
SCAN_CHUNKS_PER_STEP = 16
FFN_CHUNKS = 4
TILES_PER_ITER = 8
HEADS_PER_STEP = 1
SUM_ROWS = 16
HALO = 8
SEG_BLOCK = 256


def _rms(x, g, eps):
    return x * lax.rsqrt(jnp.mean(x * x, axis=-1, keepdims=True) + eps) * g


def _sigmoid(x):
    return 1.0 / (1.0 + jnp.exp(-x))


def _mm(a, b):
    return jnp.dot(a.astype(BF16), b.astype(BF16), preferred_element_type=F32)


def _mm_nt(a, b):
    return lax.dot_general(a.astype(BF16), b.astype(BF16), (((1,), (1,)), ((), ())),
                           preferred_element_type=F32)


def _split2(x):
    hi = x.astype(BF16)
    lo = (x - hi.astype(F32)).astype(BF16)
    return hi, lo


def _mm_exact_rhs(a, b_bf16):
    hi, lo = _split2(a)
    return (jnp.dot(hi, b_bf16, preferred_element_type=F32)
            + jnp.dot(lo, b_bf16, preferred_element_type=F32))


def _seg_sum(x, e):
    n = e.shape[0]
    return jnp.concatenate([_mm_exact_rhs(x[:, c:c + n], e) for c in range(0, x.shape[1], n)],
                           axis=1)


def _mm3(a, b):
    ah, al = _split2(a)
    bh, bl = _split2(b)
    return (jnp.dot(ah, bh, preferred_element_type=F32)
            + jnp.dot(ah, bl, preferred_element_type=F32)
            + jnp.dot(al, bh, preferred_element_type=F32))


def _inproj_kernel(x_ref, g_ref, w_ref, wv_ref, cos_ref, sin_ref,
                   ur_ref, q_ref, k_ref, v_ref):
    n_r = ur_ref.shape[1]
    n_qk = 2 * q_ref.shape[1]
    h = _rms(x_ref[...], g_ref[...], RMS_EPS).astype(BF16)
    ur_ref[...] = jnp.dot(h, w_ref[:, :n_r], preferred_element_type=F32)
    v_ref[...] = lax.dot_general(wv_ref[...], h, (((1,), (1,)), ((), ())),
                                 preferred_element_type=F32).astype(BF16)
    cos_t = cos_ref[...]
    sin_t = sin_ref[...]
    lane = lax.broadcasted_iota(jnp.int32, cos_t.shape, 1)
    first_half = (lane & (HEAD - 1)) < (ROPE_DIM // 2)
    n_tiles = n_qk // LANES
    qk = jnp.dot(h, w_ref[:, n_r:n_r + n_qk], preferred_element_type=F32)
    for c in range(n_tiles):
        xq = qk[:, c * LANES:(c + 1) * LANES]
        partner = jnp.where(first_half,
                            pltpu.roll(xq, LANES - ROPE_DIM // 2, 1),
                            pltpu.roll(xq, ROPE_DIM // 2, 1))
        ro = xq * cos_t + partner * sin_t
        if c < n_tiles // 2:
            q_ref[:, c * LANES:(c + 1) * LANES] = (ro * Q_SCALE).astype(BF16)
        else:
            cc = c - n_tiles // 2
            k_ref[:, cc * LANES:(cc + 1) * LANES] = ro.astype(BF16)


def _rope_tables(S):
    pos = jnp.arange(S, dtype=F32)
    inv_freq = ROPE_THETA ** (-jnp.arange(0, ROPE_DIM, 2, dtype=F32) / ROPE_DIM)
    ang = pos[:, None] * inv_freq[None, :]
    cos8, sin8 = jnp.cos(ang), jnp.sin(ang)
    ones = jnp.ones((S, HEAD - ROPE_DIM), F32)
    zeros = jnp.zeros((S, HEAD - ROPE_DIM), F32)
    cos64 = jnp.concatenate([cos8, cos8, ones], axis=1)
    sin64 = jnp.concatenate([-sin8, sin8, zeros], axis=1)
    return jnp.tile(cos64, (1, 2)), jnp.tile(sin64, (1, 2))


def _in_proj(x, g, w_all, w_vt, nr, nqk, cos_t, sin_t, tm):
    B, S, D = x.shape
    nv = w_vt.shape[0]
    const = lambda b, i: (0, 0)
    row = lambda b, i: (b, i, 0)
    return pl.pallas_call(
        _inproj_kernel,
        grid=(B, S // tm),
        in_specs=[
            pl.BlockSpec((None, tm, D), row),
            pl.BlockSpec((1, D), const),
            pl.BlockSpec(w_all.shape, const),
            pl.BlockSpec((nv, D), const),
            pl.BlockSpec((tm, LANES), lambda b, i: (i, 0)),
            pl.BlockSpec((tm, LANES), lambda b, i: (i, 0)),
        ],
        out_specs=[
            pl.BlockSpec((None, tm, nr), row),
            pl.BlockSpec((None, tm, nqk // 2), row),
            pl.BlockSpec((None, tm, nqk // 2), row),
            pl.BlockSpec((None, nv, tm), lambda b, i: (b, 0, i)),
        ],
        out_shape=[
            jax.ShapeDtypeStruct((B, S, nr), F32),
            jax.ShapeDtypeStruct((B, S, nqk // 2), BF16),
            jax.ShapeDtypeStruct((B, S, nqk // 2), BF16),
            jax.ShapeDtypeStruct((B, nv, S), BF16),
        ],
        compiler_params=pltpu.CompilerParams(
            dimension_semantics=("parallel", "parallel"), vmem_limit_bytes=VMEM_LIMIT),
        name="in_proj",
    )(x, g, w_all, w_vt, cos_t, sin_t)


def _prep_kernel(u_ref, up_ref, un_ref, mup_ref, mun_ref, w0_ref, w2_ref, a0_ref, a2_ref,
                 g2_ref, kk_w_ref, ka_ref, rk_ref, e_ref,
                 r_ref, v_ref, kk_ref, g_ref, bonus_ref, lw_ref, b_ref, kd_ref):
    i = pl.program_id(1)
    n = pl.num_programs(1)
    u = u_ref[...]
    tm = u.shape[0]
    W = r_ref.shape[-1]
    row8 = lax.broadcasted_iota(jnp.int32, (HALO, 1), 0)
    prev_row = jnp.where(i > 0, up_ref[HALO - 1:HALO, :], 0.0)
    next_row = jnp.where(i < n - 1, un_ref[0:1, :], 0.0)
    rolled = pltpu.roll(u, 1, 0)
    u_prev = jnp.concatenate([jnp.where(row8 == 0, prev_row, rolled[:HALO]), rolled[HALO:]], axis=0)
    rolled = pltpu.roll(u, tm - 1, 0)
    u_next = jnp.concatenate([rolled[:tm - HALO],
                              jnp.where(row8 == HALO - 1, next_row, rolled[tm - HALO:])], axis=0)
    mup = mup_ref[...]
    mun = mun_ref[...]
    us = (1.0 - mup - mun) * u + mup * u_prev + mun * u_next

    r = us[:, 0:W]
    k = us[:, W:2 * W]
    v = us[:, 2 * W:3 * W]
    wd = us[:, 3 * W:3 * W + LANES]
    ad = us[:, 3 * W + LANES:3 * W + 2 * LANES]
    gd = us[:, 3 * W + 2 * LANES:3 * W + 3 * LANES]

    z = w0_ref[...] + _mm(jnp.tanh(wd), w2_ref[...])
    lw = -(_sigmoid(z) * DECAY_SCALE)
    a = _sigmoid(a0_ref[...] + _mm(ad, a2_ref[...]))
    g_ref[...] = _mm(_sigmoid(gd), g2_ref[...]).astype(g_ref.dtype)

    e = e_ref[...]
    kraw = k * kk_w_ref[...]
    ss = _seg_sum(kraw * kraw, e)
    kk = kraw / jnp.maximum(jnp.sqrt(ss), 1e-12)
    ka = ka_ref[...]
    kd_sum = jnp.zeros_like(k)
    for d in range(2):
        a_d = a[:, d * W:(d + 1) * W]
        kd = k * (1.0 + (a_d - 1.0) * ka)
        kd_sum = kd_sum + kd
        lw_ref[d] = lw[:, d * W:(d + 1) * W]
        b_ref[d] = (kk * a_d).astype(b_ref.dtype)
        kd_ref[d] = kd.astype(kd_ref.dtype)
    c = _seg_sum(r * rk_ref[...] * kd_sum, e)
    r_ref[...] = r.astype(r_ref.dtype)
    v_ref[...] = v.astype(v_ref.dtype)
    kk_ref[...] = kk.astype(kk_ref.dtype)
    bonus_ref[...] = (c * v).astype(bonus_ref.dtype)


def _rwkv_prep(u_r, mu_prev, mu_next, w0, w2bd, a0, a2bd, g2, k_k, k_a, r_k, e_seg, tm):
    B, S, C = u_r.shape
    W = k_k.shape[-1]
    nblk8 = S // 8
    const = lambda b, i: (0, 0)
    row = lambda b, i: (b, i, 0)
    drow = lambda b, i: (0, b, i, 0)
    tok16 = jax.ShapeDtypeStruct((B, S, W), BF16)
    dtok = jax.ShapeDtypeStruct((2, B, S, W), F32)
    dtok16 = jax.ShapeDtypeStruct((2, B, S, W), BF16)
    return pl.pallas_call(
        _prep_kernel,
        grid=(B, S // tm),
        in_specs=[
            pl.BlockSpec((None, tm, C), row),
            pl.BlockSpec((None, 8, C), lambda b, i: (b, jnp.maximum(i * (tm // 8) - 1, 0), 0)),
            pl.BlockSpec((None, 8, C), lambda b, i: (b, jnp.minimum((i + 1) * (tm // 8), nblk8 - 1), 0)),
            pl.BlockSpec((1, C), const),
            pl.BlockSpec((1, C), const),
            pl.BlockSpec((1, 2 * W), const),
            pl.BlockSpec((LANES, 2 * W), const),
            pl.BlockSpec((1, 2 * W), const),
            pl.BlockSpec((LANES, 2 * W), const),
            pl.BlockSpec((LANES, W), const),
            pl.BlockSpec((1, W), const),
            pl.BlockSpec((1, W), const),
            pl.BlockSpec((1, W), const),
            pl.BlockSpec(e_seg.shape, const),
        ],
        out_specs=[pl.BlockSpec((None, tm, W), row)] * 5
                  + [pl.BlockSpec((2, None, tm, W), drow)] * 3,
        out_shape=[tok16, tok16, tok16, tok16, tok16, dtok, dtok16, dtok16],
        compiler_params=pltpu.CompilerParams(
            dimension_semantics=("parallel", "parallel"), vmem_limit_bytes=VMEM_LIMIT),
        name="rwkv_prep",
    )(u_r, u_r, u_r, mu_prev, mu_next, w0, w2bd, a0, a2bd, g2, k_k, k_a, r_k, e_seg)


def _scan_kernel(r_ref, v_ref, kk_ref, lw_ref, b_ref, kd_ref, y_ref,
                 t_scr, g_scr, h_scr, q_scr, yl_scr, *, nsub, n_pair, n_step):
    step = pl.program_id(0)
    last = pl.num_programs(0) - 2
    row_cur = jnp.minimum(step, last) // n_step
    step_prev = jnp.maximum(step - 1, 0)
    rev = (row_cur // n_pair) % 2
    rev_prev = ((step_prev // n_step) // n_pair) % 2
    sgn = 1 - 2 * rev
    C = CHUNK
    P2 = 2 * C

    @pl.when(step == 0)
    def _():
        t_scr[...] = jnp.zeros_like(t_scr)
        g_scr[...] = jnp.zeros_like(g_scr)
        h_scr[...] = jnp.zeros_like(h_scr)
        q_scr[...] = jnp.zeros_like(q_scr)
        yl_scr[...] = jnp.zeros_like(yl_scr)

    def slices(direction):
        out = []
        for s in range(nsub):
            ci = s + direction * (nsub - 1 - 2 * s)
            out.append(pl.ds(pl.multiple_of(ci * C, C), C))
        return out

    chunk_slices = slices(rev)
    chain_slices = slices(rev_prev)

    chain = {"t": jnp.where(step_prev % n_step == 0, 0.0, t_scr[...]), "next": 0}

    def chain_steps(n):
        for _ in range(n):
            s = chain["next"]
            if s == nsub:
                return
            t_cur = chain["t"]
            t16 = t_cur.astype(BF16)
            y_st = jnp.dot(q_scr[s], t16, preferred_element_type=F32) + yl_scr[s]
            chain["t"] = jnp.dot(g_scr[s], t16, preferred_element_type=F32) + h_scr[s]
            y_ref[chain_slices[s], :] = y_st[:C, :] + y_st[C:, :]
            chain["next"] = s + 1

    row = lax.broadcasted_iota(jnp.int32, (P2, P2), 0)
    col = lax.broadcasted_iota(jnp.int32, (P2, P2), 1)
    rt = row & (C - 1)
    ct = col & (C - 1)
    delta = (rt - ct) * sgn
    strict = delta > 0
    incl = delta >= 0
    eye = row == col
    ident = jnp.where(eye, 1.0, 0.0)
    blk8 = (rt >> 3) == (ct >> 3)
    blk16 = (rt >> 4) == (ct >> 4)
    blk32 = (rt >> 5) == (ct >> 5)
    in8 = strict & blk8
    in16 = strict & blk16 & ~blk8
    in32 = strict & blk32 & ~blk16
    in64 = strict & ~blk32
    lane = lax.broadcasted_iota(jnp.int32, (C, LANES), 1)
    head0 = lane < HEAD

    def stack(x):
        return jnp.concatenate([jnp.where(head0, x, 0.0), jnp.where(head0, 0.0, x)], axis=0)

    subs = range(nsub)
    sls = chunk_slices

    lws = [lw_ref[sl, :] for sl in sls]
    tots = [jnp.sum(lw, axis=0, keepdims=True) for lw in lws]
    tok = lax.broadcasted_iota(jnp.int32, (C, LANES), 0)
    cums = []
    for lw, tot in zip(lws, tots):
        pre = lw
        shift = 1
        while shift < C:
            pre = pre + jnp.where(tok >= shift, pltpu.roll(pre, shift, 0), 0.0)
            shift *= 2
        cums.append(jnp.where(rev == 1, tot - pre + lw, pre))
    a_st, r_st, b_st, k_st, bh_t, kh_t, v_st = [], [], [], [], [], [], []
    for s in subs:
        sl, lw, cum, tot = sls[s], lws[s], cums[s], tots[s]
        e_neg = jnp.exp(-cum)
        e_rem = jnp.exp(tot - cum)
        b = b_ref[sl, :]
        kd = kd_ref[sl, :]
        a_st.append(stack(kk_ref[sl, :] * -jnp.exp(cum - lw)).astype(BF16))
        r_st.append(stack(r_ref[sl, :] * jnp.exp(cum)))
        b_st.append(stack(b * e_neg).astype(BF16))
        k_st.append(stack(kd * e_neg).astype(BF16))
        bh_t.append(stack(b * e_rem).T.astype(BF16))
        kh_t.append(stack(kd * e_rem).T.astype(BF16))
        v_st.append(stack(v_ref[sl, :]).astype(BF16))

    nt = (((1,), (1,)), ((), ()))
    sc = [lax.dot_general(jnp.concatenate([a_st[s], r_st[s].astype(BF16)], axis=0),
                          jnp.concatenate([b_st[s], k_st[s]], axis=0), nt,
                          preferred_element_type=F32) for s in subs]
    per_stage = -(-nsub // 8)
    chain_steps(per_stage)
    l_ab = [x[:P2, :P2] for x in sc]
    l_ak = [jnp.where(strict, x[:P2, P2:], 0.0).astype(BF16) for x in sc]
    a_rb = [jnp.where(incl, x[P2:, :P2], 0.0).astype(BF16) for x in sc]
    a_rk = [jnp.where(incl, x[P2:, P2:], 0.0).astype(BF16) for x in sc]
    x_loc = [jnp.dot(l_ak[s], v_st[s], preferred_element_type=F32) for s in subs]
    chain_steps(per_stage)

    d8 = [jnp.where(in8, x, 0.0) for x in l_ab]
    m1 = [ident + x for x in d8]
    p1 = [_mm(x, x) for x in d8]
    chain_steps(per_stage)
    pm = [_mm(p1[s], jnp.concatenate([p1[s], m1[s]], axis=1)) for s in subs]
    m2 = [m1[s] + pm[s][:, P2:] for s in subs]
    m8 = [m2[s] + _mm(pm[s][:, :P2], m2[s]) for s in subs]
    chain_steps(per_stage)
    o16 = [jnp.where(in16, x, 0.0) for x in l_ab]
    om = [_mm(o16[s], m8[s]) for s in subs]
    m16 = [m8[s] + _mm(m8[s], om[s]) for s in subs]
    chain_steps(per_stage)
    o32 = [jnp.where(in32, x, 0.0) for x in l_ab]
    om = [_mm(o32[s], m16[s]) for s in subs]
    m32 = [(m16[s] + _mm(m16[s], om[s])).astype(BF16) for s in subs]
    chain_steps(per_stage)
    o64 = [jnp.where(in64, x, 0.0).astype(BF16) for x in l_ab]
    y1 = [jnp.dot(m32[s], jnp.concatenate([a_st[s], x_loc[s].astype(BF16)], axis=1),
                  preferred_element_type=F32) for s in subs]
    oy = [_mm(o64[s], y1[s]).astype(BF16) for s in subs]
    chain_steps(per_stage)
    wu16 = [(y1[s] + jnp.dot(m32[s], oy[s], preferred_element_type=F32)).astype(BF16)
            for s in subs]
    chain_steps(per_stage)
    zero_blk = jnp.zeros((P2, LANES), BF16)
    fin = [jnp.dot(jnp.concatenate([jnp.concatenate([a_rb[s], a_rk[s]], axis=1),
                                    jnp.concatenate([bh_t[s], kh_t[s]], axis=1)], axis=0),
                   jnp.concatenate([wu16[s], jnp.concatenate([zero_blk, v_st[s]], axis=1)], axis=0),
                   preferred_element_type=F32) for s in subs]
    chain_steps(nsub)
    t_scr[...] = chain["t"]
    for s in subs:
        q_scr[s] = (r_st[s] + fin[s][:P2, :LANES]).astype(BF16)
        yl_scr[s] = fin[s][:P2, LANES:]
        g_scr[s] = (jnp.where(eye, jnp.exp(tots[s]), 0.0) + fin[s][P2:, :LANES]).astype(BF16)
        h_scr[s] = fin[s][P2:, LANES:]


def _rwkv_scan(r, v, kk, lw, bb, kd, nsub):
    B, S, W = r.shape
    n_pair = W // LANES
    T = nsub * CHUNK
    n_step = S // T

    n_total = B * 2 * n_pair * n_step

    def locate(t):
        g = t // n_step
        c = t % n_step
        d = (g // n_pair) % 2
        return d, g // (2 * n_pair), c + d * (n_step - 1 - 2 * c), g % n_pair

    def shared(t):
        _, b, blk, p = locate(jnp.minimum(t, n_total - 1))
        return (b, blk, p)

    def per_dir(t):
        return locate(jnp.minimum(t, n_total - 1))

    def out_map(t):
        return locate(jnp.maximum(t - 1, 0))

    mat = pltpu.VMEM((nsub, LANES, LANES), F32)
    mat16 = pltpu.VMEM((nsub, LANES, LANES), BF16)
    return pl.pallas_call(
        functools.partial(_scan_kernel, nsub=nsub, n_pair=n_pair, n_step=n_step),
        grid=(n_total + 1,),
        in_specs=[pl.BlockSpec((None, T, LANES), shared)] * 3
                 + [pl.BlockSpec((None, None, T, LANES), per_dir)] * 3,
        out_specs=pl.BlockSpec((None, None, T, LANES), out_map),
        out_shape=jax.ShapeDtypeStruct((2, B, S, W), F32),
        scratch_shapes=[pltpu.VMEM((LANES, LANES), F32), mat16, mat, mat16, mat],
        compiler_params=pltpu.CompilerParams(
            dimension_semantics=("arbitrary",), vmem_limit_bytes=VMEM_LIMIT),
        name="rwkv_scan",
    )(r, v, kk, lw, bb, kd)


def _attn_kernel(q_ref, qn_ref, k_ref, vt_ref, lq1_ref, lk1_ref, lq2_ref, lk2_ref, sw_ref, o_ref,
                 m_scr, alpha_scr, acc_scr, s_even, s_odd, p_even, p_odd, mx_even, mx_odd,
                 *, tk, lambda_init):
    n_kv = k_ref.shape[0] // tk
    dv = vt_ref.shape[0] // HEADS_PER_STEP
    n_streams = 2 * HEADS_PER_STEP
    lane = lax.broadcasted_iota(jnp.int32, (q_ref.shape[0], LANES), 1)

    def component_queries(ref):
        out = []
        for hd in range(HEADS_PER_STEP):
            q = ref[:, hd * LANES:(hd + 1) * LANES]
            zero = jnp.zeros_like(q)
            out += [jnp.where(lane < HEAD, q, zero), jnp.where(lane < HEAD, zero, q)]
        return out

    qc = component_queries(q_ref)
    m_scr[...] = jnp.full(m_scr.shape, -jnp.inf, F32)
    acc_scr[...] = jnp.zeros(acc_scr.shape, F32)

    def scores(j, u, queries):
        hd = u // 2
        ks = k_ref[pl.ds(pl.multiple_of(j * tk, tk), tk), hd * LANES:(hd + 1) * LANES]
        return lax.dot_general(ks, queries[u], (((1,), (1,)), ((), ())),
                               preferred_element_type=F32)

    ones_rows = jnp.ones((SUM_ROWS, tk), BF16)

    def weighted_values(j, p_ref, u):
        hd = u // 2
        vts = vt_ref[hd * dv:(hd + 1) * dv, pl.ds(pl.multiple_of(j * tk, tk), tk)]
        vts = jnp.concatenate([vts, ones_rows], axis=0)
        acc_scr[u] = alpha_scr[u] * acc_scr[u] + jnp.dot(vts, p_ref[u],
                                                         preferred_element_type=F32)

    def put_scores(j, s_ref, mx_ref, u, queries):
        s = scores(j, u, queries)
        s_ref[u] = s
        mx_ref[u] = jnp.max(s, axis=0, keepdims=True)

    def tile_step(j, s_cur, mx_cur, s_nxt, mx_nxt, p_cur, p_prev, first=False, last=False):
        for u in range(n_streams):
            if not first:
                weighted_values(j - 1, p_prev, u)
            if not last:
                put_scores(j + 1, s_nxt, mx_nxt, u, qc)
        for u in range(n_streams):
            m_old = m_scr[u]
            m_new = jnp.maximum(m_old, mx_cur[u])
            alpha = jnp.exp2(m_old - m_new)
            p = jnp.exp2(s_cur[u] - m_new)
            p_cur[u] = p.astype(BF16)
            alpha_scr[u] = alpha
            m_scr[u] = m_new

    qc_next = component_queries(qn_ref)

    @pl.when(pl.program_id(2) == 0)
    def _():
        for u in range(n_streams):
            put_scores(0, s_even, mx_even, u, qc)

    unroll = TILES_PER_ITER if n_kv % TILES_PER_ITER == 0 else 2

    def steps(base, head=False, tail=False):
        for t in range(0, unroll, 2):
            tile_step(base + t, s_even, mx_even, s_odd, mx_odd, p_even, p_odd,
                      first=head and t == 0)
            tile_step(base + t + 1, s_odd, mx_odd, s_even, mx_even, p_odd, p_even,
                      last=tail and t + 2 == unroll)

    def body(i, carry):
        steps(unroll * i)
        return carry

    n_groups = n_kv // unroll
    if n_groups == 1:
        steps(0, head=True, tail=True)
    else:
        steps(0, head=True)
        lax.fori_loop(1, n_groups - 1, body, 0)
        steps(n_kv - unroll, tail=True)
    for u in range(n_streams):
        weighted_values(n_kv - 1, p_odd, u)
    for u in range(n_streams):
        put_scores(0, s_even, mx_even, u, qc_next)

    lam = (jnp.exp(jnp.sum(lq1_ref[...] * lk1_ref[...], axis=-1, keepdims=True))
           - jnp.exp(jnp.sum(lq2_ref[...] * lk2_ref[...], axis=-1, keepdims=True)) + lambda_init)
    for hd in range(HEADS_PER_STEP):
        u1, u2 = 2 * hd, 2 * hd + 1
        o = (acc_scr[u1, :dv, :] / acc_scr[u1, dv:dv + 1, :]
             - lam * (acc_scr[u2, :dv, :] / acc_scr[u2, dv:dv + 1, :]))
        o = o * lax.rsqrt(jnp.mean(o * o, axis=0, keepdims=True) + SUBLN_EPS)
        o = o * (sw_ref[...] * (1.0 - lambda_init))
        o_ref[:, hd * dv:(hd + 1) * dv] = o.T.astype(o_ref.dtype)


def _diff_attn(q, k, vt, lq1, lk1, lq2, lk2, subln_w, lambda_init, tq, tk):
    B, S, QW = q.shape
    H = QW // LANES
    DV = vt.shape[1] // H
    hps = HEADS_PER_STEP
    ns = 2 * hps
    const = lambda b, h, i: (0, 0)
    return pl.pallas_call(
        functools.partial(_attn_kernel, tk=tk, lambda_init=lambda_init),
        grid=(B, H // hps, S // tq),
        in_specs=[
            pl.BlockSpec((None, tq, hps * LANES), lambda b, h, i: (b, i, h)),
            pl.BlockSpec((None, tq, hps * LANES),
                         lambda b, h, i: (b, jnp.minimum(i + 1, S // tq - 1), h)),
            pl.BlockSpec((None, S, hps * LANES), lambda b, h, i: (b, 0, h)),
            pl.BlockSpec((None, hps * DV, S), lambda b, h, i: (b, h, 0)),
            pl.BlockSpec((1, HEAD), const),
            pl.BlockSpec((1, HEAD), const),
            pl.BlockSpec((1, HEAD), const),
            pl.BlockSpec((1, HEAD), const),
            pl.BlockSpec((DV, 1), const),
        ],
        out_specs=pl.BlockSpec((None, tq, hps * DV), lambda b, h, i: (b, i, h)),
        out_shape=jax.ShapeDtypeStruct((B, S, H * DV), BF16),
        scratch_shapes=[pltpu.VMEM((ns, 1, tq), F32),
                        pltpu.VMEM((ns, 1, tq), F32),
                        pltpu.VMEM((ns, DV + SUM_ROWS, tq), F32),
                        pltpu.VMEM((ns, tk, tq), F32),
                        pltpu.VMEM((ns, tk, tq), F32),
                        pltpu.VMEM((ns, tk, tq), BF16),
                        pltpu.VMEM((ns, tk, tq), BF16),
                        pltpu.VMEM((ns, 1, tq), F32),
                        pltpu.VMEM((ns, 1, tq), F32)],
        compiler_params=pltpu.CompilerParams(
            dimension_semantics=("parallel", "parallel", "arbitrary"),
            vmem_limit_bytes=VMEM_LIMIT),
        name="diff_attn",
    )(q, q, k, vt, lq1, lk1, lq2, lk2, subln_w)


def _merge_kernel(x_ref, y_ref, bonus_ref, g_ref, ob_ref, nm_ref, wg_ref, lnw_ref, lnb_ref,
                  e_ref, woa_ref, wob_ref, wout_ref, o_ref):
    x = x_ref[...]
    D = x.shape[-1]
    h = _rms(x, nm_ref[...], RMS_EPS).astype(BF16)
    n_in = wg_ref.shape[1]
    gates = _sigmoid(jnp.dot(h, wg_ref[:, n_in - 2 * D:], preferred_element_type=F32))
    e = e_ref[...]
    y = y_ref[0] + y_ref[1]
    mean = _seg_sum(y, e) * (1.0 / HEAD)
    yc = y - mean
    var = _seg_sum(yc * yc, e) * (1.0 / HEAD)
    yn = yc * lax.rsqrt(var + GN_EPS) * lnw_ref[...] + lnb_ref[...]
    ya = _mm((yn + bonus_ref[...]) * g_ref[...], woa_ref[...])
    yb = jnp.dot(ob_ref[...], wob_ref[...], preferred_element_type=F32)
    merged = gates[:, :D] * ya + gates[:, D:] * yb
    o_ref[...] = x + _mm(merged, wout_ref[...])


def _merge(x, y, bonus, g, ob, norm_mix, w_g, ln_w, ln_b, e_seg, w_oa, w_ob, w_out, tm):
    B, S, D = x.shape
    W = bonus.shape[-1]
    const = lambda b, i: (0, 0)
    row = lambda b, i: (b, i, 0)
    return pl.pallas_call(
        _merge_kernel,
        grid=(B, S // tm),
        in_specs=[
            pl.BlockSpec((None, tm, D), row),
            pl.BlockSpec((2, None, tm, W), lambda b, i: (0, b, i, 0)),
            pl.BlockSpec((None, tm, W), row),
            pl.BlockSpec((None, tm, W), row),
            pl.BlockSpec((None, tm, W), row),
            pl.BlockSpec((1, D), const),
            pl.BlockSpec(w_g.shape, const),
            pl.BlockSpec((1, W), const),
            pl.BlockSpec((1, W), const),
            pl.BlockSpec(e_seg.shape, const),
            pl.BlockSpec((W, D), const),
            pl.BlockSpec((W, D), const),
            pl.BlockSpec((D, D), const),
        ],
        out_specs=pl.BlockSpec((None, tm, D), row),
        out_shape=jax.ShapeDtypeStruct((B, S, D), F32),
        compiler_params=pltpu.CompilerParams(
            dimension_semantics=("parallel", "parallel"), vmem_limit_bytes=VMEM_LIMIT),
        name="merge",
    )(x, y, bonus, g, ob, norm_mix, w_g, ln_w, ln_b, e_seg, w_oa, w_ob, w_out)


def _ffn_kernel(x_ref, p_ref, nf_ref, w1_ref, w2_ref, np_ref, wpg_ref, wpp_ref, nfin_ref, o_ref,
                *, n_chunk):
    x = x_ref[...]
    h = _rms(x, nf_ref[...], RMS_EPS).astype(BF16)
    ff = w1_ref.shape[1] // n_chunk
    acc = x
    for c in range(n_chunk):
        t = jnp.maximum(jnp.dot(h, w1_ref[:, c * ff:(c + 1) * ff], preferred_element_type=F32), 0.0)
        acc = acc + jnp.dot((t * t).astype(BF16), w2_ref[c * ff:(c + 1) * ff, :],
                            preferred_element_type=F32)
    x = acc
    h = _rms(x, np_ref[...], RMS_EPS).astype(BF16)
    gate = _sigmoid(jnp.dot(h, wpg_ref[...], preferred_element_type=F32))
    x = x + gate * _mm(p_ref[...], wpp_ref[...])
    o_ref[...] = x if nfin_ref is None else _rms(x, nfin_ref[...], RMS_EPS)


def _ffn_kernel_plain(x_ref, p_ref, nf_ref, w1_ref, w2_ref, np_ref, wpg_ref, wpp_ref, o_ref, *, n_chunk):
    _ffn_kernel(x_ref, p_ref, nf_ref, w1_ref, w2_ref, np_ref, wpg_ref, wpp_ref, None, o_ref,
                n_chunk=n_chunk)


def _ffn(x, p, norm_ffn, w1, w2, norm_ple, w_pg, w_pp, norm_final, tm, n_chunk):
    B, S, D = x.shape
    PD = p.shape[-1]
    FF = w1.shape[1]
    const = lambda b, i: (0, 0)
    row = lambda b, i: (b, i, 0)
    in_specs = [
        pl.BlockSpec((None, tm, D), row),
        pl.BlockSpec((None, tm, PD), row),
        pl.BlockSpec((1, D), const),
        pl.BlockSpec((D, FF), const, pipeline_mode=pl.Buffered(1)),
        pl.BlockSpec((FF, D), const, pipeline_mode=pl.Buffered(1)),
        pl.BlockSpec((1, D), const),
        pl.BlockSpec((D, D), const, pipeline_mode=pl.Buffered(1)),
        pl.BlockSpec((PD, D), const, pipeline_mode=pl.Buffered(1)),
    ]
    args = [x, p, norm_ffn, w1, w2, norm_ple, w_pg, w_pp]
    if norm_final is not None:
        in_specs.append(pl.BlockSpec((1, D), const))
        args.append(norm_final)
        body = functools.partial(_ffn_kernel, n_chunk=n_chunk)
    else:
        body = functools.partial(_ffn_kernel_plain, n_chunk=n_chunk)
    return pl.pallas_call(
        body,
        grid=(B, S // tm),
        in_specs=in_specs,
        out_specs=pl.BlockSpec((None, tm, D), row),
        out_shape=jax.ShapeDtypeStruct((B, S, D), F32),
        compiler_params=pltpu.CompilerParams(
            dimension_semantics=("parallel", "parallel"), vmem_limit_bytes=VMEM_LIMIT),
        name="ffn",
    )(*args)


def _block_diag2(w):
    z = jnp.zeros_like(w[0])
    return jnp.concatenate([jnp.concatenate([w[0], z], axis=1),
                            jnp.concatenate([z, w[1]], axis=1)], axis=0)


def _tiles(S):
    tm = min(TOKEN_TILE, S)
    tq = min(ATTN_TILE, S)
    tk = min(ATTN_TILE, S // 2)
    nsub = min(SCAN_CHUNKS_PER_STEP, S // CHUNK)
    assert S % tm == 0 and S % tq == 0 and (S // tk) % 2 == 0 and S % (nsub * CHUNK) == 0, S
    return tm, tq, tk, nsub


def kernel(x, p, norm_mix, w_in, shift_mu_prev, shift_mu_next, rwkv_w0, rwkv_w2, rwkv_a0,
           rwkv_a2, rwkv_g2, rwkv_k_k, rwkv_k_a, rwkv_r_k, rwkv_ln_w, rwkv_ln_b, rwkv_w_o,
           da_lq1, da_lk1, da_lq2, da_lk2, da_subln_w, da_w_o, w_out, norm_ffn, w_ff1, w_ff2,
           norm_ple, w_ple_gate, w_ple_proj, norm_final):
    B, S, D = x.shape
    L = w_in.shape[0]
    W = rwkv_w0.shape[-1]
    n_r = shift_mu_prev.shape[-1]
    n_qk = 2 * da_w_o.shape[1]
    n_v = da_w_o.shape[1]
    cos_t, sin_t = _rope_tables(S)
    seg = jnp.arange(SEG_BLOCK) // HEAD
    e_seg = (seg[:, None] == seg[None, :]).astype(BF16)
    tm, tq, tk, nsub = _tiles(S)
    for i in range(L):
        lambda_init = 0.8 - 0.6 * math.exp(-0.3 * i)
        w_i = w_in[i].astype(BF16)
        u_r, q, k, v = _in_proj(
            x, norm_mix[i][None], w_i, w_i[:, n_r + n_qk:n_r + n_qk + n_v].T, n_r, n_qk,
            cos_t, sin_t, tm)
        r, vv, kk, g, bonus, lw, bb, kd = _rwkv_prep(
            u_r, shift_mu_prev[i][None], shift_mu_next[i][None],
            rwkv_w0[i].reshape(1, 2 * W), _block_diag2(rwkv_w2[i]).astype(BF16),
            rwkv_a0[i].reshape(1, 2 * W), _block_diag2(rwkv_a2[i]).astype(BF16),
            rwkv_g2[i].astype(BF16), rwkv_k_k[i][None], rwkv_k_a[i][None],
            rwkv_r_k[i].reshape(1, W), e_seg, tm)
        y = _rwkv_scan(r, vv, kk, lw, bb, kd, nsub=nsub)
        ob = _diff_attn(q, k, v, da_lq1[i][None], da_lk1[i][None], da_lq2[i][None],
                        da_lk2[i][None], da_subln_w[i][:, None], lambda_init, tq=tq, tk=tk)
        x = _merge(x, y, bonus, g, ob, norm_mix[i][None],
                   w_i, rwkv_ln_w[i][None], rwkv_ln_b[i][None], e_seg,
                   rwkv_w_o[i].astype(BF16), da_w_o[i].astype(BF16), w_out[i].astype(BF16), tm)
        x = _ffn(x, p[i], norm_ffn[i][None], w_ff1[i].astype(BF16), w_ff2[i].astype(BF16),
                 norm_ple[i][None], w_ple_gate[i].astype(BF16), w_ple_proj[i].astype(BF16),
                 norm_final[None] if i == L - 1 else None, tm, n_chunk=FFN_CHUNKS)
    return x
```

```python
import functools
import math

import jax
import jax.numpy as jnp
from jax import lax
from jax.experimental import pallas as pl
from jax.experimental.pallas import tpu as pltpu

F32 = jnp.float32
BF16 = jnp.bfloat16

HEAD = 64
LANES = 128
CHUNK = 64
RMS_EPS = 1e-6
GN_EPS = 64e-5
SUBLN_EPS = 1e-5
ROPE_THETA = 500000.0
ROPE_DIM = 16
Q_SCALE = HEAD ** -0.5 * math.log2(math.e)
DECAY_SCALE = math.exp(-0.5)
VMEM_LIMIT = 56 * 1024 * 1024
TOKEN_TILE = 512
ATTN_TILE = 512
SCAN_CHUNKS_PER_STEP = 16
FFN_CHUNKS = 4
TILES_PER_ITER = 8
HEADS_PER_STEP = 1
SUM_ROWS = 16
HALO = 8
SEG_BLOCK = 256


def _rms(x, g, eps):
    return x * lax.rsqrt(jnp.mean(x * x, axis=-1, keepdims=True) + eps) * g


def _sigmoid(x):
    return 1.0 / (1.0 + jnp.exp(-x))


def _mm(a, b):
    return jnp.dot(a.astype(BF16), b.astype(BF16), preferred_element_type=F32)


def _mm_nt(a, b):
    return lax.dot_general(a.astype(BF16), b.astype(BF16), (((1,), (1,)), ((), ())),
                           preferred_element_type=F32)


def _split2(x):
    hi = x.astype(BF16)
    lo = (x - hi.astype(F32)).astype(BF16)
    return hi, lo


def _mm_exact_rhs(a, b_bf16):
    hi, lo = _split2(a)
    return (jnp.dot(hi, b_bf16, preferred_element_type=F32)
            + jnp.dot(lo, b_bf16, preferred_element_type=F32))


def _seg_sum(x, e):
    n = e.shape[0]
    return jnp.concatenate([_mm_exact_rhs(x[:, c:c + n], e) for c in range(0, x.shape[1], n)],
                           axis=1)


def _mm3(a, b):
    ah, al = _split2(a)
    bh, bl = _split2(b)
    return (jnp.dot(ah, bh, preferred_element_type=F32)
            + jnp.dot(ah, bl, preferred_element_type=F32)
            + jnp.dot(al, bh, preferred_element_type=F32))


def _inproj_kernel(x_ref, g_ref, w_ref, wv_ref, cos_ref, sin_ref,
                   ur_ref, q_ref, k_ref, v_ref):
    n_r = ur_ref.shape[1]
    n_qk = 2 * q_ref.shape[1]
    h = _rms(x_ref[...], g_ref[...], RMS_EPS).astype(BF16)
    ur_ref[...] = jnp.dot(h, w_ref[:, :n_r], preferred_element_type=F32)
    v_ref[...] = lax.dot_general(wv_ref[...], h, (((1,), (1,)), ((), ())),
                                 preferred_element_type=F32).astype(BF16)
    cos_t = cos_ref[...]
    sin_t = sin_ref[...]
    lane = lax.broadcasted_iota(jnp.int32, cos_t.shape, 1)
    first_half = (lane & (HEAD - 1)) < (ROPE_DIM // 2)
    n_tiles = n_qk // LANES
    qk = jnp.dot(h, w_ref[:, n_r:n_r + n_qk], preferred_element_type=F32)
    for c in range(n_tiles):
        xq = qk[:, c * LANES:(c + 1) * LANES]
        partner = jnp.where(first_half,
                            pltpu.roll(xq, LANES - ROPE_DIM // 2, 1),
                            pltpu.roll(xq, ROPE_DIM // 2, 1))
        ro = xq * cos_t + partner * sin_t
        if c < n_tiles // 2:
            q_ref[:, c * LANES:(c + 1) * LANES] = (ro * Q_SCALE).astype(BF16)
        else:
            cc = c - n_tiles // 2
            k_ref[:, cc * LANES:(cc + 1) * LANES] = ro.astype(BF16)


def _rope_tables(S):
    pos = jnp.arange(S, dtype=F32)
    inv_freq = ROPE_THETA ** (-jnp.arange(0, ROPE_DIM, 2, dtype=F32) / ROPE_DIM)
    ang = pos[:, None] * inv_freq[None, :]
    cos8, sin8 = jnp.cos(ang), jnp.sin(ang)
    ones = jnp.ones((S, HEAD - ROPE_DIM), F32)
    zeros = jnp.zeros((S, HEAD - ROPE_DIM), F32)
    cos64 = jnp.concatenate([cos8, cos8, ones], axis=1)
    sin64 = jnp.concatenate([-sin8, sin8, zeros], axis=1)
    return jnp.tile(cos64, (1, 2)), jnp.tile(sin64, (1, 2))


def _in_proj(x, g, w_all, w_vt, nr, nqk, cos_t, sin_t, tm):
    B, S, D = x.shape
    nv = w_vt.shape[0]
    const = lambda b, i: (0, 0)
    row = lambda b, i: (b, i, 0)
    return pl.pallas_call(
        _inproj_kernel,
        grid=(B, S // tm),
        in_specs=[
            pl.BlockSpec((None, tm, D), row),
            pl.BlockSpec((1, D), const),
            pl.BlockSpec(w_all.shape, const),
            pl.BlockSpec((nv, D), const),
            pl.BlockSpec((tm, LANES), lambda b, i: (i, 0)),
            pl.BlockSpec((tm, LANES), lambda b, i: (i, 0)),
        ],
        out_specs=[
            pl.BlockSpec((None, tm, nr), row),
            pl.BlockSpec((None, tm, nqk // 2), row),
            pl.BlockSpec((None, tm, nqk // 2), row),
            pl.BlockSpec((None, nv, tm), lambda b, i: (b, 0, i)),
        ],
        out_shape=[
            jax.ShapeDtypeStruct((B, S, nr), F32),
            jax.ShapeDtypeStruct((B, S, nqk // 2), BF16),
            jax.ShapeDtypeStruct((B, S, nqk // 2), BF16),
            jax.ShapeDtypeStruct((B, nv, S), BF16),
        ],
        compiler_params=pltpu.CompilerParams(
            dimension_semantics=("parallel", "parallel"), vmem_limit_bytes=VMEM_LIMIT),
        name="in_proj",
    )(x, g, w_all, w_vt, cos_t, sin_t)


def _prep_kernel(u_ref, up_ref, un_ref, mup_ref, mun_ref, w0_ref, w2_ref, a0_ref, a2_ref,
                 g2_ref, kk_w_ref, ka_ref, rk_ref, e_ref,
                 r_ref, v_ref, kk_ref, g_ref, bonus_ref, lw_ref, b_ref, kd_ref):
    i = pl.program_id(1)
    n = pl.num_programs(1)
    u = u_ref[...]
    tm = u.shape[0]
    W = r_ref.shape[-1]
    row8 = lax.broadcasted_iota(jnp.int32, (HALO, 1), 0)
    prev_row = jnp.where(i > 0, up_ref[HALO - 1:HALO, :], 0.0)
    next_row = jnp.where(i < n - 1, un_ref[0:1, :], 0.0)
    rolled = pltpu.roll(u, 1, 0)
    u_prev = jnp.concatenate([jnp.where(row8 == 0, prev_row, rolled[:HALO]), rolled[HALO:]], axis=0)
    rolled = pltpu.roll(u, tm - 1, 0)
    u_next = jnp.concatenate([rolled[:tm - HALO],
                              jnp.where(row8 == HALO - 1, next_row, rolled[tm - HALO:])], axis=0)
    mup = mup_ref[...]
    mun = mun_ref[...]
    us = (1.0 - mup - mun) * u + mup * u_prev + mun * u_next

    r = us[:, 0:W]
    k = us[:, W:2 * W]
    v = us[:, 2 * W:3 * W]
    wd = us[:, 3 * W:3 * W + LANES]
    ad = us[:, 3 * W + LANES:3 * W + 2 * LANES]
    gd = us[:, 3 * W + 2 * LANES:3 * W + 3 * LANES]

    z = w0_ref[...] + _mm(jnp.tanh(wd), w2_ref[...])
    lw = -(_sigmoid(z) * DECAY_SCALE)
    a = _sigmoid(a0_ref[...] + _mm(ad, a2_ref[...]))
    g_ref[...] = _mm(_sigmoid(gd), g2_ref[...]).astype(g_ref.dtype)

    e = e_ref[...]
    kraw = k * kk_w_ref[...]
    ss = _seg_sum(kraw * kraw, e)
    kk = kraw / jnp.maximum(jnp.sqrt(ss), 1e-12)
    ka = ka_ref[...]
    kd_sum = jnp.zeros_like(k)
    for d in range(2):
        a_d = a[:, d * W:(d + 1) * W]
        kd = k * (1.0 + (a_d - 1.0) * ka)
        kd_sum = kd_sum + kd
        lw_ref[d] = lw[:, d * W:(d + 1) * W]
        b_ref[d] = (kk * a_d).astype(b_ref.dtype)
        kd_ref[d] = kd.astype(kd_ref.dtype)
    c = _seg_sum(r * rk_ref[...] * kd_sum, e)
    r_ref[...] = r.astype(r_ref.dtype)
    v_ref[...] = v.astype(v_ref.dtype)
    kk_ref[...] = kk.astype(kk_ref.dtype)
    bonus_ref[...] = (c * v).astype(bonus_ref.dtype)


def _rwkv_prep(u_r, mu_prev, mu_next, w0, w2bd, a0, a2bd, g2, k_k, k_a, r_k, e_seg, tm):
    B, S, C = u_r.shape
    W = k_k.shape[-1]
    nblk8 = S // 8
    const = lambda b, i: (0, 0)
    row = lambda b, i: (b, i, 0)
    drow = lambda b, i: (0, b, i, 0)
    tok16 = jax.ShapeDtypeStruct((B, S, W), BF16)
    dtok = jax.ShapeDtypeStruct((2, B, S, W), F32)
    dtok16 = jax.ShapeDtypeStruct((2, B, S, W), BF16)
    return pl.pallas_call(
        _prep_kernel,
        grid=(B, S // tm),
        in_specs=[
            pl.BlockSpec((None, tm, C), row),
            pl.BlockSpec((None, 8, C), lambda b, i: (b, jnp.maximum(i * (tm // 8) - 1, 0), 0)),
            pl.BlockSpec((None, 8, C), lambda b, i: (b, jnp.minimum((i + 1) * (tm // 8), nblk8 - 1), 0)),
            pl.BlockSpec((1, C), const),
            pl.BlockSpec((1, C), const),
            pl.BlockSpec((1, 2 * W), const),
            pl.BlockSpec((LANES, 2 * W), const),
            pl.BlockSpec((1, 2 * W), const),
            pl.BlockSpec((LANES, 2 * W), const),
            pl.BlockSpec((LANES, W), const),
            pl.BlockSpec((1, W), const),
            pl.BlockSpec((1, W), const),
            pl.BlockSpec((1, W), const),
            pl.BlockSpec(e_seg.shape, const),
        ],
        out_specs=[pl.BlockSpec((None, tm, W), row)] * 5
                  + [pl.BlockSpec((2, None, tm, W), drow)] * 3,
        out_shape=[tok16, tok16, tok16, tok16, tok16, dtok, dtok16, dtok16],
        compiler_params=pltpu.CompilerParams(
            dimension_semantics=("parallel", "parallel"), vmem_limit_bytes=VMEM_LIMIT),
        name="rwkv_prep",
    )(u_r, u_r, u_r, mu_prev, mu_next, w0, w2bd, a0, a2bd, g2, k_k, k_a, r_k, e_seg)


def _scan_kernel(r_ref, v_ref, kk_ref, lw_ref, b_ref, kd_ref, y_ref,
                 t_scr, g_scr, h_scr, q_scr, yl_scr, *, nsub, n_pair, n_step):
    step = pl.program_id(0)
    last = pl.num_programs(0) - 2
    row_cur = jnp.minimum(step, last) // n_step
    step_prev = jnp.maximum(step - 1, 0)
    rev = (row_cur // n_pair) % 2
    rev_prev = ((step_prev // n_step) // n_pair) % 2
    sgn = 1 - 2 * rev
    C = CHUNK
    P2 = 2 * C

    @pl.when(step == 0)
    def _():
        t_scr[...] = jnp.zeros_like(t_scr)
        g_scr[...] = jnp.zeros_like(g_scr)
        h_scr[...] = jnp.zeros_like(h_scr)
        q_scr[...] = jnp.zeros_like(q_scr)
        yl_scr[...] = jnp.zeros_like(yl_scr)

    def slices(direction):
        out = []
        for s in range(nsub):
            ci = s + direction * (nsub - 1 - 2 * s)
            out.append(pl.ds(pl.multiple_of(ci * C, C), C))
        return out

    chunk_slices = slices(rev)
    chain_slices = slices(rev_prev)

    chain = {"t": jnp.where(step_prev % n_step == 0, 0.0, t_scr[...]), "next": 0}

    def chain_steps(n):
        for _ in range(n):
            s = chain["next"]
            if s == nsub:
                return
            t_cur = chain["t"]
            t16 = t_cur.astype(BF16)
            y_st = jnp.dot(q_scr[s], t16, preferred_element_type=F32) + yl_scr[s]
            chain["t"] = jnp.dot(g_scr[s], t16, preferred_element_type=F32) + h_scr[s]
            y_ref[chain_slices[s], :] = y_st[:C, :] + y_st[C:, :]
            chain["next"] = s + 1

    row = lax.broadcasted_iota(jnp.int32, (P2, P2), 0)
    col = lax.broadcasted_iota(jnp.int32, (P2, P2), 1)
    rt = row & (C - 1)
    ct = col & (C - 1)
    delta = (rt - ct) * sgn
    strict = delta > 0
    incl = delta >= 0
    eye = row == col
    ident = jnp.where(eye, 1.0, 0.0)
    blk8 = (rt >> 3) == (ct >> 3)
    blk16 = (rt >> 4) == (ct >> 4)
    blk32 = (rt >> 5) == (ct >> 5)
    in8 = strict & blk8
    in16 = strict & blk16 & ~blk8
    in32 = strict & blk32 & ~blk16
    in64 = strict & ~blk32
    lane = lax.broadcasted_iota(jnp.int32, (C, LANES), 1)
    head0 = lane < HEAD

    def stack(x):
        return jnp.concatenate([jnp.where(head0, x, 0.0), jnp.where(head0, 0.0, x)], axis=0)

    subs = range(nsub)
    sls = chunk_slices

    lws = [lw_ref[sl, :] for sl in sls]
    tots = [jnp.sum(lw, axis=0, keepdims=True) for lw in lws]
    tok = lax.broadcasted_iota(jnp.int32, (C, LANES), 0)
    cums = []
    for lw, tot in zip(lws, tots):
        pre = lw
        shift = 1
        while shift < C:
            pre = pre + jnp.where(tok >= shift, pltpu.roll(pre, shift, 0), 0.0)
            shift *= 2
        cums.append(jnp.where(rev == 1, tot - pre + lw, pre))
    a_st, r_st, b_st, k_st, bh_t, kh_t, v_st = [], [], [], [], [], [], []
    for s in subs:
        sl, lw, cum, tot = sls[s], lws[s], cums[s], tots[s]
        e_neg = jnp.exp(-cum)
        e_rem = jnp.exp(tot - cum)
        b = b_ref[sl, :]
        kd = kd_ref[sl, :]
        a_st.append(stack(kk_ref[sl, :] * -jnp.exp(cum - lw)).astype(BF16))
        r_st.append(stack(r_ref[sl, :] * jnp.exp(cum)))
        b_st.append(stack(b * e_neg).astype(BF16))
        k_st.append(stack(kd * e_neg).astype(BF16))
        bh_t.append(stack(b * e_rem).T.astype(BF16))
        kh_t.append(stack(kd * e_rem).T.astype(BF16))
        v_st.append(stack(v_ref[sl, :]).astype(BF16))

    nt = (((1,), (1,)), ((), ()))
    sc = [lax.dot_general(jnp.concatenate([a_st[s], r_st[s].astype(BF16)], axis=0),
                          jnp.concatenate([b_st[s], k_st[s]], axis=0), nt,
                          preferred_element_type=F32) for s in subs]
    per_stage = -(-nsub // 8)
    chain_steps(per_stage)
    l_ab = [x[:P2, :P2] for x in sc]
    l_ak = [jnp.where(strict, x[:P2, P2:], 0.0).astype(BF16) for x in sc]
    a_rb = [jnp.where(incl, x[P2:, :P2], 0.0).astype(BF16) for x in sc]
    a_rk = [jnp.where(incl, x[P2:, P2:], 0.0).astype(BF16) for x in sc]
    x_loc = [jnp.dot(l_ak[s], v_st[s], preferred_element_type=F32) for s in subs]
    chain_steps(per_stage)

    d8 = [jnp.where(in8, x, 0.0) for x in l_ab]
    m1 = [ident + x for x in d8]
    p1 = [_mm(x, x) for x in d8]
    chain_steps(per_stage)
    pm = [_mm(p1[s], jnp.concatenate([p1[s], m1[s]], axis=1)) for s in subs]
    m2 = [m1[s] + pm[s][:, P2:] for s in subs]
    m8 = [m2[s] + _mm(pm[s][:, :P2], m2[s]) for s in subs]
    chain_steps(per_stage)
    o16 = [jnp.where(in16, x, 0.0) for x in l_ab]
    om = [_mm(o16[s], m8[s]) for s in subs]
    m16 = [m8[s] + _mm(m8[s], om[s]) for s in subs]
    chain_steps(per_stage)
    o32 = [jnp.where(in32, x, 0.0) for x in l_ab]
    om = [_mm(o32[s], m16[s]) for s in subs]
    m32 = [(m16[s] + _mm(m16[s], om[s])).astype(BF16) for s in subs]
    chain_steps(per_stage)
    o64 = [jnp.where(in64, x, 0.0).astype(BF16) for x in l_ab]
    y1 = [jnp.dot(m32[s], jnp.concatenate([a_st[s], x_loc[s].astype(BF16)], axis=1),
                  preferred_element_type=F32) for s in subs]
    oy = [_mm(o64[s], y1[s]).astype(BF16) for s in subs]
    chain_steps(per_stage)
    wu16 = [(y1[s] + jnp.dot(m32[s], oy[s], preferred_element_type=F32)).astype(BF16)
            for s in subs]
    chain_steps(per_stage)
    zero_blk = jnp.zeros((P2, LANES), BF16)
    fin = [jnp.dot(jnp.concatenate([jnp.concatenate([a_rb[s], a_rk[s]], axis=1),
                                    jnp.concatenate([bh_t[s], kh_t[s]], axis=1)], axis=0),
                   jnp.concatenate([wu16[s], jnp.concatenate([zero_blk, v_st[s]], axis=1)], axis=0),
                   preferred_element_type=F32) for s in subs]
    chain_steps(nsub)
    t_scr[...] = chain["t"]
    for s in subs:
        q_scr[s] = (r_st[s] + fin[s][:P2, :LANES]).astype(BF16)
        yl_scr[s] = fin[s][:P2, LANES:]
        g_scr[s] = (jnp.where(eye, jnp.exp(tots[s]), 0.0) + fin[s][P2:, :LANES]).astype(BF16)
        h_scr[s] = fin[s][P2:, LANES:]


def _rwkv_scan(r, v, kk, lw, bb, kd, nsub):
    B, S, W = r.shape
    n_pair = W // LANES
    T = nsub * CHUNK
    n_step = S // T

    n_total = B * 2 * n_pair * n_step

    def locate(t):
        g = t // n_step
        c = t % n_step
        d = (g // n_pair) % 2
        return d, g // (2 * n_pair), c + d * (n_step - 1 - 2 * c), g % n_pair

    def shared(t):
        _, b, blk, p = locate(jnp.minimum(t, n_total - 1))
        return (b, blk, p)

    def per_dir(t):
        return locate(jnp.minimum(t, n_total - 1))

    def out_map(t):
        return locate(jnp.maximum(t - 1, 0))

    mat = pltpu.VMEM((nsub, LANES, LANES), F32)
    mat16 = pltpu.VMEM((nsub, LANES, LANES), BF16)
    return pl.pallas_call(
        functools.partial(_scan_kernel, nsub=nsub, n_pair=n_pair, n_step=n_step),
        grid=(n_total + 1,),
        in_specs=[pl.BlockSpec((None, T, LANES), shared)] * 3
                 + [pl.BlockSpec((None, None, T, LANES), per_dir)] * 3,
        out_specs=pl.BlockSpec((None, None, T, LANES), out_map),
        out_shape=jax.ShapeDtypeStruct((2, B, S, W), F32),
        scratch_shapes=[pltpu.VMEM((LANES, LANES), F32), mat16, mat, mat16, mat],
        compiler_params=pltpu.CompilerParams(
            dimension_semantics=("arbitrary",), vmem_limit_bytes=VMEM_LIMIT),
        name="rwkv_scan",
    )(r, v, kk, lw, bb, kd)


def _attn_kernel(q_ref, qn_ref, k_ref, vt_ref, lq1_ref, lk1_ref, lq2_ref, lk2_ref, sw_ref, o_ref,
                 m_scr, alpha_scr, acc_scr, s_even, s_odd, p_even, p_odd, mx_even, mx_odd,
                 *, tk, lambda_init):
    n_kv = k_ref.shape[0] // tk
    dv = vt_ref.shape[0] // HEADS_PER_STEP
    n_streams = 2 * HEADS_PER_STEP
    lane = lax.broadcasted_iota(jnp.int32, (q_ref.shape[0], LANES), 1)

    def component_queries(ref):
        out = []
        for hd in range(HEADS_PER_STEP):
            q = ref[:, hd * LANES:(hd + 1) * LANES]
            zero = jnp.zeros_like(q)
            out += [jnp.where(lane < HEAD, q, zero), jnp.where(lane < HEAD, zero, q)]
        return out

    qc = component_queries(q_ref)
    m_scr[...] = jnp.full(m_scr.shape, -jnp.inf, F32)
    acc_scr[...] = jnp.zeros(acc_scr.shape, F32)

    def scores(j, u, queries):
        hd = u // 2
        ks = k_ref[pl.ds(pl.multiple_of(j * tk, tk), tk), hd * LANES:(hd + 1) * LANES]
        return lax.dot_general(ks, queries[u], (((1,), (1,)), ((), ())),
                               preferred_element_type=F32)

    ones_rows = jnp.ones((SUM_ROWS, tk), BF16)

    def weighted_values(j, p_ref, u):
        hd = u // 2
        vts = vt_ref[hd * dv:(hd + 1) * dv, pl.ds(pl.multiple_of(j * tk, tk), tk)]
        vts = jnp.concatenate([vts, ones_rows], axis=0)
        acc_scr[u] = alpha_scr[u] * acc_scr[u] + jnp.dot(vts, p_ref[u],
                                                         preferred_element_type=F32)

    def put_scores(j, s_ref, mx_ref, u, queries):
        s = scores(j, u, queries)
        s_ref[u] = s
        mx_ref[u] = jnp.max(s, axis=0, keepdims=True)

    def tile_step(j, s_cur, mx_cur, s_nxt, mx_nxt, p_cur, p_prev, first=False, last=False):
        for u in range(n_streams):
            if not first:
                weighted_values(j - 1, p_prev, u)
            if not last:
                put_scores(j + 1, s_nxt, mx_nxt, u, qc)
        for u in range(n_streams):
            m_old = m_scr[u]
            m_new = jnp.maximum(m_old, mx_cur[u])
            alpha = jnp.exp2(m_old - m_new)
            p = jnp.exp2(s_cur[u] - m_new)
            p_cur[u] = p.astype(BF16)
            alpha_scr[u] = alpha
            m_scr[u] = m_new

    qc_next = component_queries(qn_ref)

    @pl.when(pl.program_id(2) == 0)
    def _():
        for u in range(n_streams):
            put_scores(0, s_even, mx_even, u, qc)

    unroll = TILES_PER_ITER if n_kv % TILES_PER_ITER == 0 else 2

    def steps(base, head=False, tail=False):
        for t in range(0, unroll, 2):
            tile_step(base + t, s_even, mx_even, s_odd, mx_odd, p_even, p_odd,
                      first=head and t == 0)
            tile_step(base + t + 1, s_odd, mx_odd, s_even, mx_even, p_odd, p_even,
                      last=tail and t + 2 == unroll)

    def body(i, carry):
        steps(unroll * i)
        return carry

    n_groups = n_kv // unroll
    if n_groups == 1:
        steps(0, head=True, tail=True)
    else:
        steps(0, head=True)
        lax.fori_loop(1, n_groups - 1, body, 0)
        steps(n_kv - unroll, tail=True)
    for u in range(n_streams):
        weighted_values(n_kv - 1, p_odd, u)
    for u in range(n_streams):
        put_scores(0, s_even, mx_even, u, qc_next)

    lam = (jnp.exp(jnp.sum(lq1_ref[...] * lk1_ref[...], axis=-1, keepdims=True))
           - jnp.exp(jnp.sum(lq2_ref[...] * lk2_ref[...], axis=-1, keepdims=True)) + lambda_init)
    for hd in range(HEADS_PER_STEP):
        u1, u2 = 2 * hd, 2 * hd + 1
        o = (acc_scr[u1, :dv, :] / acc_scr[u1, dv:dv + 1, :]
             - lam * (acc_scr[u2, :dv, :] / acc_scr[u2, dv:dv + 1, :]))
        o = o * lax.rsqrt(jnp.mean(o * o, axis=0, keepdims=True) + SUBLN_EPS)
        o = o * (sw_ref[...] * (1.0 - lambda_init))
        o_ref[:, hd * dv:(hd + 1) * dv] = o.T.astype(o_ref.dtype)


def _diff_attn(q, k, vt, lq1, lk1, lq2, lk2, subln_w, lambda_init, tq, tk):
    B, S, QW = q.shape
    H = QW // LANES
    DV = vt.shape[1] // H
    hps = HEADS_PER_STEP
    ns = 2 * hps
    const = lambda b, h, i: (0, 0)
    return pl.pallas_call(
        functools.partial(_attn_kernel, tk=tk, lambda_init=lambda_init),
        grid=(B, H // hps, S // tq),
        in_specs=[
            pl.BlockSpec((None, tq, hps * LANES), lambda b, h, i: (b, i, h)),
            pl.BlockSpec((None, tq, hps * LANES),
                         lambda b, h, i: (b, jnp.minimum(i + 1, S // tq - 1), h)),
            pl.BlockSpec((None, S, hps * LANES), lambda b, h, i: (b, 0, h)),
            pl.BlockSpec((None, hps * DV, S), lambda b, h, i: (b, h, 0)),
            pl.BlockSpec((1, HEAD), const),
            pl.BlockSpec((1, HEAD), const),
            pl.BlockSpec((1, HEAD), const),
            pl.BlockSpec((1, HEAD), const),
            pl.BlockSpec((DV, 1), const),
        ],
        out_specs=pl.BlockSpec((None, tq, hps * DV), lambda b, h, i: (b, i, h)),
        out_shape=jax.ShapeDtypeStruct((B, S, H * DV), BF16),
        scratch_shapes=[pltpu.VMEM((ns, 1, tq), F32),
                        pltpu.VMEM((ns, 1, tq), F32),
                        pltpu.VMEM((ns, DV + SUM_ROWS, tq), F32),
                        pltpu.VMEM((ns, tk, tq), F32),
                        pltpu.VMEM((ns, tk, tq), F32),
                        pltpu.VMEM((ns, tk, tq), BF16),
                        pltpu.VMEM((ns, tk, tq), BF16),
                        pltpu.VMEM((ns, 1, tq), F32),
                        pltpu.VMEM((ns, 1, tq), F32)],
        compiler_params=pltpu.CompilerParams(
            dimension_semantics=("parallel", "parallel", "arbitrary"),
            vmem_limit_bytes=VMEM_LIMIT),
        name="diff_attn",
    )(q, q, k, vt, lq1, lk1, lq2, lk2, subln_w)


def _merge_kernel(x_ref, y_ref, bonus_ref, g_ref, ob_ref, nm_ref, wg_ref, lnw_ref, lnb_ref,
                  e_ref, woa_ref, wob_ref, wout_ref, o_ref):
    x = x_ref[...]
    D = x.shape[-1]
    h = _rms(x, nm_ref[...], RMS_EPS).astype(BF16)
    n_in = wg_ref.shape[1]
    gates = _sigmoid(jnp.dot(h, wg_ref[:, n_in - 2 * D:], preferred_element_type=F32))
    e = e_ref[...]
    y = y_ref[0] + y_ref[1]
    mean = _seg_sum(y, e) * (1.0 / HEAD)
    yc = y - mean
    var = _seg_sum(yc * yc, e) * (1.0 / HEAD)
    yn = yc * lax.rsqrt(var + GN_EPS) * lnw_ref[...] + lnb_ref[...]
    ya = _mm((yn + bonus_ref[...]) * g_ref[...], woa_ref[...])
    yb = jnp.dot(ob_ref[...], wob_ref[...], preferred_element_type=F32)
    merged = gates[:, :D] * ya + gates[:, D:] * yb
    o_ref[...] = x + _mm(merged, wout_ref[...])


def _merge(x, y, bonus, g, ob, norm_mix, w_g, ln_w, ln_b, e_seg, w_oa, w_ob, w_out, tm):
    B, S, D = x.shape
    W = bonus.shape[-1]
    const = lambda b, i: (0, 0)
    row = lambda b, i: (b, i, 0)
    return pl.pallas_call(
        _merge_kernel,
        grid=(B, S // tm),
        in_specs=[
            pl.BlockSpec((None, tm, D), row),
            pl.BlockSpec((2, None, tm, W), lambda b, i: (0, b, i, 0)),
            pl.BlockSpec((None, tm, W), row),
            pl.BlockSpec((None, tm, W), row),
            pl.BlockSpec((None, tm, W), row),
            pl.BlockSpec((1, D), const),
            pl.BlockSpec(w_g.shape, const),
            pl.BlockSpec((1, W), const),
            pl.BlockSpec((1, W), const),
            pl.BlockSpec(e_seg.shape, const),
            pl.BlockSpec((W, D), const),
            pl.BlockSpec((W, D), const),
            pl.BlockSpec((D, D), const),
        ],
        out_specs=pl.BlockSpec((None, tm, D), row),
        out_shape=jax.ShapeDtypeStruct((B, S, D), F32),
        compiler_params=pltpu.CompilerParams(
            dimension_semantics=("parallel", "parallel"), vmem_limit_bytes=VMEM_LIMIT),
        name="merge",
    )(x, y, bonus, g, ob, norm_mix, w_g, ln_w, ln_b, e_seg, w_oa, w_ob, w_out)


def _ffn_kernel(x_ref, p_ref, nf_ref, w1_ref, w2_ref, np_ref, wpg_ref, wpp_ref, nfin_ref, o_ref,
                *, n_chunk):
    x = x_ref[...]
    h = _rms(x, nf_ref[...], RMS_EPS).astype(BF16)
    ff = w1_ref.shape[1] // n_chunk
    acc = x
    for c in range(n_chunk):
        t = jnp.maximum(jnp.dot(h, w1_ref[:, c * ff:(c + 1) * ff], preferred_element_type=F32), 0.0)
        acc = acc + jnp.dot((t * t).astype(BF16), w2_ref[c * ff:(c + 1) * ff, :],
                            preferred_element_type=F32)
    x = acc
    h = _rms(x, np_ref[...], RMS_EPS).astype(BF16)
    gate = _sigmoid(jnp.dot(h, wpg_ref[...], preferred_element_type=F32))
    x = x + gate * _mm(p_ref[...], wpp_ref[...])
    o_ref[...] = x if nfin_ref is None else _rms(x, nfin_ref[...], RMS_EPS)


def _ffn_kernel_plain(x_ref, p_ref, nf_ref, w1_ref, w2_ref, np_ref, wpg_ref, wpp_ref, o_ref, *, n_chunk):
    _ffn_kernel(x_ref, p_ref, nf_ref, w1_ref, w2_ref, np_ref, wpg_ref, wpp_ref, None, o_ref,
                n_chunk=n_chunk)


def _ffn(x, p, norm_ffn, w1, w2, norm_ple, w_pg, w_pp, norm_final, tm, n_chunk):
    B, S, D = x.shape
    PD = p.shape[-1]
    FF = w1.shape[1]
    const = lambda b, i: (0, 0)
    row = lambda b, i: (b, i, 0)
    in_specs = [
        pl.BlockSpec((None, tm, D), row),
        pl.BlockSpec((None, tm, PD), row),
        pl.BlockSpec((1, D), const),
        pl.BlockSpec((D, FF), const, pipeline_mode=pl.Buffered(1)),
        pl.BlockSpec((FF, D), const, pipeline_mode=pl.Buffered(1)),
        pl.BlockSpec((1, D), const),
        pl.BlockSpec((D, D), const, pipeline_mode=pl.Buffered(1)),
        pl.BlockSpec((PD, D), const, pipeline_mode=pl.Buffered(1)),
    ]
    args = [x, p, norm_ffn, w1, w2, norm_ple, w_pg, w_pp]
    if norm_final is not None:
        in_specs.append(pl.BlockSpec((1, D), const))
        args.append(norm_final)
        body = functools.partial(_ffn_kernel, n_chunk=n_chunk)
    else:
        body = functools.partial(_ffn_kernel_plain, n_chunk=n_chunk)
    return pl.pallas_call(
        body,
        grid=(B, S // tm),
        in_specs=in_specs,
        out_specs=pl.BlockSpec((None, tm, D), row),
        out_shape=jax.ShapeDtypeStruct((B, S, D), F32),
        compiler_params=pltpu.CompilerParams(
            dimension_semantics=("parallel", "parallel"), vmem_limit_bytes=VMEM_LIMIT),
        name="ffn",
    )(*args)


def _block_diag2(w):
    z = jnp.zeros_like(w[0])
    return jnp.concatenate([jnp.concatenate([w[0], z], axis=1),
                            jnp.concatenate([z, w[1]], axis=1)], axis=0)


def _tiles(S):
    tm = min(TOKEN_TILE, S)
    tq = min(2 * ATTN_TILE, S)
    tk = min(ATTN_TILE, S // 2)
    nsub = min(SCAN_CHUNKS_PER_STEP, S // CHUNK)
    assert S % tm == 0 and S % tq == 0 and (S // tk) % 2 == 0 and S % (nsub * CHUNK) == 0, S
    return tm, tq, tk, nsub


def kernel(x, p, norm_mix, w_in, shift_mu_prev, shift_mu_next, rwkv_w0, rwkv_w2, rwkv_a0,
           rwkv_a2, rwkv_g2, rwkv_k_k, rwkv_k_a, rwkv_r_k, rwkv_ln_w, rwkv_ln_b, rwkv_w_o,
           da_lq1, da_lk1, da_lq2, da_lk2, da_subln_w, da_w_o, w_out, norm_ffn, w_ff1, w_ff2,
           norm_ple, w_ple_gate, w_ple_proj, norm_final):
    B, S, D = x.shape
    L = w_in.shape[0]
    W = rwkv_w0.shape[-1]
    n_r = shift_mu_prev.shape[-1]
    n_qk = 2 * da_w_o.shape[1]
    n_v = da_w_o.shape[1]
    cos_t, sin_t = _rope_tables(S)
    seg = jnp.arange(SEG_BLOCK) // HEAD
    e_seg = (seg[:, None] == seg[None, :]).astype(BF16)
    tm, tq, tk, nsub = _tiles(S)
    for i in range(L):
        lambda_init = 0.8 - 0.6 * math.exp(-0.3 * i)
        w_i = w_in[i].astype(BF16)
        u_r, q, k, v = _in_proj(
            x, norm_mix[i][None], w_i, w_i[:, n_r + n_qk:n_r + n_qk + n_v].T, n_r, n_qk,
            cos_t, sin_t, tm)
        r, vv, kk, g, bonus, lw, bb, kd = _rwkv_prep(
            u_r, shift_mu_prev[i][None], shift_mu_next[i][None],
            rwkv_w0[i].reshape(1, 2 * W), _block_diag2(rwkv_w2[i]).astype(BF16),
            rwkv_a0[i].reshape(1, 2 * W), _block_diag2(rwkv_a2[i]).astype(BF16),
            rwkv_g2[i].astype(BF16), rwkv_k_k[i][None], rwkv_k_a[i][None],
            rwkv_r_k[i].reshape(1, W), e_seg, tm)
        y = _rwkv_scan(r, vv, kk, lw, bb, kd, nsub=nsub)
        ob = _diff_attn(q, k, v, da_lq1[i][None], da_lk1[i][None], da_lq2[i][None],
                        da_lk2[i][None], da_subln_w[i][:, None], lambda_init, tq=tq, tk=tk)
        x = _merge(x, y, bonus, g, ob, norm_mix[i][None],
                   w_i, rwkv_ln_w[i][None], rwkv_ln_b[i][None], e_seg,
                   rwkv_w_o[i].astype(BF16), da_w_o[i].astype(BF16), w_out[i].astype(BF16), tm)
        x = _ffn(x, p[i], norm_ffn[i][None], w_ff1[i].astype(BF16), w_ff2[i].astype(BF16),
                 norm_ple[i][None], w_ple_gate[i].astype(BF16), w_ple_proj[i].astype(BF16),
                 norm_final[None] if i == L - 1 else None, tm, n_chunk=FFN_CHUNKS)
    return x
```

```python
import functools
import math

import jax
import jax.numpy as jnp
from jax import lax
from jax.experimental import pallas as pl
from jax.experimental.pallas import tpu as pltpu

F32 = jnp.float32
BF16 = jnp.bfloat16

HEAD = 64
LANES = 128
CHUNK = 64
RMS_EPS = 1e-6
GN_EPS = 64e-5
SUBLN_EPS = 1e-5
ROPE_THETA = 500000.0
ROPE_DIM = 16
Q_SCALE = HEAD ** -0.5 * math.log2(math.e)
DECAY_SCALE = math.exp(-0.5)
VMEM_LIMIT = 56 * 1024 * 1024
TOKEN_TILE = 512
ATTN_Q_TILE = 1024
ATTN_K_TILE = 512
SCAN_CHUNKS_PER_STEP = 16
FFN_CHUNKS = 4
TILES_PER_ITER = 8
SUM_ROWS = 16
HALO = 8
SEG_BLOCK = 256


def _rms(x, g, eps):
    return x * lax.rsqrt(jnp.mean(x * x, axis=-1, keepdims=True) + eps) * g


def _sigmoid(x):
    return 1.0 / (1.0 + jnp.exp(-x))


def _mm(a, b):
    return jnp.dot(a.astype(BF16), b.astype(BF16), preferred_element_type=F32)


def _split2(x):
    hi = x.astype(BF16)
    lo = (x - hi.astype(F32)).astype(BF16)
    return hi, lo


def _mm_exact_rhs(a, b_bf16):
    hi, lo = _split2(a)
    return (jnp.dot(hi, b_bf16, preferred_element_type=F32)
            + jnp.dot(lo, b_bf16, preferred_element_type=F32))


def _seg_sum(x, e):
    n = e.shape[0]
    return jnp.concatenate([_mm_exact_rhs(x[:, c:c + n], e) for c in range(0, x.shape[1], n)],
                           axis=1)


def _inproj_kernel(x_ref, g_ref, w_ref, wv_ref, cos_ref, sin_ref,
                   ur_ref, q_ref, k_ref, v_ref):
    n_r = ur_ref.shape[1]
    n_qk = 2 * q_ref.shape[1]
    h = _rms(x_ref[...], g_ref[...], RMS_EPS).astype(BF16)
    ur_ref[...] = jnp.dot(h, w_ref[:, :n_r], preferred_element_type=F32)
    v_ref[...] = lax.dot_general(wv_ref[...], h, (((1,), (1,)), ((), ())),
                                 preferred_element_type=F32).astype(BF16)
    cos_t = cos_ref[...]
    sin_t = sin_ref[...]
    lane = lax.broadcasted_iota(jnp.int32, cos_t.shape, 1)
    first_half = (lane & (HEAD - 1)) < (ROPE_DIM // 2)
    n_tiles = n_qk // LANES
    qk = jnp.dot(h, w_ref[:, n_r:n_r + n_qk], preferred_element_type=F32)
    for c in range(n_tiles):
        xq = qk[:, c * LANES:(c + 1) * LANES]
        partner = jnp.where(first_half,
                            pltpu.roll(xq, LANES - ROPE_DIM // 2, 1),
                            pltpu.roll(xq, ROPE_DIM // 2, 1))
        ro = xq * cos_t + partner * sin_t
        if c < n_tiles // 2:
            q_ref[:, c * LANES:(c + 1) * LANES] = (ro * Q_SCALE).astype(BF16)
        else:
            cc = c - n_tiles // 2
            k_ref[:, cc * LANES:(cc + 1) * LANES] = ro.astype(BF16)


def _rope_tables(S):
    pos = jnp.arange(S, dtype=F32)
    inv_freq = ROPE_THETA ** (-jnp.arange(0, ROPE_DIM, 2, dtype=F32) / ROPE_DIM)
    ang = pos[:, None] * inv_freq[None, :]
    cos8, sin8 = jnp.cos(ang), jnp.sin(ang)
    ones = jnp.ones((S, HEAD - ROPE_DIM), F32)
    zeros = jnp.zeros((S, HEAD - ROPE_DIM), F32)
    cos64 = jnp.concatenate([cos8, cos8, ones], axis=1)
    sin64 = jnp.concatenate([-sin8, sin8, zeros], axis=1)
    return jnp.tile(cos64, (1, 2)), jnp.tile(sin64, (1, 2))


def _in_proj(x, g, w_all, w_vt, nr, nqk, cos_t, sin_t, tm):
    B, S, D = x.shape
    nv = w_vt.shape[0]
    const = lambda b, i: (0, 0)
    row = lambda b, i: (b, i, 0)
    return pl.pallas_call(
        _inproj_kernel,
        grid=(B, S // tm),
        in_specs=[
            pl.BlockSpec((None, tm, D), row),
            pl.BlockSpec((1, D), const),
            pl.BlockSpec(w_all.shape, const),
            pl.BlockSpec((nv, D), const),
            pl.BlockSpec((tm, LANES), lambda b, i: (i, 0)),
            pl.BlockSpec((tm, LANES), lambda b, i: (i, 0)),
        ],
        out_specs=[
            pl.BlockSpec((None, tm, nr), row),
            pl.BlockSpec((None, tm, nqk // 2), row),
            pl.BlockSpec((None, tm, nqk // 2), row),
            pl.BlockSpec((None, nv, tm), lambda b, i: (b, 0, i)),
        ],
        out_shape=[
            jax.ShapeDtypeStruct((B, S, nr), F32),
            jax.ShapeDtypeStruct((B, S, nqk // 2), BF16),
            jax.ShapeDtypeStruct((B, S, nqk // 2), BF16),
            jax.ShapeDtypeStruct((B, nv, S), BF16),
        ],
        compiler_params=pltpu.CompilerParams(
            dimension_semantics=("parallel", "parallel"), vmem_limit_bytes=VMEM_LIMIT),
        name="in_proj",
    )(x, g, w_all, w_vt, cos_t, sin_t)


def _prep_kernel(u_ref, up_ref, un_ref, mup_ref, mun_ref, w0_ref, w2_ref, a0_ref, a2_ref,
                 g2_ref, kk_w_ref, ka_ref, rk_ref, e_ref,
                 r_ref, v_ref, kk_ref, g_ref, bonus_ref, lw_ref, b_ref, kd_ref):
    i = pl.program_id(1)
    n = pl.num_programs(1)
    u = u_ref[...]
    tm = u.shape[0]
    W = r_ref.shape[-1]
    row8 = lax.broadcasted_iota(jnp.int32, (HALO, 1), 0)
    prev_row = jnp.where(i > 0, up_ref[HALO - 1:HALO, :], 0.0)
    next_row = jnp.where(i < n - 1, un_ref[0:1, :], 0.0)
    rolled = pltpu.roll(u, 1, 0)
    u_prev = jnp.concatenate([jnp.where(row8 == 0, prev_row, rolled[:HALO]), rolled[HALO:]], axis=0)
    rolled = pltpu.roll(u, tm - 1, 0)
    u_next = jnp.concatenate([rolled[:tm - HALO],
                              jnp.where(row8 == HALO - 1, next_row, rolled[tm - HALO:])], axis=0)
    mup = mup_ref[...]
    mun = mun_ref[...]
    us = (1.0 - mup - mun) * u + mup * u_prev + mun * u_next

    r = us[:, 0:W]
    k = us[:, W:2 * W]
    v = us[:, 2 * W:3 * W]
    wd = us[:, 3 * W:3 * W + LANES]
    ad = us[:, 3 * W + LANES:3 * W + 2 * LANES]
    gd = us[:, 3 * W + 2 * LANES:3 * W + 3 * LANES]

    z = w0_ref[...] + _mm(jnp.tanh(wd), w2_ref[...])
    lw = -(_sigmoid(z) * DECAY_SCALE)
    a = _sigmoid(a0_ref[...] + _mm(ad, a2_ref[...]))
    g_ref[...] = _mm(_sigmoid(gd), g2_ref[...]).astype(g_ref.dtype)

    e = e_ref[...]
    kraw = k * kk_w_ref[...]
    ss = _seg_sum(kraw * kraw, e)
    kk = kraw / jnp.maximum(jnp.sqrt(ss), 1e-12)
    ka = ka_ref[...]
    kd_sum = jnp.zeros_like(k)
    for d in range(2):
        a_d = a[:, d * W:(d + 1) * W]
        kd = k * (1.0 + (a_d - 1.0) * ka)
        kd_sum = kd_sum + kd
        lw_ref[d] = lw[:, d * W:(d + 1) * W]
        b_ref[d] = (kk * a_d).astype(b_ref.dtype)
        kd_ref[d] = kd.astype(kd_ref.dtype)
    c = _seg_sum(r * rk_ref[...] * kd_sum, e)
    r_ref[...] = r.astype(r_ref.dtype)
    v_ref[...] = v.astype(v_ref.dtype)
    kk_ref[...] = kk.astype(kk_ref.dtype)
    bonus_ref[...] = (c * v).astype(bonus_ref.dtype)


def _rwkv_prep(u_r, mu_prev, mu_next, w0, w2bd, a0, a2bd, g2, k_k, k_a, r_k, e_seg, tm):
    B, S, C = u_r.shape
    W = k_k.shape[-1]
    nblk8 = S // 8
    const = lambda b, i: (0, 0)
    row = lambda b, i: (b, i, 0)
    drow = lambda b, i: (0, b, i, 0)
    tok16 = jax.ShapeDtypeStruct((B, S, W), BF16)
    dtok = jax.ShapeDtypeStruct((2, B, S, W), F32)
    dtok16 = jax.ShapeDtypeStruct((2, B, S, W), BF16)
    return pl.pallas_call(
        _prep_kernel,
        grid=(B, S // tm),
        in_specs=[
            pl.BlockSpec((None, tm, C), row),
            pl.BlockSpec((None, 8, C), lambda b, i: (b, jnp.maximum(i * (tm // 8) - 1, 0), 0)),
            pl.BlockSpec((None, 8, C), lambda b, i: (b, jnp.minimum((i + 1) * (tm // 8), nblk8 - 1), 0)),
            pl.BlockSpec((1, C), const),
            pl.BlockSpec((1, C), const),
            pl.BlockSpec((1, 2 * W), const),
            pl.BlockSpec((LANES, 2 * W), const),
            pl.BlockSpec((1, 2 * W), const),
            pl.BlockSpec((LANES, 2 * W), const),
            pl.BlockSpec((LANES, W), const),
            pl.BlockSpec((1, W), const),
            pl.BlockSpec((1, W), const),
            pl.BlockSpec((1, W), const),
            pl.BlockSpec(e_seg.shape, const),
        ],
        out_specs=[pl.BlockSpec((None, tm, W), row)] * 5
                  + [pl.BlockSpec((2, None, tm, W), drow)] * 3,
        out_shape=[tok16, tok16, tok16, tok16, tok16, dtok, dtok16, dtok16],
        compiler_params=pltpu.CompilerParams(
            dimension_semantics=("parallel", "parallel"), vmem_limit_bytes=VMEM_LIMIT),
        name="rwkv_prep",
    )(u_r, u_r, u_r, mu_prev, mu_next, w0, w2bd, a0, a2bd, g2, k_k, k_a, r_k, e_seg)


def _scan_kernel(r_ref, v_ref, kk_ref, lw_ref, b_ref, kd_ref, y_ref,
                 t_scr, g_scr, h_scr, q_scr, yl_scr, *, nsub, n_pair, n_step):
    step = pl.program_id(0)
    last = pl.num_programs(0) - 2
    row_cur = jnp.minimum(step, last) // n_step
    step_prev = jnp.maximum(step - 1, 0)
    rev = (row_cur // n_pair) % 2
    rev_prev = ((step_prev // n_step) // n_pair) % 2
    sgn = 1 - 2 * rev
    C = CHUNK
    P2 = 2 * C

    @pl.when(step == 0)
    def _():
        t_scr[...] = jnp.zeros_like(t_scr)
        g_scr[...] = jnp.zeros_like(g_scr)
        h_scr[...] = jnp.zeros_like(h_scr)
        q_scr[...] = jnp.zeros_like(q_scr)
        yl_scr[...] = jnp.zeros_like(yl_scr)

    def slices(direction):
        out = []
        for s in range(nsub):
            ci = s + direction * (nsub - 1 - 2 * s)
            out.append(pl.ds(pl.multiple_of(ci * C, C), C))
        return out

    chunk_slices = slices(rev)
    chain_slices = slices(rev_prev)

    chain = {"t": jnp.where(step_prev % n_step == 0, 0.0, t_scr[...]), "next": 0}

    def chain_steps(n):
        for _ in range(n):
            s = chain["next"]
            if s == nsub:
                return
            t_cur = chain["t"]
            t16 = t_cur.astype(BF16)
            y_st = jnp.dot(q_scr[s], t16, preferred_element_type=F32) + yl_scr[s]
            chain["t"] = jnp.dot(g_scr[s], t16, preferred_element_type=F32) + h_scr[s]
            y_ref[chain_slices[s], :] = y_st[:C, :] + y_st[C:, :]
            chain["next"] = s + 1

    row = lax.broadcasted_iota(jnp.int32, (P2, P2), 0)
    col = lax.broadcasted_iota(jnp.int32, (P2, P2), 1)
    rt = row & (C - 1)
    ct = col & (C - 1)
    delta = (rt - ct) * sgn
    strict = delta > 0
    incl = delta >= 0
    eye = row == col
    ident = jnp.where(eye, 1.0, 0.0)
    blk8 = (rt >> 3) == (ct >> 3)
    blk16 = (rt >> 4) == (ct >> 4)
    blk32 = (rt >> 5) == (ct >> 5)
    in8 = strict & blk8
    in16 = strict & blk16 & ~blk8
    in32 = strict & blk32 & ~blk16
    in64 = strict & ~blk32
    lane = lax.broadcasted_iota(jnp.int32, (C, LANES), 1)
    head0 = lane < HEAD

    def stack(x):
        return jnp.concatenate([jnp.where(head0, x, 0.0), jnp.where(head0, 0.0, x)], axis=0)

    subs = range(nsub)
    sls = chunk_slices

    lws = [lw_ref[sl, :] for sl in sls]
    tots = [jnp.sum(lw, axis=0, keepdims=True) for lw in lws]
    tok = lax.broadcasted_iota(jnp.int32, (C, LANES), 0)
    cums = []
    for lw, tot in zip(lws, tots):
        pre = lw
        shift = 1
        while shift < C:
            pre = pre + jnp.where(tok >= shift, pltpu.roll(pre, shift, 0), 0.0)
            shift *= 2
        cums.append(jnp.where(rev == 1, tot - pre + lw, pre))
    a_st, r_st, b_st, k_st, bh_t, kh_t, v_st = [], [], [], [], [], [], []
    for s in subs:
        sl, lw, cum, tot = sls[s], lws[s], cums[s], tots[s]
        e_neg = jnp.exp(-cum)
        e_rem = jnp.exp(tot - cum)
        b = b_ref[sl, :]
        kd = kd_ref[sl, :]
        a_st.append(stack(kk_ref[sl, :] * -jnp.exp(cum - lw)).astype(BF16))
        r_st.append(stack(r_ref[sl, :] * jnp.exp(cum)))
        b_st.append(stack(b * e_neg).astype(BF16))
        k_st.append(stack(kd * e_neg).astype(BF16))
        bh_t.append(stack(b * e_rem).T.astype(BF16))
        kh_t.append(stack(kd * e_rem).T.astype(BF16))
        v_st.append(stack(v_ref[sl, :]).astype(BF16))

    nt = (((1,), (1,)), ((), ()))
    sc = [lax.dot_general(jnp.concatenate([a_st[s], r_st[s].astype(BF16)], axis=0),
                          jnp.concatenate([b_st[s], k_st[s]], axis=0), nt,
                          preferred_element_type=F32) for s in subs]
    per_stage = -(-nsub // 8)
    chain_steps(per_stage)
    l_ab = [x[:P2, :P2] for x in sc]
    l_ak = [jnp.where(strict, x[:P2, P2:], 0.0).astype(BF16) for x in sc]
    a_rb = [jnp.where(incl, x[P2:, :P2], 0.0).astype(BF16) for x in sc]
    a_rk = [jnp.where(incl, x[P2:, P2:], 0.0).astype(BF16) for x in sc]
    x_loc = [jnp.dot(l_ak[s], v_st[s], preferred_element_type=F32) for s in subs]
    chain_steps(per_stage)

    d8 = [jnp.where(in8, x, 0.0) for x in l_ab]
    m1 = [ident + x for x in d8]
    p1 = [_mm(x, x) for x in d8]
    chain_steps(per_stage)
    pm = [_mm(p1[s], jnp.concatenate([p1[s], m1[s]], axis=1)) for s in subs]
    m2 = [m1[s] + pm[s][:, P2:] for s in subs]
    m8 = [m2[s] + _mm(pm[s][:, :P2], m2[s]) for s in subs]
    chain_steps(per_stage)
    o16 = [jnp.where(in16, x, 0.0) for x in l_ab]
    om = [_mm(o16[s], m8[s]) for s in subs]
    m16 = [m8[s] + _mm(m8[s], om[s]) for s in subs]
    chain_steps(per_stage)
    o32 = [jnp.where(in32, x, 0.0) for x in l_ab]
    om = [_mm(o32[s], m16[s]) for s in subs]
    m32 = [(m16[s] + _mm(m16[s], om[s])).astype(BF16) for s in subs]
    chain_steps(per_stage)
    o64 = [jnp.where(in64, x, 0.0).astype(BF16) for x in l_ab]
    y1 = [jnp.dot(m32[s], jnp.concatenate([a_st[s], x_loc[s].astype(BF16)], axis=1),
                  preferred_element_type=F32) for s in subs]
    oy = [_mm(o64[s], y1[s]).astype(BF16) for s in subs]
    chain_steps(per_stage)
    wu16 = [(y1[s] + jnp.dot(m32[s], oy[s], preferred_element_type=F32)).astype(BF16)
            for s in subs]
    chain_steps(per_stage)
    zero_blk = jnp.zeros((P2, LANES), BF16)
    fin = [jnp.dot(jnp.concatenate([jnp.concatenate([a_rb[s], a_rk[s]], axis=1),
                                    jnp.concatenate([bh_t[s], kh_t[s]], axis=1)], axis=0),
                   jnp.concatenate([wu16[s], jnp.concatenate([zero_blk, v_st[s]], axis=1)], axis=0),
                   preferred_element_type=F32) for s in subs]
    chain_steps(nsub)
    t_scr[...] = chain["t"]
    for s in subs:
        q_scr[s] = (r_st[s] + fin[s][:P2, :LANES]).astype(BF16)
        yl_scr[s] = fin[s][:P2, LANES:]
        g_scr[s] = (jnp.where(eye, jnp.exp(tots[s]), 0.0) + fin[s][P2:, :LANES]).astype(BF16)
        h_scr[s] = fin[s][P2:, LANES:]


def _rwkv_scan(r, v, kk, lw, bb, kd, nsub):
    B, S, W = r.shape
    n_pair = W // LANES
    T = nsub * CHUNK
    n_step = S // T

    n_total = B * 2 * n_pair * n_step

    def locate(t):
        g = t // n_step
        c = t % n_step
        d = (g // n_pair) % 2
        return d, g // (2 * n_pair), c + d * (n_step - 1 - 2 * c), g % n_pair

    def shared(t):
        _, b, blk, p = locate(jnp.minimum(t, n_total - 1))
        return (b, blk, p)

    def per_dir(t):
        return locate(jnp.minimum(t, n_total - 1))

    def out_map(t):
        return locate(jnp.maximum(t - 1, 0))

    mat = pltpu.VMEM((nsub, LANES, LANES), F32)
    mat16 = pltpu.VMEM((nsub, LANES, LANES), BF16)
    return pl.pallas_call(
        functools.partial(_scan_kernel, nsub=nsub, n_pair=n_pair, n_step=n_step),
        grid=(n_total + 1,),
        in_specs=[pl.BlockSpec((None, T, LANES), shared)] * 3
                 + [pl.BlockSpec((None, None, T, LANES), per_dir)] * 3,
        out_specs=pl.BlockSpec((None, None, T, LANES), out_map),
        out_shape=jax.ShapeDtypeStruct((2, B, S, W), F32),
        scratch_shapes=[pltpu.VMEM((LANES, LANES), F32), mat16, mat, mat16, mat],
        compiler_params=pltpu.CompilerParams(
            dimension_semantics=("arbitrary",), vmem_limit_bytes=VMEM_LIMIT),
        name="rwkv_scan",
    )(r, v, kk, lw, bb, kd)


def _attn_kernel(q_ref, qn_ref, k_ref, vt_ref, lq1_ref, lk1_ref, lq2_ref, lk2_ref, sw_ref, o_ref,
                 m_scr, alpha_scr, acc_scr, s_even, s_odd, p_even, p_odd, mx_even, mx_odd,
                 *, tk, lambda_init):
    n_kv = k_ref.shape[0] // tk
    dv = vt_ref.shape[0]
    n_streams = 2
    lane = lax.broadcasted_iota(jnp.int32, q_ref.shape, 1)

    def component_queries(ref):
        q = ref[...]
        zero = jnp.zeros_like(q)
        return [jnp.where(lane < HEAD, q, zero), jnp.where(lane < HEAD, zero, q)]

    qc = component_queries(q_ref)
    m_scr[...] = jnp.full(m_scr.shape, -jnp.inf, F32)
    acc_scr[...] = jnp.zeros(acc_scr.shape, F32)

    def scores(j, u, queries):
        ks = k_ref[pl.ds(pl.multiple_of(j * tk, tk), tk), :]
        return lax.dot_general(ks, queries[u], (((1,), (1,)), ((), ())),
                               preferred_element_type=F32)

    ones_rows = jnp.ones((SUM_ROWS, tk), BF16)

    def weighted_values(j, p_ref, u):
        vts = vt_ref[:, pl.ds(pl.multiple_of(j * tk, tk), tk)]
        vts = jnp.concatenate([vts, ones_rows], axis=0)
        acc_scr[u] = alpha_scr[u] * acc_scr[u] + jnp.dot(vts, p_ref[u],
                                                         preferred_element_type=F32)

    def put_scores(j, s_ref, mx_ref, u, queries):
        s = scores(j, u, queries)
        s_ref[u] = s
        mx_ref[u] = jnp.max(s, axis=0, keepdims=True)

    def tile_step(j, s_cur, mx_cur, s_nxt, mx_nxt, p_cur, p_prev, first=False, last=False):
        for u in range(n_streams):
            if not first:
                weighted_values(j - 1, p_prev, u)
            if not last:
                put_scores(j + 1, s_nxt, mx_nxt, u, qc)
        for u in range(n_streams):
            m_old = m_scr[u]
            m_new = jnp.maximum(m_old, mx_cur[u])
            alpha = jnp.exp2(m_old - m_new)
            p = jnp.exp2(s_cur[u] - m_new)
            p_cur[u] = p.astype(BF16)
            alpha_scr[u] = alpha
            m_scr[u] = m_new

    qc_next = component_queries(qn_ref)

    @pl.when(pl.program_id(2) == 0)
    def _():
        for u in range(n_streams):
            put_scores(0, s_even, mx_even, u, qc)

    unroll = TILES_PER_ITER if n_kv % TILES_PER_ITER == 0 else 2

    def steps(base, head=False, tail=False):
        for t in range(0, unroll, 2):
            tile_step(base + t, s_even, mx_even, s_odd, mx_odd, p_even, p_odd,
                      first=head and t == 0)
            tile_step(base + t + 1, s_odd, mx_odd, s_even, mx_even, p_odd, p_even,
                      last=tail and t + 2 == unroll)

    def body(i, carry):
        steps(unroll * i)
        return carry

    n_groups = n_kv // unroll
    if n_groups == 1:
        steps(0, head=True, tail=True)
    else:
        steps(0, head=True)
        lax.fori_loop(1, n_groups - 1, body, 0)
        steps(n_kv - unroll, tail=True)
    for u in range(n_streams):
        weighted_values(n_kv - 1, p_odd, u)
    for u in range(n_streams):
        put_scores(0, s_even, mx_even, u, qc_next)

    lam = (jnp.exp(jnp.sum(lq1_ref[...] * lk1_ref[...], axis=-1, keepdims=True))
           - jnp.exp(jnp.sum(lq2_ref[...] * lk2_ref[...], axis=-1, keepdims=True)) + lambda_init)
    o = (acc_scr[0, :dv, :] / acc_scr[0, dv:dv + 1, :]
         - lam * (acc_scr[1, :dv, :] / acc_scr[1, dv:dv + 1, :]))
    o = o * lax.rsqrt(jnp.mean(o * o, axis=0, keepdims=True) + SUBLN_EPS)
    o = o * (sw_ref[...] * (1.0 - lambda_init))
    o_ref[...] = o.T.astype(o_ref.dtype)


def _diff_attn(q, k, vt, lq1, lk1, lq2, lk2, subln_w, lambda_init, tq, tk):
    B, S, QW = q.shape
    H = QW // LANES
    DV = vt.shape[1] // H
    ns = 2
    const = lambda b, h, i: (0, 0)
    return pl.pallas_call(
        functools.partial(_attn_kernel, tk=tk, lambda_init=lambda_init),
        grid=(B, H, S // tq),
        in_specs=[
            pl.BlockSpec((None, tq, LANES), lambda b, h, i: (b, i, h)),
            pl.BlockSpec((None, tq, LANES),
                         lambda b, h, i: (b, jnp.minimum(i + 1, S // tq - 1), h)),
            pl.BlockSpec((None, S, LANES), lambda b, h, i: (b, 0, h)),
            pl.BlockSpec((None, DV, S), lambda b, h, i: (b, h, 0)),
            pl.BlockSpec((1, HEAD), const),
            pl.BlockSpec((1, HEAD), const),
            pl.BlockSpec((1, HEAD), const),
            pl.BlockSpec((1, HEAD), const),
            pl.BlockSpec((DV, 1), const),
        ],
        out_specs=pl.BlockSpec((None, tq, DV), lambda b, h, i: (b, i, h)),
        out_shape=jax.ShapeDtypeStruct((B, S, H * DV), BF16),
        scratch_shapes=[pltpu.VMEM((ns, 1, tq), F32),
                        pltpu.VMEM((ns, 1, tq), F32),
                        pltpu.VMEM((ns, DV + SUM_ROWS, tq), F32),
                        pltpu.VMEM((ns, tk, tq), F32),
                        pltpu.VMEM((ns, tk, tq), F32),
                        pltpu.VMEM((ns, tk, tq), BF16),
                        pltpu.VMEM((ns, tk, tq), BF16),
                        pltpu.VMEM((ns, 1, tq), F32),
                        pltpu.VMEM((ns, 1, tq), F32)],
        compiler_params=pltpu.CompilerParams(
            dimension_semantics=("parallel", "parallel", "arbitrary"),
            vmem_limit_bytes=VMEM_LIMIT),
        name="diff_attn",
    )(q, q, k, vt, lq1, lk1, lq2, lk2, subln_w)


def _merge_kernel(x_ref, y_ref, bonus_ref, g_ref, ob_ref, nm_ref, wg_ref, lnw_ref, lnb_ref,
                  e_ref, woa_ref, wob_ref, wout_ref, o_ref):
    x = x_ref[...]
    D = x.shape[-1]
    h = _rms(x, nm_ref[...], RMS_EPS).astype(BF16)
    n_in = wg_ref.shape[1]
    gates = _sigmoid(jnp.dot(h, wg_ref[:, n_in - 2 * D:], preferred_element_type=F32))
    e = e_ref[...]
    y = y_ref[0] + y_ref[1]
    mean = _seg_sum(y, e) * (1.0 / HEAD)
    yc = y - mean
    var = _seg_sum(yc * yc, e) * (1.0 / HEAD)
    yn = yc * lax.rsqrt(var + GN_EPS) * lnw_ref[...] + lnb_ref[...]
    ya = _mm((yn + bonus_ref[...]) * g_ref[...], woa_ref[...])
    yb = jnp.dot(ob_ref[...], wob_ref[...], preferred_element_type=F32)
    merged = gates[:, :D] * ya + gates[:, D:] * yb
    o_ref[...] = x + _mm(merged, wout_ref[...])


def _merge(x, y, bonus, g, ob, norm_mix, w_g, ln_w, ln_b, e_seg, w_oa, w_ob, w_out, tm):
    B, S, D = x.shape
    W = bonus.shape[-1]
    const = lambda b, i: (0, 0)
    row = lambda b, i: (b, i, 0)
    return pl.pallas_call(
        _merge_kernel,
        grid=(B, S // tm),
        in_specs=[
            pl.BlockSpec((None, tm, D), row),
            pl.BlockSpec((2, None, tm, W), lambda b, i: (0, b, i, 0)),
            pl.BlockSpec((None, tm, W), row),
            pl.BlockSpec((None, tm, W), row),
            pl.BlockSpec((None, tm, W), row),
            pl.BlockSpec((1, D), const),
            pl.BlockSpec(w_g.shape, const),
            pl.BlockSpec((1, W), const),
            pl.BlockSpec((1, W), const),
            pl.BlockSpec(e_seg.shape, const),
            pl.BlockSpec((W, D), const),
            pl.BlockSpec((W, D), const),
            pl.BlockSpec((D, D), const),
        ],
        out_specs=pl.BlockSpec((None, tm, D), row),
        out_shape=jax.ShapeDtypeStruct((B, S, D), F32),
        compiler_params=pltpu.CompilerParams(
            dimension_semantics=("parallel", "parallel"), vmem_limit_bytes=VMEM_LIMIT),
        name="merge",
    )(x, y, bonus, g, ob, norm_mix, w_g, ln_w, ln_b, e_seg, w_oa, w_ob, w_out)


def _ffn_kernel(x_ref, p_ref, nf_ref, w1_ref, w2_ref, np_ref, wpg_ref, wpp_ref, nfin_ref, o_ref,
                *, n_chunk):
    x = x_ref[...]
    h = _rms(x, nf_ref[...], RMS_EPS).astype(BF16)
    ff = w1_ref.shape[1] // n_chunk
    acc = x
    for c in range(n_chunk):
        t = jnp.maximum(jnp.dot(h, w1_ref[:, c * ff:(c + 1) * ff], preferred_element_type=F32), 0.0)
        acc = acc + jnp.dot((t * t).astype(BF16), w2_ref[c * ff:(c + 1) * ff, :],
                            preferred_element_type=F32)
    x = acc
    h = _rms(x, np_ref[...], RMS_EPS).astype(BF16)
    gate = _sigmoid(jnp.dot(h, wpg_ref[...], preferred_element_type=F32))
    x = x + gate * _mm(p_ref[...], wpp_ref[...])
    o_ref[...] = x if nfin_ref is None else _rms(x, nfin_ref[...], RMS_EPS)


def _ffn_kernel_plain(x_ref, p_ref, nf_ref, w1_ref, w2_ref, np_ref, wpg_ref, wpp_ref, o_ref, *, n_chunk):
    _ffn_kernel(x_ref, p_ref, nf_ref, w1_ref, w2_ref, np_ref, wpg_ref, wpp_ref, None, o_ref,
                n_chunk=n_chunk)


def _ffn(x, p, norm_ffn, w1, w2, norm_ple, w_pg, w_pp, norm_final, tm, n_chunk):
    B, S, D = x.shape
    PD = p.shape[-1]
    FF = w1.shape[1]
    const = lambda b, i: (0, 0)
    row = lambda b, i: (b, i, 0)
    in_specs = [
        pl.BlockSpec((None, tm, D), row),
        pl.BlockSpec((None, tm, PD), row),
        pl.BlockSpec((1, D), const),
        pl.BlockSpec((D, FF), const, pipeline_mode=pl.Buffered(1)),
        pl.BlockSpec((FF, D), const, pipeline_mode=pl.Buffered(1)),
        pl.BlockSpec((1, D), const),
        pl.BlockSpec((D, D), const, pipeline_mode=pl.Buffered(1)),
        pl.BlockSpec((PD, D), const, pipeline_mode=pl.Buffered(1)),
    ]
    args = [x, p, norm_ffn, w1, w2, norm_ple, w_pg, w_pp]
    if norm_final is not None:
        in_specs.append(pl.BlockSpec((1, D), const))
        args.append(norm_final)
        body = functools.partial(_ffn_kernel, n_chunk=n_chunk)
    else:
        body = functools.partial(_ffn_kernel_plain, n_chunk=n_chunk)
    return pl.pallas_call(
        body,
        grid=(B, S // tm),
        in_specs=in_specs,
        out_specs=pl.BlockSpec((None, tm, D), row),
        out_shape=jax.ShapeDtypeStruct((B, S, D), F32),
        compiler_params=pltpu.CompilerParams(
            dimension_semantics=("parallel", "parallel"), vmem_limit_bytes=VMEM_LIMIT),
        name="ffn",
    )(*args)


def _block_diag2(w):
    z = jnp.zeros_like(w[0])
    return jnp.concatenate([jnp.concatenate([w[0], z], axis=1),
                            jnp.concatenate([z, w[1]], axis=1)], axis=0)


def _tiles(S):
    tm = min(TOKEN_TILE, S)
    tq = min(ATTN_Q_TILE, S)
    tk = min(ATTN_K_TILE, S // 2)
    nsub = min(SCAN_CHUNKS_PER_STEP, S // CHUNK)
    assert S % tm == 0 and S % tq == 0 and (S // tk) % 2 == 0 and S % (nsub * CHUNK) == 0, S
    return tm, tq, tk, nsub


def kernel(x, p, norm_mix, w_in, shift_mu_prev, shift_mu_next, rwkv_w0, rwkv_w2, rwkv_a0,
           rwkv_a2, rwkv_g2, rwkv_k_k, rwkv_k_a, rwkv_r_k, rwkv_ln_w, rwkv_ln_b, rwkv_w_o,
           da_lq1, da_lk1, da_lq2, da_lk2, da_subln_w, da_w_o, w_out, norm_ffn, w_ff1, w_ff2,
           norm_ple, w_ple_gate, w_ple_proj, norm_final):
    B, S, D = x.shape
    L = w_in.shape[0]
    W = rwkv_w0.shape[-1]
    n_r = shift_mu_prev.shape[-1]
    n_qk = 2 * da_w_o.shape[1]
    n_v = da_w_o.shape[1]
    cos_t, sin_t = _rope_tables(S)
    seg = jnp.arange(SEG_BLOCK) // HEAD
    e_seg = (seg[:, None] == seg[None, :]).astype(BF16)
    tm, tq, tk, nsub = _tiles(S)
    for i in range(L):
        lambda_init = 0.8 - 0.6 * math.exp(-0.3 * i)
        w_i = w_in[i].astype(BF16)
        u_r, q, k, v = _in_proj(
            x, norm_mix[i][None], w_i, w_i[:, n_r + n_qk:n_r + n_qk + n_v].T, n_r, n_qk,
            cos_t, sin_t, tm)
        r, vv, kk, g, bonus, lw, bb, kd = _rwkv_prep(
            u_r, shift_mu_prev[i][None], shift_mu_next[i][None],
            rwkv_w0[i].reshape(1, 2 * W), _block_diag2(rwkv_w2[i]).astype(BF16),
            rwkv_a0[i].reshape(1, 2 * W), _block_diag2(rwkv_a2[i]).astype(BF16),
            rwkv_g2[i].astype(BF16), rwkv_k_k[i][None], rwkv_k_a[i][None],
            rwkv_r_k[i].reshape(1, W), e_seg, tm)
        y = _rwkv_scan(r, vv, kk, lw, bb, kd, nsub=nsub)
        ob = _diff_attn(q, k, v, da_lq1[i][None], da_lk1[i][None], da_lq2[i][None],
                        da_lk2[i][None], da_subln_w[i][:, None], lambda_init, tq=tq, tk=tk)
        x = _merge(x, y, bonus, g, ob, norm_mix[i][None],
                   w_i, rwkv_ln_w[i][None], rwkv_ln_b[i][None], e_seg,
                   rwkv_w_o[i].astype(BF16), da_w_o[i].astype(BF16), w_out[i].astype(BF16), tm)
        x = _ffn(x, p[i], norm_ffn[i][None], w_ff1[i].astype(BF16), w_ff2[i].astype(BF16),
                 norm_ple[i][None], w_ple_gate[i].astype(BF16), w_ple_proj[i].astype(BF16),
                 norm_final[None] if i == L - 1 else None, tm, n_chunk=FFN_CHUNKS)
    return x
```

```python
import functools
import math

import jax
import jax.numpy as jnp
from jax import lax
from jax.experimental import pallas as pl
from jax.experimental.pallas import tpu as pltpu

F32 = jnp.float32
BF16 = jnp.bfloat16

HEAD = 64
LANES = 128
CHUNK = 64
RMS_EPS = 1e-6
GN_EPS = 64e-5
SUBLN_EPS = 1e-5
ROPE_THETA = 500000.0
ROPE_DIM = 16
Q_SCALE = HEAD ** -0.5 * math.log2(math.e)
DECAY_SCALE = math.exp(-0.5)
VMEM_LIMIT = 56 * 1024 * 1024
TOKEN_TILE = 512
ATTN_Q_TILE = 1024
ATTN_K_TILE = 512
SCAN_CHUNKS_PER_STEP = 16
FFN_CHUNKS = 4
TILES_PER_ITER = 8
SUM_ROWS = 16
HALO = 8
SEG_BLOCK = 256


def _rms(x, g, eps):
    return x * lax.rsqrt(jnp.mean(x * x, axis=-1, keepdims=True) + eps) * g


def _sigmoid(x):
    return 0.5 * jnp.tanh(0.5 * x) + 0.5


def _mm(a, b):
    return jnp.dot(a.astype(BF16), b.astype(BF16), preferred_element_type=F32)


def _split2(x):
    hi = x.astype(BF16)
    lo = (x - hi.astype(F32)).astype(BF16)
    return hi, lo


def _mm_exact_rhs(a, b_bf16):
    hi, lo = _split2(a)
    return (jnp.dot(hi, b_bf16, preferred_element_type=F32)
            + jnp.dot(lo, b_bf16, preferred_element_type=F32))


def _seg_sum(x, e):
    n = e.shape[0]
    return jnp.concatenate([_mm_exact_rhs(x[:, c:c + n], e) for c in range(0, x.shape[1], n)],
                           axis=1)


def _inproj_kernel(x_ref, g_ref, w_ref, wv_ref, cos_ref, sin_ref,
                   ur_ref, q_ref, k_ref, v_ref):
    n_r = ur_ref.shape[1]
    n_qk = 2 * q_ref.shape[1]
    h = _rms(x_ref[...], g_ref[...], RMS_EPS).astype(BF16)
    ur_ref[...] = jnp.dot(h, w_ref[:, :n_r], preferred_element_type=F32)
    v_ref[...] = lax.dot_general(wv_ref[...], h, (((1,), (1,)), ((), ())),
                                 preferred_element_type=F32).astype(BF16)
    cos_t = cos_ref[...]
    sin_t = sin_ref[...]
    lane = lax.broadcasted_iota(jnp.int32, cos_t.shape, 1)
    first_half = (lane & (HEAD - 1)) < (ROPE_DIM // 2)
    n_tiles = n_qk // LANES
    qk = jnp.dot(h, w_ref[:, n_r:n_r + n_qk], preferred_element_type=F32)
    for c in range(n_tiles):
        xq = qk[:, c * LANES:(c + 1) * LANES]
        partner = jnp.where(first_half,
                            pltpu.roll(xq, LANES - ROPE_DIM // 2, 1),
                            pltpu.roll(xq, ROPE_DIM // 2, 1))
        ro = xq * cos_t + partner * sin_t
        if c < n_tiles // 2:
            q_ref[:, c * LANES:(c + 1) * LANES] = (ro * Q_SCALE).astype(BF16)
        else:
            cc = c - n_tiles // 2
            k_ref[:, cc * LANES:(cc + 1) * LANES] = ro.astype(BF16)


def _rope_tables(S):
    pos = jnp.arange(S, dtype=F32)
    inv_freq = ROPE_THETA ** (-jnp.arange(0, ROPE_DIM, 2, dtype=F32) / ROPE_DIM)
    ang = pos[:, None] * inv_freq[None, :]
    cos8, sin8 = jnp.cos(ang), jnp.sin(ang)
    ones = jnp.ones((S, HEAD - ROPE_DIM), F32)
    zeros = jnp.zeros((S, HEAD - ROPE_DIM), F32)
    cos64 = jnp.concatenate([cos8, cos8, ones], axis=1)
    sin64 = jnp.concatenate([-sin8, sin8, zeros], axis=1)
    return jnp.tile(cos64, (1, 2)), jnp.tile(sin64, (1, 2))


def _in_proj(x, g, w_all, w_vt, nr, nqk, cos_t, sin_t, tm):
    B, S, D = x.shape
    nv = w_vt.shape[0]
    const = lambda b, i: (0, 0)
    row = lambda b, i: (b, i, 0)
    return pl.pallas_call(
        _inproj_kernel,
        grid=(B, S // tm),
        in_specs=[
            pl.BlockSpec((None, tm, D), row),
            pl.BlockSpec((1, D), const),
            pl.BlockSpec(w_all.shape, const),
            pl.BlockSpec((nv, D), const),
            pl.BlockSpec((tm, LANES), lambda b, i: (i, 0)),
            pl.BlockSpec((tm, LANES), lambda b, i: (i, 0)),
        ],
        out_specs=[
            pl.BlockSpec((None, tm, nr), row),
            pl.BlockSpec((None, tm, nqk // 2), row),
            pl.BlockSpec((None, tm, nqk // 2), row),
            pl.BlockSpec((None, nv, tm), lambda b, i: (b, 0, i)),
        ],
        out_shape=[
            jax.ShapeDtypeStruct((B, S, nr), F32),
            jax.ShapeDtypeStruct((B, S, nqk // 2), BF16),
            jax.ShapeDtypeStruct((B, S, nqk // 2), BF16),
            jax.ShapeDtypeStruct((B, nv, S), BF16),
        ],
        compiler_params=pltpu.CompilerParams(
            dimension_semantics=("parallel", "parallel"), vmem_limit_bytes=VMEM_LIMIT),
        name="in_proj",
    )(x, g, w_all, w_vt, cos_t, sin_t)


def _prep_kernel(u_ref, up_ref, un_ref, mup_ref, mun_ref, w0_ref, w2_ref, a0_ref, a2_ref,
                 g2_ref, kk_w_ref, ka_ref, rk_ref, e_ref,
                 r_ref, v_ref, kk_ref, g_ref, bonus_ref, lw_ref, b_ref, kd_ref):
    i = pl.program_id(1)
    n = pl.num_programs(1)
    u = u_ref[...]
    tm = u.shape[0]
    W = r_ref.shape[-1]
    row8 = lax.broadcasted_iota(jnp.int32, (HALO, 1), 0)
    prev_row = jnp.where(i > 0, up_ref[HALO - 1:HALO, :], 0.0)
    next_row = jnp.where(i < n - 1, un_ref[0:1, :], 0.0)
    rolled = pltpu.roll(u, 1, 0)
    u_prev = jnp.concatenate([jnp.where(row8 == 0, prev_row, rolled[:HALO]), rolled[HALO:]], axis=0)
    rolled = pltpu.roll(u, tm - 1, 0)
    u_next = jnp.concatenate([rolled[:tm - HALO],
                              jnp.where(row8 == HALO - 1, next_row, rolled[tm - HALO:])], axis=0)
    mup = mup_ref[...]
    mun = mun_ref[...]
    us = (1.0 - mup - mun) * u + mup * u_prev + mun * u_next

    r = us[:, 0:W]
    k = us[:, W:2 * W]
    v = us[:, 2 * W:3 * W]
    wd = us[:, 3 * W:3 * W + LANES]
    ad = us[:, 3 * W + LANES:3 * W + 2 * LANES]
    gd = us[:, 3 * W + 2 * LANES:3 * W + 3 * LANES]

    z = w0_ref[...] + _mm(jnp.tanh(wd), w2_ref[...])
    lw = (-0.5 * DECAY_SCALE) * jnp.tanh(0.5 * z) - 0.5 * DECAY_SCALE
    a = _sigmoid(a0_ref[...] + _mm(ad, a2_ref[...]))
    g_ref[...] = _mm(_sigmoid(gd), g2_ref[...]).astype(g_ref.dtype)

    e = e_ref[...]
    kraw = k * kk_w_ref[...]
    ss = _seg_sum(kraw * kraw, e)
    kk = kraw / jnp.maximum(jnp.sqrt(ss), 1e-12)
    ka = ka_ref[...]
    kd_sum = jnp.zeros_like(k)
    for d in range(2):
        a_d = a[:, d * W:(d + 1) * W]
        kd = k * (1.0 + (a_d - 1.0) * ka)
        kd_sum = kd_sum + kd
        lw_ref[d] = lw[:, d * W:(d + 1) * W]
        b_ref[d] = (kk * a_d).astype(b_ref.dtype)
        kd_ref[d] = kd.astype(kd_ref.dtype)
    c = _seg_sum(r * rk_ref[...] * kd_sum, e)
    r_ref[...] = r.astype(r_ref.dtype)
    v_ref[...] = v.astype(v_ref.dtype)
    kk_ref[...] = kk.astype(kk_ref.dtype)
    bonus_ref[...] = (c * v).astype(bonus_ref.dtype)


def _rwkv_prep(u_r, mu_prev, mu_next, w0, w2bd, a0, a2bd, g2, k_k, k_a, r_k, e_seg, tm):
    B, S, C = u_r.shape
    W = k_k.shape[-1]
    nblk8 = S // 8
    const = lambda b, i: (0, 0)
    row = lambda b, i: (b, i, 0)
    drow = lambda b, i: (0, b, i, 0)
    tok16 = jax.ShapeDtypeStruct((B, S, W), BF16)
    dtok = jax.ShapeDtypeStruct((2, B, S, W), F32)
    dtok16 = jax.ShapeDtypeStruct((2, B, S, W), BF16)
    return pl.pallas_call(
        _prep_kernel,
        grid=(B, S // tm),
        in_specs=[
            pl.BlockSpec((None, tm, C), row),
            pl.BlockSpec((None, 8, C), lambda b, i: (b, jnp.maximum(i * (tm // 8) - 1, 0), 0)),
            pl.BlockSpec((None, 8, C), lambda b, i: (b, jnp.minimum((i + 1) * (tm // 8), nblk8 - 1), 0)),
            pl.BlockSpec((1, C), const),
            pl.BlockSpec((1, C), const),
            pl.BlockSpec((1, 2 * W), const),
            pl.BlockSpec((LANES, 2 * W), const),
            pl.BlockSpec((1, 2 * W), const),
            pl.BlockSpec((LANES, 2 * W), const),
            pl.BlockSpec((LANES, W), const),
            pl.BlockSpec((1, W), const),
            pl.BlockSpec((1, W), const),
            pl.BlockSpec((1, W), const),
            pl.BlockSpec(e_seg.shape, const),
        ],
        out_specs=[pl.BlockSpec((None, tm, W), row)] * 5
                  + [pl.BlockSpec((2, None, tm, W), drow)] * 3,
        out_shape=[tok16, tok16, tok16, tok16, tok16, dtok, dtok16, dtok16],
        compiler_params=pltpu.CompilerParams(
            dimension_semantics=("parallel", "parallel"), vmem_limit_bytes=VMEM_LIMIT),
        name="rwkv_prep",
    )(u_r, u_r, u_r, mu_prev, mu_next, w0, w2bd, a0, a2bd, g2, k_k, k_a, r_k, e_seg)


def _scan_kernel(r_ref, v_ref, kk_ref, lw_ref, b_ref, kd_ref, y_ref,
                 t_scr, g_scr, h_scr, q_scr, yl_scr, *, nsub, n_pair, n_step):
    step = pl.program_id(0)
    last = pl.num_programs(0) - 2
    row_cur = jnp.minimum(step, last) // n_step
    step_prev = jnp.maximum(step - 1, 0)
    rev = (row_cur // n_pair) % 2
    rev_prev = ((step_prev // n_step) // n_pair) % 2
    sgn = 1 - 2 * rev
    C = CHUNK
    P2 = 2 * C

    @pl.when(step == 0)
    def _():
        t_scr[...] = jnp.zeros_like(t_scr)
        g_scr[...] = jnp.zeros_like(g_scr)
        h_scr[...] = jnp.zeros_like(h_scr)
        q_scr[...] = jnp.zeros_like(q_scr)
        yl_scr[...] = jnp.zeros_like(yl_scr)

    def slices(direction):
        out = []
        for s in range(nsub):
            ci = s + direction * (nsub - 1 - 2 * s)
            out.append(pl.ds(pl.multiple_of(ci * C, C), C))
        return out

    chunk_slices = slices(rev)
    chain_slices = slices(rev_prev)

    chain = {"t": jnp.where(step_prev % n_step == 0, 0.0, t_scr[...]), "next": 0}

    def chain_steps(n):
        for _ in range(n):
            s = chain["next"]
            if s == nsub:
                return
            t_cur = chain["t"]
            t16 = t_cur.astype(BF16)
            y_st = jnp.dot(q_scr[s], t16, preferred_element_type=F32) + yl_scr[s]
            chain["t"] = jnp.dot(g_scr[s], t16, preferred_element_type=F32) + h_scr[s]
            y_ref[chain_slices[s], :] = y_st[:C, :] + y_st[C:, :]
            chain["next"] = s + 1

    row = lax.broadcasted_iota(jnp.int32, (P2, P2), 0)
    col = lax.broadcasted_iota(jnp.int32, (P2, P2), 1)
    rt = row & (C - 1)
    ct = col & (C - 1)
    delta = (rt - ct) * sgn
    strict = delta > 0
    incl = delta >= 0
    eye = row == col
    ident = jnp.where(eye, 1.0, 0.0)
    blk8 = (rt >> 3) == (ct >> 3)
    blk16 = (rt >> 4) == (ct >> 4)
    blk32 = (rt >> 5) == (ct >> 5)
    in8 = strict & blk8
    in16 = strict & blk16 & ~blk8
    in32 = strict & blk32 & ~blk16
    in64 = strict & ~blk32
    lane = lax.broadcasted_iota(jnp.int32, (C, LANES), 1)
    head0 = lane < HEAD

    def stack(x):
        return jnp.concatenate([jnp.where(head0, x, 0.0), jnp.where(head0, 0.0, x)], axis=0)

    subs = range(nsub)
    sls = chunk_slices

    lws = [lw_ref[sl, :] for sl in sls]
    tots = [jnp.sum(lw, axis=0, keepdims=True) for lw in lws]
    tok = lax.broadcasted_iota(jnp.int32, (C, LANES), 0)
    cums = []
    for lw, tot in zip(lws, tots):
        pre = lw
        shift = 1
        while shift < C:
            pre = pre + jnp.where(tok >= shift, pltpu.roll(pre, shift, 0), 0.0)
            shift *= 2
        cums.append(jnp.where(rev == 1, tot - pre + lw, pre))
    a_st, r_st, b_st, k_st, bh_t, kh_t, v_st = [], [], [], [], [], [], []
    for s in subs:
        sl, lw, cum, tot = sls[s], lws[s], cums[s], tots[s]
        e_neg = jnp.exp(-cum)
        e_rem = jnp.exp(tot - cum)
        b = b_ref[sl, :]
        kd = kd_ref[sl, :]
        a_st.append(stack(kk_ref[sl, :] * -jnp.exp(cum - lw)).astype(BF16))
        r_st.append(stack(r_ref[sl, :] * jnp.exp(cum)))
        b_st.append(stack(b * e_neg).astype(BF16))
        k_st.append(stack(kd * e_neg).astype(BF16))
        bh_t.append(stack(b * e_rem).T.astype(BF16))
        kh_t.append(stack(kd * e_rem).T.astype(BF16))
        v_st.append(stack(v_ref[sl, :]).astype(BF16))

    nt = (((1,), (1,)), ((), ()))
    sc = [lax.dot_general(jnp.concatenate([a_st[s], r_st[s].astype(BF16)], axis=0),
                          jnp.concatenate([b_st[s], k_st[s]], axis=0), nt,
                          preferred_element_type=F32) for s in subs]
    per_stage = -(-nsub // 8)
    chain_steps(per_stage)
    l_ab = [x[:P2, :P2] for x in sc]
    l_ak = [jnp.where(strict, x[:P2, P2:], 0.0).astype(BF16) for x in sc]
    a_rb = [jnp.where(incl, x[P2:, :P2], 0.0).astype(BF16) for x in sc]
    a_rk = [jnp.where(incl, x[P2:, P2:], 0.0).astype(BF16) for x in sc]
    x_loc = [jnp.dot(l_ak[s], v_st[s], preferred_element_type=F32) for s in subs]
    chain_steps(per_stage)

    d8 = [jnp.where(in8, x, 0.0) for x in l_ab]
    m1 = [ident + x for x in d8]
    p1 = [_mm(x, x) for x in d8]
    chain_steps(per_stage)
    pm = [_mm(p1[s], jnp.concatenate([p1[s], m1[s]], axis=1)) for s in subs]
    m2 = [m1[s] + pm[s][:, P2:] for s in subs]
    m8 = [m2[s] + _mm(pm[s][:, :P2], m2[s]) for s in subs]
    chain_steps(per_stage)
    o16 = [jnp.where(in16, x, 0.0) for x in l_ab]
    om = [_mm(o16[s], m8[s]) for s in subs]
    m16 = [m8[s] + _mm(m8[s], om[s]) for s in subs]
    chain_steps(per_stage)
    o32 = [jnp.where(in32, x, 0.0) for x in l_ab]
    om = [_mm(o32[s], m16[s]) for s in subs]
    m32 = [(m16[s] + _mm(m16[s], om[s])).astype(BF16) for s in subs]
    chain_steps(per_stage)
    o64 = [jnp.where(in64, x, 0.0).astype(BF16) for x in l_ab]
    y1 = [jnp.dot(m32[s], jnp.concatenate([a_st[s], x_loc[s].astype(BF16)], axis=1),
                  preferred_element_type=F32) for s in subs]
    oy = [_mm(o64[s], y1[s]).astype(BF16) for s in subs]
    chain_steps(per_stage)
    wu16 = [(y1[s] + jnp.dot(m32[s], oy[s], preferred_element_type=F32)).astype(BF16)
            for s in subs]
    chain_steps(per_stage)
    zero_blk = jnp.zeros((P2, LANES), BF16)
    fin = [jnp.dot(jnp.concatenate([jnp.concatenate([a_rb[s], a_rk[s]], axis=1),
                                    jnp.concatenate([bh_t[s], kh_t[s]], axis=1)], axis=0),
                   jnp.concatenate([wu16[s], jnp.concatenate([zero_blk, v_st[s]], axis=1)], axis=0),
                   preferred_element_type=F32) for s in subs]
    chain_steps(nsub)
    t_scr[...] = chain["t"]
    for s in subs:
        q_scr[s] = (r_st[s] + fin[s][:P2, :LANES]).astype(BF16)
        yl_scr[s] = fin[s][:P2, LANES:]
        g_scr[s] = (jnp.where(eye, jnp.exp(tots[s]), 0.0) + fin[s][P2:, :LANES]).astype(BF16)
        h_scr[s] = fin[s][P2:, LANES:]


def _rwkv_scan(r, v, kk, lw, bb, kd, nsub):
    B, S, W = r.shape
    n_pair = W // LANES
    T = nsub * CHUNK
    n_step = S // T

    n_total = B * 2 * n_pair * n_step

    def locate(t):
        g = t // n_step
        c = t % n_step
        d = (g // n_pair) % 2
        return d, g // (2 * n_pair), c + d * (n_step - 1 - 2 * c), g % n_pair

    def shared(t):
        _, b, blk, p = locate(jnp.minimum(t, n_total - 1))
        return (b, blk, p)

    def per_dir(t):
        return locate(jnp.minimum(t, n_total - 1))

    def out_map(t):
        return locate(jnp.maximum(t - 1, 0))

    mat = pltpu.VMEM((nsub, LANES, LANES), F32)
    mat16 = pltpu.VMEM((nsub, LANES, LANES), BF16)
    return pl.pallas_call(
        functools.partial(_scan_kernel, nsub=nsub, n_pair=n_pair, n_step=n_step),
        grid=(n_total + 1,),
        in_specs=[pl.BlockSpec((None, T, LANES), shared)] * 3
                 + [pl.BlockSpec((None, None, T, LANES), per_dir)] * 3,
        out_specs=pl.BlockSpec((None, None, T, LANES), out_map),
        out_shape=jax.ShapeDtypeStruct((2, B, S, W), F32),
        scratch_shapes=[pltpu.VMEM((LANES, LANES), F32), mat16, mat, mat16, mat],
        compiler_params=pltpu.CompilerParams(
            dimension_semantics=("arbitrary",), vmem_limit_bytes=VMEM_LIMIT),
        name="rwkv_scan",
    )(r, v, kk, lw, bb, kd)


def _attn_kernel(q_ref, qn_ref, k_ref, vt_ref, lq1_ref, lk1_ref, lq2_ref, lk2_ref, sw_ref, o_ref,
                 m_scr, alpha_scr, acc_scr, s_even, s_odd, p_even, p_odd, mx_even, mx_odd,
                 *, tk, lambda_init):
    n_kv = k_ref.shape[0] // tk
    dv = vt_ref.shape[0]
    n_streams = 2
    lane = lax.broadcasted_iota(jnp.int32, q_ref.shape, 1)

    def component_queries(ref):
        q = ref[...]
        zero = jnp.zeros_like(q)
        return [jnp.where(lane < HEAD, q, zero), jnp.where(lane < HEAD, zero, q)]

    qc = component_queries(q_ref)
    m_scr[...] = jnp.full(m_scr.shape, -jnp.inf, F32)
    acc_scr[...] = jnp.zeros(acc_scr.shape, F32)

    def scores(j, u, queries):
        ks = k_ref[pl.ds(pl.multiple_of(j * tk, tk), tk), :]
        return lax.dot_general(ks, queries[u], (((1,), (1,)), ((), ())),
                               preferred_element_type=F32)

    ones_rows = jnp.ones((SUM_ROWS, tk), BF16)

    def weighted_values(j, p_ref, u):
        vts = vt_ref[:, pl.ds(pl.multiple_of(j * tk, tk), tk)]
        vts = jnp.concatenate([vts, ones_rows], axis=0)
        acc_scr[u] = alpha_scr[u] * acc_scr[u] + jnp.dot(vts, p_ref[u],
                                                         preferred_element_type=F32)

    def put_scores(j, s_ref, mx_ref, u, queries):
        s = scores(j, u, queries)
        s_ref[u] = s
        mx_ref[u] = jnp.max(s, axis=0, keepdims=True)

    def tile_step(j, s_cur, mx_cur, s_nxt, mx_nxt, p_cur, p_prev, first=False, last=False):
        for u in range(n_streams):
            if not first:
                weighted_values(j - 1, p_prev, u)
            if not last:
                put_scores(j + 1, s_nxt, mx_nxt, u, qc)
        for u in range(n_streams):
            m_old = m_scr[u]
            m_new = jnp.maximum(m_old, mx_cur[u])
            alpha = jnp.exp2(m_old - m_new)
            p = jnp.exp2(s_cur[u] - m_new)
            p_cur[u] = p.astype(BF16)
            alpha_scr[u] = alpha
            m_scr[u] = m_new

    qc_next = component_queries(qn_ref)

    @pl.when(pl.program_id(2) == 0)
    def _():
        for u in range(n_streams):
            put_scores(0, s_even, mx_even, u, qc)

    unroll = TILES_PER_ITER if n_kv % TILES_PER_ITER == 0 else 2

    def steps(base, head=False, tail=False):
        for t in range(0, unroll, 2):
            tile_step(base + t, s_even, mx_even, s_odd, mx_odd, p_even, p_odd,
                      first=head and t == 0)
            tile_step(base + t + 1, s_odd, mx_odd, s_even, mx_even, p_odd, p_even,
                      last=tail and t + 2 == unroll)

    def body(i, carry):
        steps(unroll * i)
        return carry

    n_groups = n_kv // unroll
    if n_groups == 1:
        steps(0, head=True, tail=True)
    else:
        steps(0, head=True)
        lax.fori_loop(1, n_groups - 1, body, 0)
        steps(n_kv - unroll, tail=True)
    for u in range(n_streams):
        weighted_values(n_kv - 1, p_odd, u)
    for u in range(n_streams):
        put_scores(0, s_even, mx_even, u, qc_next)

    lam = (jnp.exp(jnp.sum(lq1_ref[...] * lk1_ref[...], axis=-1, keepdims=True))
           - jnp.exp(jnp.sum(lq2_ref[...] * lk2_ref[...], axis=-1, keepdims=True)) + lambda_init)
    o = (acc_scr[0, :dv, :] / acc_scr[0, dv:dv + 1, :]
         - lam * (acc_scr[1, :dv, :] / acc_scr[1, dv:dv + 1, :]))
    o = o * lax.rsqrt(jnp.mean(o * o, axis=0, keepdims=True) + SUBLN_EPS)
    o = o * (sw_ref[...] * (1.0 - lambda_init))
    o_ref[...] = o.T.astype(o_ref.dtype)


def _diff_attn(q, k, vt, lq1, lk1, lq2, lk2, subln_w, lambda_init, tq, tk):
    B, S, QW = q.shape
    H = QW // LANES
    DV = vt.shape[1] // H
    ns = 2
    const = lambda b, h, i: (0, 0)
    return pl.pallas_call(
        functools.partial(_attn_kernel, tk=tk, lambda_init=lambda_init),
        grid=(B, H, S // tq),
        in_specs=[
            pl.BlockSpec((None, tq, LANES), lambda b, h, i: (b, i, h)),
            pl.BlockSpec((None, tq, LANES),
                         lambda b, h, i: (b, jnp.minimum(i + 1, S // tq - 1), h)),
            pl.BlockSpec((None, S, LANES), lambda b, h, i: (b, 0, h)),
            pl.BlockSpec((None, DV, S), lambda b, h, i: (b, h, 0)),
            pl.BlockSpec((1, HEAD), const),
            pl.BlockSpec((1, HEAD), const),
            pl.BlockSpec((1, HEAD), const),
            pl.BlockSpec((1, HEAD), const),
            pl.BlockSpec((DV, 1), const),
        ],
        out_specs=pl.BlockSpec((None, tq, DV), lambda b, h, i: (b, i, h)),
        out_shape=jax.ShapeDtypeStruct((B, S, H * DV), BF16),
        scratch_shapes=[pltpu.VMEM((ns, 1, tq), F32),
                        pltpu.VMEM((ns, 1, tq), F32),
                        pltpu.VMEM((ns, DV + SUM_ROWS, tq), F32),
                        pltpu.VMEM((ns, tk, tq), F32),
                        pltpu.VMEM((ns, tk, tq), F32),
                        pltpu.VMEM((ns, tk, tq), BF16),
                        pltpu.VMEM((ns, tk, tq), BF16),
                        pltpu.VMEM((ns, 1, tq), F32),
                        pltpu.VMEM((ns, 1, tq), F32)],
        compiler_params=pltpu.CompilerParams(
            dimension_semantics=("parallel", "parallel", "arbitrary"),
            vmem_limit_bytes=VMEM_LIMIT),
        name="diff_attn",
    )(q, q, k, vt, lq1, lk1, lq2, lk2, subln_w)


def _merge_kernel(x_ref, y_ref, bonus_ref, g_ref, ob_ref, nm_ref, wg_ref, lnw_ref, lnb_ref,
                  e_ref, woa_ref, wob_ref, wout_ref, o_ref):
    x = x_ref[...]
    D = x.shape[-1]
    h = _rms(x, nm_ref[...], RMS_EPS).astype(BF16)
    n_in = wg_ref.shape[1]
    gates = _sigmoid(jnp.dot(h, wg_ref[:, n_in - 2 * D:], preferred_element_type=F32))
    e = e_ref[...]
    y = y_ref[0] + y_ref[1]
    mean = _seg_sum(y, e) * (1.0 / HEAD)
    yc = y - mean
    var = _seg_sum(yc * yc, e) * (1.0 / HEAD)
    yn = yc * lax.rsqrt(var + GN_EPS) * lnw_ref[...] + lnb_ref[...]
    ya = _mm((yn + bonus_ref[...]) * g_ref[...], woa_ref[...])
    yb = jnp.dot(ob_ref[...], wob_ref[...], preferred_element_type=F32)
    merged = gates[:, :D] * ya + gates[:, D:] * yb
    o_ref[...] = x + _mm(merged, wout_ref[...])


def _merge(x, y, bonus, g, ob, norm_mix, w_g, ln_w, ln_b, e_seg, w_oa, w_ob, w_out, tm):
    B, S, D = x.shape
    W = bonus.shape[-1]
    const = lambda b, i: (0, 0)
    row = lambda b, i: (b, i, 0)
    return pl.pallas_call(
        _merge_kernel,
        grid=(B, S // tm),
        in_specs=[
            pl.BlockSpec((None, tm, D), row),
            pl.BlockSpec((2, None, tm, W), lambda b, i: (0, b, i, 0)),
            pl.BlockSpec((None, tm, W), row),
            pl.BlockSpec((None, tm, W), row),
            pl.BlockSpec((None, tm, W), row),
            pl.BlockSpec((1, D), const),
            pl.BlockSpec(w_g.shape, const),
            pl.BlockSpec((1, W), const),
            pl.BlockSpec((1, W), const),
            pl.BlockSpec(e_seg.shape, const),
            pl.BlockSpec((W, D), const),
            pl.BlockSpec((W, D), const),
            pl.BlockSpec((D, D), const),
        ],
        out_specs=pl.BlockSpec((None, tm, D), row),
        out_shape=jax.ShapeDtypeStruct((B, S, D), F32),
        compiler_params=pltpu.CompilerParams(
            dimension_semantics=("parallel", "parallel"), vmem_limit_bytes=VMEM_LIMIT),
        name="merge",
    )(x, y, bonus, g, ob, norm_mix, w_g, ln_w, ln_b, e_seg, w_oa, w_ob, w_out)


def _ffn_kernel(x_ref, p_ref, nf_ref, w1_ref, w2_ref, np_ref, wpg_ref, wpp_ref, nfin_ref, o_ref,
                *, n_chunk):
    x = x_ref[...]
    h = _rms(x, nf_ref[...], RMS_EPS).astype(BF16)
    ff = w1_ref.shape[1] // n_chunk
    acc = x
    for c in range(n_chunk):
        t = jnp.maximum(jnp.dot(h, w1_ref[:, c * ff:(c + 1) * ff], preferred_element_type=F32), 0.0)
        acc = acc + jnp.dot((t * t).astype(BF16), w2_ref[c * ff:(c + 1) * ff, :],
                            preferred_element_type=F32)
    x = acc
    h = _rms(x, np_ref[...], RMS_EPS).astype(BF16)
    gate = _sigmoid(jnp.dot(h, wpg_ref[...], preferred_element_type=F32))
    x = x + gate * _mm(p_ref[...], wpp_ref[...])
    o_ref[...] = x if nfin_ref is None else _rms(x, nfin_ref[...], RMS_EPS)


def _ffn_kernel_plain(x_ref, p_ref, nf_ref, w1_ref, w2_ref, np_ref, wpg_ref, wpp_ref, o_ref, *, n_chunk):
    _ffn_kernel(x_ref, p_ref, nf_ref, w1_ref, w2_ref, np_ref, wpg_ref, wpp_ref, None, o_ref,
                n_chunk=n_chunk)


def _ffn(x, p, norm_ffn, w1, w2, norm_ple, w_pg, w_pp, norm_final, tm, n_chunk):
    B, S, D = x.shape
    PD = p.shape[-1]
    FF = w1.shape[1]
    const = lambda b, i: (0, 0)
    row = lambda b, i: (b, i, 0)
    in_specs = [
        pl.BlockSpec((None, tm, D), row),
        pl.BlockSpec((None, tm, PD), row),
        pl.BlockSpec((1, D), const),
        pl.BlockSpec((D, FF), const, pipeline_mode=pl.Buffered(1)),
        pl.BlockSpec((FF, D), const, pipeline_mode=pl.Buffered(1)),
        pl.BlockSpec((1, D), const),
        pl.BlockSpec((D, D), const, pipeline_mode=pl.Buffered(1)),
        pl.BlockSpec((PD, D), const, pipeline_mode=pl.Buffered(1)),
    ]
    args = [x, p, norm_ffn, w1, w2, norm_ple, w_pg, w_pp]
    if norm_final is not None:
        in_specs.append(pl.BlockSpec((1, D), const))
        args.append(norm_final)
        body = functools.partial(_ffn_kernel, n_chunk=n_chunk)
    else:
        body = functools.partial(_ffn_kernel_plain, n_chunk=n_chunk)
    return pl.pallas_call(
        body,
        grid=(B, S // tm),
        in_specs=in_specs,
        out_specs=pl.BlockSpec((None, tm, D), row),
        out_shape=jax.ShapeDtypeStruct((B, S, D), F32),
        compiler_params=pltpu.CompilerParams(
            dimension_semantics=("parallel", "parallel"), vmem_limit_bytes=VMEM_LIMIT),
        name="ffn",
    )(*args)


def _block_diag2(w):
    z = jnp.zeros_like(w[0])
    return jnp.concatenate([jnp.concatenate([w[0], z], axis=1),
                            jnp.concatenate([z, w[1]], axis=1)], axis=0)


def _tiles(S):
    tm = min(TOKEN_TILE, S)
    tq = min(ATTN_Q_TILE, S)
    tk = min(ATTN_K_TILE, S // 2)
    nsub = min(SCAN_CHUNKS_PER_STEP, S // CHUNK)
    assert S % tm == 0 and S % tq == 0 and (S // tk) % 2 == 0 and S % (nsub * CHUNK) == 0, S
    return tm, tq, tk, nsub


def kernel(x, p, norm_mix, w_in, shift_mu_prev, shift_mu_next, rwkv_w0, rwkv_w2, rwkv_a0,
           rwkv_a2, rwkv_g2, rwkv_k_k, rwkv_k_a, rwkv_r_k, rwkv_ln_w, rwkv_ln_b, rwkv_w_o,
           da_lq1, da_lk1, da_lq2, da_lk2, da_subln_w, da_w_o, w_out, norm_ffn, w_ff1, w_ff2,
           norm_ple, w_ple_gate, w_ple_proj, norm_final):
    B, S, D = x.shape
    L = w_in.shape[0]
    W = rwkv_w0.shape[-1]
    n_r = shift_mu_prev.shape[-1]
    n_qk = 2 * da_w_o.shape[1]
    n_v = da_w_o.shape[1]
    cos_t, sin_t = _rope_tables(S)
    seg = jnp.arange(SEG_BLOCK) // HEAD
    e_seg = (seg[:, None] == seg[None, :]).astype(BF16)
    tm, tq, tk, nsub = _tiles(S)
    for i in range(L):
        lambda_init = 0.8 - 0.6 * math.exp(-0.3 * i)
        w_i = w_in[i].astype(BF16)
        u_r, q, k, v = _in_proj(
            x, norm_mix[i][None], w_i, w_i[:, n_r + n_qk:n_r + n_qk + n_v].T, n_r, n_qk,
            cos_t, sin_t, tm)
        r, vv, kk, g, bonus, lw, bb, kd = _rwkv_prep(
            u_r, shift_mu_prev[i][None], shift_mu_next[i][None],
            rwkv_w0[i].reshape(1, 2 * W), _block_diag2(rwkv_w2[i]).astype(BF16),
            rwkv_a0[i].reshape(1, 2 * W), _block_diag2(rwkv_a2[i]).astype(BF16),
            rwkv_g2[i].astype(BF16), rwkv_k_k[i][None], rwkv_k_a[i][None],
            rwkv_r_k[i].reshape(1, W), e_seg, tm)
        y = _rwkv_scan(r, vv, kk, lw, bb, kd, nsub=nsub)
        ob = _diff_attn(q, k, v, da_lq1[i][None], da_lk1[i][None], da_lq2[i][None],
                        da_lk2[i][None], da_subln_w[i][:, None], lambda_init, tq=tq, tk=tk)
        x = _merge(x, y, bonus, g, ob, norm_mix[i][None],
                   w_i, rwkv_ln_w[i][None], rwkv_ln_b[i][None], e_seg,
                   rwkv_w_o[i].astype(BF16), da_w_o[i].astype(BF16), w_out[i].astype(BF16), tm)
        x = _ffn(x, p[i], norm_ffn[i][None], w_ff1[i].astype(BF16), w_ff2[i].astype(BF16),
                 norm_ple[i][None], w_ple_gate[i].astype(BF16), w_ple_proj[i].astype(BF16),
                 norm_final[None] if i == L - 1 else None, tm, n_chunk=FFN_CHUNKS)
    return x
```

```python
import functools
import math

import jax
import jax.numpy as jnp
from jax import lax
from jax.experimental import pallas as pl
from jax.experimental.pallas import tpu as pltpu

F32 = jnp.float32
BF16 = jnp.bfloat16

HEAD = 64
LANES = 128
CHUNK = 64
RMS_EPS = 1e-6
GN_EPS = 64e-5
SUBLN_EPS = 1e-5
ROPE_THETA = 500000.0
ROPE_DIM = 16
Q_SCALE = HEAD ** -0.5 * math.log2(math.e)
DECAY_SCALE = math.exp(-0.5)
VMEM_LIMIT = 56 * 1024 * 1024
TOKEN_TILE = 512
ATTN_Q_TILE = 1024
ATTN_K_TILE = 512
SCAN_CHUNKS_PER_STEP = 16
FFN_CHUNKS = 4
TILES_PER_ITER = 8
SUM_ROWS = 16
HALO = 8
SEG_BLOCK = 256


def _rms(x, g, eps):
    return x * lax.rsqrt(jnp.mean(x * x, axis=-1, keepdims=True) + eps) * g


def _sigmoid(x):
    return 0.5 * jnp.tanh(0.5 * x) + 0.5


def _mm(a, b):
    return jnp.dot(a.astype(BF16), b.astype(BF16), preferred_element_type=F32)


def _split2(x):
    hi = x.astype(BF16)
    lo = (x - hi.astype(F32)).astype(BF16)
    return hi, lo


def _mm_exact_rhs(a, b_bf16):
    hi, lo = _split2(a)
    return (jnp.dot(hi, b_bf16, preferred_element_type=F32)
            + jnp.dot(lo, b_bf16, preferred_element_type=F32))


def _seg_sum(x, e):
    n = e.shape[0]
    return jnp.concatenate([_mm_exact_rhs(x[:, c:c + n], e) for c in range(0, x.shape[1], n)],
                           axis=1)


def _inproj_kernel(x_ref, g_ref, w_ref, wv_ref, cos_ref, sin_ref,
                   ur_ref, q_ref, k_ref, v_ref):
    n_r = ur_ref.shape[1]
    n_qk = 2 * q_ref.shape[1]
    h = _rms(x_ref[...], g_ref[...], RMS_EPS).astype(BF16)
    ur_ref[...] = jnp.dot(h, w_ref[:, :n_r], preferred_element_type=F32)
    v_ref[...] = lax.dot_general(wv_ref[...], h, (((1,), (1,)), ((), ())),
                                 preferred_element_type=F32).astype(BF16)
    cos_t = cos_ref[...]
    sin_t = sin_ref[...]
    lane = lax.broadcasted_iota(jnp.int32, cos_t.shape, 1)
    first_half = (lane & (HEAD - 1)) < (ROPE_DIM // 2)
    n_tiles = n_qk // LANES
    qk = jnp.dot(h, w_ref[:, n_r:n_r + n_qk], preferred_element_type=F32)
    for c in range(n_tiles):
        xq = qk[:, c * LANES:(c + 1) * LANES]
        partner = jnp.where(first_half,
                            pltpu.roll(xq, LANES - ROPE_DIM // 2, 1),
                            pltpu.roll(xq, ROPE_DIM // 2, 1))
        ro = xq * cos_t + partner * sin_t
        if c < n_tiles // 2:
            q_ref[:, c * LANES:(c + 1) * LANES] = (ro * Q_SCALE).astype(BF16)
        else:
            cc = c - n_tiles // 2
            k_ref[:, cc * LANES:(cc + 1) * LANES] = ro.astype(BF16)


def _rope_tables(S):
    pos = jnp.arange(S, dtype=F32)
    inv_freq = ROPE_THETA ** (-jnp.arange(0, ROPE_DIM, 2, dtype=F32) / ROPE_DIM)
    ang = pos[:, None] * inv_freq[None, :]
    cos8, sin8 = jnp.cos(ang), jnp.sin(ang)
    ones = jnp.ones((S, HEAD - ROPE_DIM), F32)
    zeros = jnp.zeros((S, HEAD - ROPE_DIM), F32)
    cos64 = jnp.concatenate([cos8, cos8, ones], axis=1)
    sin64 = jnp.concatenate([-sin8, sin8, zeros], axis=1)
    return jnp.tile(cos64, (1, 2)), jnp.tile(sin64, (1, 2))


def _in_proj(x, g, w_all, w_vt, nr, nqk, cos_t, sin_t, tm):
    B, S, D = x.shape
    nv = w_vt.shape[0]
    const = lambda b, i: (0, 0)
    row = lambda b, i: (b, i, 0)
    return pl.pallas_call(
        _inproj_kernel,
        grid=(B, S // tm),
        in_specs=[
            pl.BlockSpec((None, tm, D), row),
            pl.BlockSpec((1, D), const),
            pl.BlockSpec(w_all.shape, const),
            pl.BlockSpec((nv, D), const),
            pl.BlockSpec((tm, LANES), lambda b, i: (i, 0)),
            pl.BlockSpec((tm, LANES), lambda b, i: (i, 0)),
        ],
        out_specs=[
            pl.BlockSpec((None, tm, nr), row),
            pl.BlockSpec((None, tm, nqk // 2), row),
            pl.BlockSpec((None, tm, nqk // 2), row),
            pl.BlockSpec((None, nv, tm), lambda b, i: (b, 0, i)),
        ],
        out_shape=[
            jax.ShapeDtypeStruct((B, S, nr), F32),
            jax.ShapeDtypeStruct((B, S, nqk // 2), BF16),
            jax.ShapeDtypeStruct((B, S, nqk // 2), BF16),
            jax.ShapeDtypeStruct((B, nv, S), BF16),
        ],
        compiler_params=pltpu.CompilerParams(
            dimension_semantics=("parallel", "parallel"), vmem_limit_bytes=VMEM_LIMIT),
        name="in_proj",
    )(x, g, w_all, w_vt, cos_t, sin_t)


def _prep_kernel(u_ref, up_ref, un_ref, mup_ref, mun_ref, w0_ref, w2_ref, a0_ref, a2_ref,
                 g2_ref, kk_w_ref, ka_ref, rk_ref, e_ref,
                 r_ref, v_ref, kk_ref, g_ref, bonus_ref, lw_ref, b_ref, kd_ref):
    i = pl.program_id(1)
    n = pl.num_programs(1)
    u = u_ref[...]
    tm = u.shape[0]
    W = r_ref.shape[-1]
    row8 = lax.broadcasted_iota(jnp.int32, (HALO, 1), 0)
    prev_row = jnp.where(i > 0, up_ref[HALO - 1:HALO, :], 0.0)
    next_row = jnp.where(i < n - 1, un_ref[0:1, :], 0.0)
    rolled = pltpu.roll(u, 1, 0)
    u_prev = jnp.concatenate([jnp.where(row8 == 0, prev_row, rolled[:HALO]), rolled[HALO:]], axis=0)
    rolled = pltpu.roll(u, tm - 1, 0)
    u_next = jnp.concatenate([rolled[:tm - HALO],
                              jnp.where(row8 == HALO - 1, next_row, rolled[tm - HALO:])], axis=0)
    mup = mup_ref[...]
    mun = mun_ref[...]
    us = (1.0 - mup - mun) * u + mup * u_prev + mun * u_next

    r = us[:, 0:W]
    k = us[:, W:2 * W]
    v = us[:, 2 * W:3 * W]
    wd = us[:, 3 * W:3 * W + LANES]
    ad = us[:, 3 * W + LANES:3 * W + 2 * LANES]
    gd = us[:, 3 * W + 2 * LANES:3 * W + 3 * LANES]

    z = w0_ref[...] + _mm(jnp.tanh(wd), w2_ref[...])
    lw = (-0.5 * DECAY_SCALE) * jnp.tanh(0.5 * z) - 0.5 * DECAY_SCALE
    a = _sigmoid(a0_ref[...] + _mm(ad, a2_ref[...]))
    g_ref[...] = _mm(_sigmoid(gd), g2_ref[...]).astype(g_ref.dtype)

    e = e_ref[...]
    kraw = k * kk_w_ref[...]
    ss = _seg_sum(kraw * kraw, e)
    kk = kraw / jnp.maximum(jnp.sqrt(ss), 1e-12)
    ka = ka_ref[...]
    kd_sum = jnp.zeros_like(k)
    for d in range(2):
        a_d = a[:, d * W:(d + 1) * W]
        kd = k * (1.0 + (a_d - 1.0) * ka)
        kd_sum = kd_sum + kd
        lw_ref[d] = lw[:, d * W:(d + 1) * W]
        b_ref[d] = (kk * a_d).astype(b_ref.dtype)
        kd_ref[d] = kd.astype(kd_ref.dtype)
    c = _seg_sum(r * rk_ref[...] * kd_sum, e)
    r_ref[...] = r.astype(r_ref.dtype)
    v_ref[...] = v.astype(v_ref.dtype)
    kk_ref[...] = kk.astype(kk_ref.dtype)
    bonus_ref[...] = (c * v).astype(bonus_ref.dtype)


def _rwkv_prep(u_r, mu_prev, mu_next, w0, w2bd, a0, a2bd, g2, k_k, k_a, r_k, e_seg, tm):
    B, S, C = u_r.shape
    W = k_k.shape[-1]
    nblk8 = S // 8
    const = lambda b, i: (0, 0)
    row = lambda b, i: (b, i, 0)
    drow = lambda b, i: (0, b, i, 0)
    tok16 = jax.ShapeDtypeStruct((B, S, W), BF16)
    dtok = jax.ShapeDtypeStruct((2, B, S, W), F32)
    dtok16 = jax.ShapeDtypeStruct((2, B, S, W), BF16)
    return pl.pallas_call(
        _prep_kernel,
        grid=(B, S // tm),
        in_specs=[
            pl.BlockSpec((None, tm, C), row),
            pl.BlockSpec((None, 8, C), lambda b, i: (b, jnp.maximum(i * (tm // 8) - 1, 0), 0)),
            pl.BlockSpec((None, 8, C), lambda b, i: (b, jnp.minimum((i + 1) * (tm // 8), nblk8 - 1), 0)),
            pl.BlockSpec((1, C), const),
            pl.BlockSpec((1, C), const),
            pl.BlockSpec((1, 2 * W), const),
            pl.BlockSpec((LANES, 2 * W), const),
            pl.BlockSpec((1, 2 * W), const),
            pl.BlockSpec((LANES, 2 * W), const),
            pl.BlockSpec((LANES, W), const),
            pl.BlockSpec((1, W), const),
            pl.BlockSpec((1, W), const),
            pl.BlockSpec((1, W), const),
            pl.BlockSpec(e_seg.shape, const),
        ],
        out_specs=[pl.BlockSpec((None, tm, W), row)] * 5
                  + [pl.BlockSpec((2, None, tm, W), drow)] * 3,
        out_shape=[tok16, tok16, tok16, tok16, tok16, dtok, dtok16, dtok16],
        compiler_params=pltpu.CompilerParams(
            dimension_semantics=("parallel", "parallel"), vmem_limit_bytes=VMEM_LIMIT),
        name="rwkv_prep",
    )(u_r, u_r, u_r, mu_prev, mu_next, w0, w2bd, a0, a2bd, g2, k_k, k_a, r_k, e_seg)


def _scan_kernel(r_ref, v_ref, kk_ref, lw_ref, b_ref, kd_ref, y_ref,
                 t_scr, g_scr, h_scr, q_scr, yl_scr, *, nsub, n_pair, n_step):
    step = pl.program_id(0)
    last = pl.num_programs(0) - 2
    row_cur = jnp.minimum(step, last) // n_step
    step_prev = jnp.maximum(step - 1, 0)
    rev = (row_cur // n_pair) % 2
    rev_prev = ((step_prev // n_step) // n_pair) % 2
    sgn = 1 - 2 * rev
    C = CHUNK
    P2 = 2 * C

    @pl.when(step == 0)
    def _():
        t_scr[...] = jnp.zeros_like(t_scr)
        g_scr[...] = jnp.zeros_like(g_scr)
        h_scr[...] = jnp.zeros_like(h_scr)
        q_scr[...] = jnp.zeros_like(q_scr)
        yl_scr[...] = jnp.zeros_like(yl_scr)

    def slices(direction):
        out = []
        for s in range(nsub):
            ci = s + direction * (nsub - 1 - 2 * s)
            out.append(pl.ds(pl.multiple_of(ci * C, C), C))
        return out

    chunk_slices = slices(rev)
    chain_slices = slices(rev_prev)

    chain = {"t": jnp.where(step_prev % n_step == 0, 0.0, t_scr[...]), "next": 0}

    def chain_steps(n):
        for _ in range(n):
            s = chain["next"]
            if s == nsub:
                return
            t_cur = chain["t"]
            t16 = t_cur.astype(BF16)
            y_st = jnp.dot(q_scr[s], t16, preferred_element_type=F32) + yl_scr[s]
            chain["t"] = jnp.dot(g_scr[s], t16, preferred_element_type=F32) + h_scr[s]
            y_ref[chain_slices[s], :] = y_st[:C, :] + y_st[C:, :]
            chain["next"] = s + 1

    row = lax.broadcasted_iota(jnp.int32, (P2, P2), 0)
    col = lax.broadcasted_iota(jnp.int32, (P2, P2), 1)
    rt = row & (C - 1)
    ct = col & (C - 1)
    delta = (rt - ct) * sgn
    strict = delta > 0
    incl = delta >= 0
    eye = row == col
    ident = jnp.where(eye, 1.0, 0.0)
    blk8 = (rt >> 3) == (ct >> 3)
    blk16 = (rt >> 4) == (ct >> 4)
    blk32 = (rt >> 5) == (ct >> 5)
    in8 = strict & blk8
    in16 = strict & blk16 & ~blk8
    in32 = strict & blk32 & ~blk16
    in64 = strict & ~blk32
    lane = lax.broadcasted_iota(jnp.int32, (C, LANES), 1)
    head0 = lane < HEAD

    def stack(x):
        return jnp.concatenate([jnp.where(head0, x, 0.0), jnp.where(head0, 0.0, x)], axis=0)

    subs = range(nsub)
    sls = chunk_slices

    lws = [lw_ref[sl, :] for sl in sls]
    tots = [jnp.sum(lw, axis=0, keepdims=True) for lw in lws]
    tok = lax.broadcasted_iota(jnp.int32, (C, LANES), 0)
    cums = []
    for lw, tot in zip(lws, tots):
        pre = lw
        shift = 1
        while shift < C:
            pre = pre + jnp.where(tok >= shift, pltpu.roll(pre, shift, 0), 0.0)
            shift *= 2
        cums.append(jnp.where(rev == 1, tot - pre + lw, pre))
    a_st, r_st, b_st, k_st, bh_t, kh_t, v_st = [], [], [], [], [], [], []
    for s in subs:
        sl, lw, cum, tot = sls[s], lws[s], cums[s], tots[s]
        e_neg = jnp.exp(-cum)
        e_rem = jnp.exp(tot - cum)
        b = b_ref[sl, :]
        kd = kd_ref[sl, :]
        a_st.append(stack(kk_ref[sl, :] * -jnp.exp(cum - lw)).astype(BF16))
        r_st.append(stack(r_ref[sl, :] * jnp.exp(cum)))
        b_st.append(stack(b * e_neg).astype(BF16))
        k_st.append(stack(kd * e_neg).astype(BF16))
        bh_t.append(stack(b * e_rem).T.astype(BF16))
        kh_t.append(stack(kd * e_rem).T.astype(BF16))
        v_st.append(stack(v_ref[sl, :]).astype(BF16))

    nt = (((1,), (1,)), ((), ()))
    sc = [lax.dot_general(jnp.concatenate([a_st[s], r_st[s].astype(BF16)], axis=0),
                          jnp.concatenate([b_st[s], k_st[s]], axis=0), nt,
                          preferred_element_type=F32) for s in subs]
    per_stage = -(-nsub // 8)
    chain_steps(per_stage)
    l_ab = [x[:P2, :P2] for x in sc]
    l_ak = [jnp.where(strict, x[:P2, P2:], 0.0).astype(BF16) for x in sc]
    a_rb = [jnp.where(incl, x[P2:, :P2], 0.0).astype(BF16) for x in sc]
    a_rk = [jnp.where(incl, x[P2:, P2:], 0.0).astype(BF16) for x in sc]
    x_loc = [jnp.dot(l_ak[s], v_st[s], preferred_element_type=F32) for s in subs]
    chain_steps(per_stage)

    d8 = [jnp.where(in8, x, 0.0) for x in l_ab]
    m1 = [ident + x for x in d8]
    p1 = [_mm(x, x) for x in d8]
    chain_steps(per_stage)
    pm = [_mm(p1[s], jnp.concatenate([p1[s], m1[s]], axis=1)) for s in subs]
    m2 = [m1[s] + pm[s][:, P2:] for s in subs]
    m8 = [m2[s] + _mm(pm[s][:, :P2], m2[s]) for s in subs]
    chain_steps(per_stage)
    o16 = [jnp.where(in16, x, 0.0) for x in l_ab]
    om = [_mm(o16[s], m8[s]) for s in subs]
    m16 = [m8[s] + _mm(m8[s], om[s]) for s in subs]
    chain_steps(per_stage)
    o32 = [jnp.where(in32, x, 0.0) for x in l_ab]
    om = [_mm(o32[s], m16[s]) for s in subs]
    m32 = [(m16[s] + _mm(m16[s], om[s])).astype(BF16) for s in subs]
    chain_steps(per_stage)
    o64 = [jnp.where(in64, x, 0.0).astype(BF16) for x in l_ab]
    y1 = [jnp.dot(m32[s], jnp.concatenate([a_st[s], x_loc[s].astype(BF16)], axis=1),
                  preferred_element_type=F32) for s in subs]
    oy = [_mm(o64[s], y1[s]).astype(BF16) for s in subs]
    chain_steps(per_stage)
    wu16 = [(y1[s] + jnp.dot(m32[s], oy[s], preferred_element_type=F32)).astype(BF16)
            for s in subs]
    chain_steps(per_stage)
    zero_blk = jnp.zeros((P2, LANES), BF16)
    fin = [jnp.dot(jnp.concatenate([jnp.concatenate([a_rb[s], a_rk[s]], axis=1),
                                    jnp.concatenate([bh_t[s], kh_t[s]], axis=1)], axis=0),
                   jnp.concatenate([wu16[s], jnp.concatenate([zero_blk, v_st[s]], axis=1)], axis=0),
                   preferred_element_type=F32) for s in subs]
    chain_steps(nsub)
    t_scr[...] = chain["t"]
    for s in subs:
        q_scr[s] = (r_st[s] + fin[s][:P2, :LANES]).astype(BF16)
        yl_scr[s] = fin[s][:P2, LANES:]
        g_scr[s] = (jnp.where(eye, jnp.exp(tots[s]), 0.0) + fin[s][P2:, :LANES]).astype(BF16)
        h_scr[s] = fin[s][P2:, LANES:]


def _rwkv_scan(r, v, kk, lw, bb, kd, nsub):
    B, S, W = r.shape
    n_pair = W // LANES
    T = nsub * CHUNK
    n_step = S // T

    n_total = B * 2 * n_pair * n_step

    def locate(t):
        g = t // n_step
        c = t % n_step
        d = (g // n_pair) % 2
        return d, g // (2 * n_pair), c + d * (n_step - 1 - 2 * c), g % n_pair

    def shared(t):
        _, b, blk, p = locate(jnp.minimum(t, n_total - 1))
        return (b, blk, p)

    def per_dir(t):
        return locate(jnp.minimum(t, n_total - 1))

    def out_map(t):
        return locate(jnp.maximum(t - 1, 0))

    mat = pltpu.VMEM((nsub, LANES, LANES), F32)
    mat16 = pltpu.VMEM((nsub, LANES, LANES), BF16)
    return pl.pallas_call(
        functools.partial(_scan_kernel, nsub=nsub, n_pair=n_pair, n_step=n_step),
        grid=(n_total + 1,),
        in_specs=[pl.BlockSpec((None, T, LANES), shared)] * 3
                 + [pl.BlockSpec((None, None, T, LANES), per_dir)] * 3,
        out_specs=pl.BlockSpec((None, None, T, LANES), out_map),
        out_shape=jax.ShapeDtypeStruct((2, B, S, W), F32),
        scratch_shapes=[pltpu.VMEM((LANES, LANES), F32), mat16, mat, mat16, mat],
        compiler_params=pltpu.CompilerParams(
            dimension_semantics=("arbitrary",), vmem_limit_bytes=VMEM_LIMIT),
        name="rwkv_scan",
    )(r, v, kk, lw, bb, kd)


def _attn_kernel(q_ref, qn_ref, k_ref, vt_ref, lq1_ref, lk1_ref, lq2_ref, lk2_ref, sw_ref, o_ref,
                 m_scr, alpha_scr, acc_scr, s_even, s_odd, p_even, p_odd, mx_even, mx_odd,
                 *, tk, lambda_init):
    n_kv = k_ref.shape[0] // tk
    dv = vt_ref.shape[0]
    n_streams = 2
    lane = lax.broadcasted_iota(jnp.int32, q_ref.shape, 1)

    def component_queries(ref):
        q = ref[...]
        zero = jnp.zeros_like(q)
        return [jnp.where(lane < HEAD, q, zero), jnp.where(lane < HEAD, zero, q)]

    qc = component_queries(q_ref)
    m_scr[...] = jnp.full(m_scr.shape, -jnp.inf, F32)
    acc_scr[...] = jnp.zeros(acc_scr.shape, F32)

    def scores(j, u, queries):
        ks = k_ref[pl.ds(pl.multiple_of(j * tk, tk), tk), :]
        return lax.dot_general(ks, queries[u], (((1,), (1,)), ((), ())),
                               preferred_element_type=F32)

    ones_rows = jnp.ones((SUM_ROWS, tk), BF16)

    def weighted_values(j, p_ref, u):
        vts = vt_ref[:, pl.ds(pl.multiple_of(j * tk, tk), tk)]
        vts = jnp.concatenate([vts, ones_rows], axis=0)
        acc_scr[u] = alpha_scr[u] * acc_scr[u] + jnp.dot(vts, p_ref[u],
                                                         preferred_element_type=F32)

    def put_scores(j, s_ref, mx_ref, u, queries):
        s = scores(j, u, queries)
        s_ref[u] = s
        mx_ref[u] = jnp.max(s, axis=0, keepdims=True)

    def tile_step(j, s_cur, mx_cur, s_nxt, mx_nxt, p_cur, p_prev, first=False, last=False):
        for u in range(n_streams):
            if not first:
                weighted_values(j - 1, p_prev, u)
            if not last:
                put_scores(j + 1, s_nxt, mx_nxt, u, qc)
        for u in range(n_streams):
            m_old = m_scr[u]
            m_new = jnp.maximum(m_old, mx_cur[u])
            alpha = jnp.exp2(m_old - m_new)
            p = jnp.exp2(s_cur[u] - m_new)
            p_cur[u] = p.astype(BF16)
            alpha_scr[u] = alpha
            m_scr[u] = m_new

    qc_next = component_queries(qn_ref)

    @pl.when(pl.program_id(2) == 0)
    def _():
        for u in range(n_streams):
            put_scores(0, s_even, mx_even, u, qc)

    unroll = TILES_PER_ITER if n_kv % TILES_PER_ITER == 0 else 2

    def steps(base, head=False, tail=False):
        for t in range(0, unroll, 2):
            tile_step(base + t, s_even, mx_even, s_odd, mx_odd, p_even, p_odd,
                      first=head and t == 0)
            tile_step(base + t + 1, s_odd, mx_odd, s_even, mx_even, p_odd, p_even,
                      last=tail and t + 2 == unroll)

    def body(i, carry):
        steps(unroll * i)
        return carry

    n_groups = n_kv // unroll
    if n_groups == 1:
        steps(0, head=True, tail=True)
    else:
        steps(0, head=True)
        lax.fori_loop(1, n_groups - 1, body, 0)
        steps(n_kv - unroll, tail=True)
    for u in range(n_streams):
        weighted_values(n_kv - 1, p_odd, u)
    for u in range(n_streams):
        put_scores(0, s_even, mx_even, u, qc_next)

    lam = (jnp.exp(jnp.sum(lq1_ref[...] * lk1_ref[...], axis=-1, keepdims=True))
           - jnp.exp(jnp.sum(lq2_ref[...] * lk2_ref[...], axis=-1, keepdims=True)) + lambda_init)
    o = (acc_scr[0, :dv, :] / acc_scr[0, dv:dv + 1, :]
         - lam * (acc_scr[1, :dv, :] / acc_scr[1, dv:dv + 1, :]))
    o = o * lax.rsqrt(jnp.mean(o * o, axis=0, keepdims=True) + SUBLN_EPS)
    o = o * (sw_ref[...] * (1.0 - lambda_init))
    o_ref[...] = o.T.astype(o_ref.dtype)


def _diff_attn(q, k, vt, lq1, lk1, lq2, lk2, subln_w, lambda_init, tq, tk):
    B, S, QW = q.shape
    H = QW // LANES
    DV = vt.shape[1] // H
    ns = 2
    const = lambda b, h, i: (0, 0)
    return pl.pallas_call(
        functools.partial(_attn_kernel, tk=tk, lambda_init=lambda_init),
        grid=(B, H, S // tq),
        in_specs=[
            pl.BlockSpec((None, tq, LANES), lambda b, h, i: (b, i, h)),
            pl.BlockSpec((None, tq, LANES),
                         lambda b, h, i: (b, jnp.minimum(i + 1, S // tq - 1), h)),
            pl.BlockSpec((None, S, LANES), lambda b, h, i: (b, 0, h)),
            pl.BlockSpec((None, DV, S), lambda b, h, i: (b, h, 0)),
            pl.BlockSpec((1, HEAD), const),
            pl.BlockSpec((1, HEAD), const),
            pl.BlockSpec((1, HEAD), const),
            pl.BlockSpec((1, HEAD), const),
            pl.BlockSpec((DV, 1), const),
        ],
        out_specs=pl.BlockSpec((None, tq, DV), lambda b, h, i: (b, i, h)),
        out_shape=jax.ShapeDtypeStruct((B, S, H * DV), BF16),
        scratch_shapes=[pltpu.VMEM((ns, 1, tq), F32),
                        pltpu.VMEM((ns, 1, tq), F32),
                        pltpu.VMEM((ns, DV + SUM_ROWS, tq), F32),
                        pltpu.VMEM((ns, tk, tq), F32),
                        pltpu.VMEM((ns, tk, tq), F32),
                        pltpu.VMEM((ns, tk, tq), BF16),
                        pltpu.VMEM((ns, tk, tq), BF16),
                        pltpu.VMEM((ns, 1, tq), F32),
                        pltpu.VMEM((ns, 1, tq), F32)],
        compiler_params=pltpu.CompilerParams(
            dimension_semantics=("parallel", "parallel", "arbitrary"),
            vmem_limit_bytes=VMEM_LIMIT),
        name="diff_attn",
    )(q, q, k, vt, lq1, lk1, lq2, lk2, subln_w)


def _merge_kernel(x_ref, y_ref, bonus_ref, g_ref, ob_ref, nm_ref, wg_ref, lnw_ref, lnb_ref,
                  e_ref, woa_ref, wob_ref, wout_ref, o_ref):
    x = x_ref[...]
    D = x.shape[-1]
    h = _rms(x, nm_ref[...], RMS_EPS).astype(BF16)
    n_in = wg_ref.shape[1]
    gates = _sigmoid(jnp.dot(h, wg_ref[:, n_in - 2 * D:], preferred_element_type=F32))
    e = e_ref[...]
    y = y_ref[0] + y_ref[1]
    mean = _seg_sum(y, e) * (1.0 / HEAD)
    yc = y - mean
    var = _seg_sum(yc * yc, e) * (1.0 / HEAD)
    yn = yc * lax.rsqrt(var + GN_EPS) * lnw_ref[...] + lnb_ref[...]
    ya = _mm((yn + bonus_ref[...]) * g_ref[...], woa_ref[...])
    yb = jnp.dot(ob_ref[...], wob_ref[...], preferred_element_type=F32)
    merged = gates[:, :D] * ya + gates[:, D:] * yb
    o_ref[...] = x + _mm(merged, wout_ref[...])


def _merge(x, y, bonus, g, ob, norm_mix, w_g, ln_w, ln_b, e_seg, w_oa, w_ob, w_out, tm):
    B, S, D = x.shape
    W = bonus.shape[-1]
    const = lambda b, i: (0, 0)
    row = lambda b, i: (b, i, 0)
    return pl.pallas_call(
        _merge_kernel,
        grid=(B, S // tm),
        in_specs=[
            pl.BlockSpec((None, tm, D), row),
            pl.BlockSpec((2, None, tm, W), lambda b, i: (0, b, i, 0)),
            pl.BlockSpec((None, tm, W), row),
            pl.BlockSpec((None, tm, W), row),
            pl.BlockSpec((None, tm, W), row),
            pl.BlockSpec((1, D), const),
            pl.BlockSpec(w_g.shape, const),
            pl.BlockSpec((1, W), const),
            pl.BlockSpec((1, W), const),
            pl.BlockSpec(e_seg.shape, const),
            pl.BlockSpec((W, D), const),
            pl.BlockSpec((W, D), const),
            pl.BlockSpec((D, D), const),
        ],
        out_specs=pl.BlockSpec((None, tm, D), row),
        out_shape=jax.ShapeDtypeStruct((B, S, D), F32),
        compiler_params=pltpu.CompilerParams(
            dimension_semantics=("parallel", "parallel"), vmem_limit_bytes=VMEM_LIMIT,
            allow_input_fusion=[False] * 10 + [True] * 3),
        name="merge",
    )(x, y, bonus, g, ob, norm_mix, w_g, ln_w, ln_b, e_seg, w_oa, w_ob, w_out)


def _ffn_kernel(x_ref, p_ref, nf_ref, w1_ref, w2_ref, np_ref, wpg_ref, wpp_ref, nfin_ref, o_ref,
                *, n_chunk):
    x = x_ref[...]
    h = _rms(x, nf_ref[...], RMS_EPS).astype(BF16)
    ff = w1_ref.shape[1] // n_chunk
    acc = x
    for c in range(n_chunk):
        t = jnp.maximum(jnp.dot(h, w1_ref[:, c * ff:(c + 1) * ff], preferred_element_type=F32), 0.0)
        acc = acc + jnp.dot((t * t).astype(BF16), w2_ref[c * ff:(c + 1) * ff, :],
                            preferred_element_type=F32)
    x = acc
    h = _rms(x, np_ref[...], RMS_EPS).astype(BF16)
    gate = _sigmoid(jnp.dot(h, wpg_ref[...], preferred_element_type=F32))
    x = x + gate * _mm(p_ref[...], wpp_ref[...])
    o_ref[...] = x if nfin_ref is None else _rms(x, nfin_ref[...], RMS_EPS)


def _ffn_kernel_plain(x_ref, p_ref, nf_ref, w1_ref, w2_ref, np_ref, wpg_ref, wpp_ref, o_ref, *, n_chunk):
    _ffn_kernel(x_ref, p_ref, nf_ref, w1_ref, w2_ref, np_ref, wpg_ref, wpp_ref, None, o_ref,
                n_chunk=n_chunk)


def _ffn(x, p, norm_ffn, w1, w2, norm_ple, w_pg, w_pp, norm_final, tm, n_chunk):
    B, S, D = x.shape
    PD = p.shape[-1]
    FF = w1.shape[1]
    const = lambda b, i: (0, 0)
    row = lambda b, i: (b, i, 0)
    in_specs = [
        pl.BlockSpec((None, tm, D), row),
        pl.BlockSpec((None, tm, PD), row),
        pl.BlockSpec((1, D), const),
        pl.BlockSpec((D, FF), const, pipeline_mode=pl.Buffered(1)),
        pl.BlockSpec((FF, D), const, pipeline_mode=pl.Buffered(1)),
        pl.BlockSpec((1, D), const),
        pl.BlockSpec((D, D), const, pipeline_mode=pl.Buffered(1)),
        pl.BlockSpec((PD, D), const, pipeline_mode=pl.Buffered(1)),
    ]
    args = [x, p, norm_ffn, w1, w2, norm_ple, w_pg, w_pp]
    if norm_final is not None:
        in_specs.append(pl.BlockSpec((1, D), const))
        args.append(norm_final)
        body = functools.partial(_ffn_kernel, n_chunk=n_chunk)
    else:
        body = functools.partial(_ffn_kernel_plain, n_chunk=n_chunk)
    return pl.pallas_call(
        body,
        grid=(B, S // tm),
        in_specs=in_specs,
        out_specs=pl.BlockSpec((None, tm, D), row),
        out_shape=jax.ShapeDtypeStruct((B, S, D), F32),
        compiler_params=pltpu.CompilerParams(
            dimension_semantics=("parallel", "parallel"), vmem_limit_bytes=VMEM_LIMIT,
            allow_input_fusion=[a.dtype == BF16 and a.ndim == 2 and a.shape[0] > 1 for a in args]),
        name="ffn",
    )(*args)


def _block_diag2(w):
    z = jnp.zeros_like(w[0])
    return jnp.concatenate([jnp.concatenate([w[0], z], axis=1),
                            jnp.concatenate([z, w[1]], axis=1)], axis=0)


def _tiles(S):
    tm = min(TOKEN_TILE, S)
    tq = min(ATTN_Q_TILE, S)
    tk = min(ATTN_K_TILE, S // 2)
    nsub = min(SCAN_CHUNKS_PER_STEP, S // CHUNK)
    assert S % tm == 0 and S % tq == 0 and (S // tk) % 2 == 0 and S % (nsub * CHUNK) == 0, S
    return tm, tq, tk, nsub


def kernel(x, p, norm_mix, w_in, shift_mu_prev, shift_mu_next, rwkv_w0, rwkv_w2, rwkv_a0,
           rwkv_a2, rwkv_g2, rwkv_k_k, rwkv_k_a, rwkv_r_k, rwkv_ln_w, rwkv_ln_b, rwkv_w_o,
           da_lq1, da_lk1, da_lq2, da_lk2, da_subln_w, da_w_o, w_out, norm_ffn, w_ff1, w_ff2,
           norm_ple, w_ple_gate, w_ple_proj, norm_final):
    B, S, D = x.shape
    L = w_in.shape[0]
    W = rwkv_w0.shape[-1]
    n_r = shift_mu_prev.shape[-1]
    n_qk = 2 * da_w_o.shape[1]
    n_v = da_w_o.shape[1]
    cos_t, sin_t = _rope_tables(S)
    seg = jnp.arange(SEG_BLOCK) // HEAD
    e_seg = (seg[:, None] == seg[None, :]).astype(BF16)
    tm, tq, tk, nsub = _tiles(S)
    for i in range(L):
        lambda_init = 0.8 - 0.6 * math.exp(-0.3 * i)
        w_i = w_in[i].astype(BF16)
        u_r, q, k, v = _in_proj(
            x, norm_mix[i][None], w_i, w_i[:, n_r + n_qk:n_r + n_qk + n_v].T, n_r, n_qk,
            cos_t, sin_t, tm)
        r, vv, kk, g, bonus, lw, bb, kd = _rwkv_prep(
            u_r, shift_mu_prev[i][None], shift_mu_next[i][None],
            rwkv_w0[i].reshape(1, 2 * W), _block_diag2(rwkv_w2[i]).astype(BF16),
            rwkv_a0[i].reshape(1, 2 * W), _block_diag2(rwkv_a2[i]).astype(BF16),
            rwkv_g2[i].astype(BF16), rwkv_k_k[i][None], rwkv_k_a[i][None],
            rwkv_r_k[i].reshape(1, W), e_seg, tm)
        y = _rwkv_scan(r, vv, kk, lw, bb, kd, nsub=nsub)
        ob = _diff_attn(q, k, v, da_lq1[i][None], da_lk1[i][None], da_lq2[i][None],
                        da_lk2[i][None], da_subln_w[i][:, None], lambda_init, tq=tq, tk=tk)
        x = _merge(x, y, bonus, g, ob, norm_mix[i][None],
                   w_i, rwkv_ln_w[i][None], rwkv_ln_b[i][None], e_seg,
                   rwkv_w_o[i].astype(BF16), da_w_o[i].astype(BF16), w_out[i].astype(BF16), tm)
        x = _ffn(x, p[i], norm_ffn[i][None], w_ff1[i].astype(BF16), w_ff2[i].astype(BF16),
                 norm_ple[i][None], w_ple_gate[i].astype(BF16), w_ple_proj[i].astype(BF16),
                 norm_final[None] if i == L - 1 else None, tm, n_chunk=FFN_CHUNKS)
    return x
```
